```python
import jax, jax.numpy as jnp
from jax import lax
import numpy as np

D_MODEL = 1024
BATCH = 4
SEQ = 4096
DEPTH = 1

HGRN_HEADS = 4
HGRN_KEY_DIM = 128
HGRN_VAL_DIM = (D_MODEL // 2) // HGRN_HEADS
HGRN_KEY_WIDTH = HGRN_HEADS * HGRN_KEY_DIM
HGRN_WIDTH = HGRN_HEADS * HGRN_VAL_DIM
CHUNK = 64
POOL_WINDOWS = (2, 4, 8, 16)
POOL_GROUPS = len(POOL_WINDOWS)
POOL_WIDTH = D_MODEL - HGRN_WIDTH
POOL_GROUP_DIM = POOL_WIDTH // POOL_GROUPS
POOL_MAX_W = max(POOL_WINDOWS)
MIX_WIDTH = HGRN_WIDTH + POOL_WIDTH
IN_WIDTH = 2 * HGRN_KEY_WIDTH + 2 * HGRN_WIDTH + POOL_WIDTH
MEM_LEN = 256
XATTN_HEADS = 4
XATTN_HEAD_DIM = D_MODEL // XATTN_HEADS
D_FF = 4 * D_MODEL
EPS = 1e-6

kernel_name = "hymba_style_hgrn2_pool_hybrid"


def rmsnorm(x, g):
    xf = x.astype(jnp.float32)
    y = xf * lax.rsqrt(jnp.mean(xf * xf, axis=-1, keepdims=True) + EPS)
    return (y * g.astype(jnp.float32)).astype(x.dtype)


def hgrn2_chunkwise(q, k, v, log_f):
    B, S, H, DK = q.shape
    DV = v.shape[-1]
    n = S // CHUNK

    def to_chunks(a):
        return a.reshape(B, n, CHUNK, H, a.shape[-1]).transpose(1, 0, 3, 2, 4)

    qc, kc, vc, fc = to_chunks(q), to_chunks(k), to_chunks(v), to_chunks(log_f)
    causal = jnp.tril(jnp.ones((CHUNK, CHUNK), dtype=bool))

    def step(state, inp):
        q_, k_, v_, lf = inp
        b = jnp.cumsum(lf, axis=2)
        diff = b[:, :, :, None, :] - b[:, :, None, :, :]
        decay = jnp.exp(jnp.where(causal[None, None, :, :, None], diff, -jnp.inf))
        scores = jnp.einsum('bhtk,bhsk,bhtsk->bhts', q_, k_, decay)
        o_intra = jnp.einsum('bhts,bhsv->bhtv', scores, v_)
        o_inter = jnp.einsum('bhtk,bhkv->bhtv', q_ * jnp.exp(b), state)
        b_last = b[:, :, -1:, :]
        k_dec = k_ * jnp.exp(b_last - b)
        state = jnp.exp(b_last[:, :, 0, :])[..., None] * state + jnp.einsum('bhsk,bhsv->bhkv', k_dec, v_)
        return state, o_intra + o_inter

    s0 = jnp.zeros((B, H, DK, DV), jnp.float32)
    _, o = lax.scan(step, s0, (qc, kc, vc, fc))
    return o.transpose(1, 0, 3, 2, 4).reshape(B, S, H, DV)


def multiscale_pool(p, w_pool, pool_scale):
    B, S, _ = p.shape
    pg = p.astype(jnp.float32).reshape(B, S, POOL_GROUPS, POOL_GROUP_DIM)
    cs = jnp.cumsum(pg, axis=1)
    cs = jnp.pad(cs, ((0, 0), (POOL_MAX_W, 0), (0, 0), (0, 0)))
    pos = (jnp.arange(S) + 1)
    outs = []
    for gi, w in enumerate(POOL_WINDOWS):
        win = cs[:, POOL_MAX_W:, gi] - cs[:, POOL_MAX_W - w:POOL_MAX_W - w + S, gi]
        cnt = jnp.minimum(pos, w).astype(jnp.float32)[None, :, None]
        outs.append(win / cnt - pg[:, :, gi])
    pooled = jnp.stack(outs, axis=2)
    y = jnp.einsum('bsgc,gcd->bsgd', pooled, w_pool.astype(jnp.float32))
    return y.reshape(B, S, POOL_WIDTH) * pool_scale.astype(jnp.float32)


def setup_inputs(seed: int = 0) -> dict:
    key = jax.random.key(seed)
    ks = jax.random.split(key, 24)
    f32 = jnp.float32

    def dense(k, shape, fan_in):
        return jax.random.normal(k, shape, f32) * (fan_in ** -0.5)

    def gain(k, shape):
        return 1.0 + 0.02 * jax.random.normal(k, shape, f32)

    return {
        "x": jax.random.normal(ks[0], (BATCH, SEQ, D_MODEL), f32),
        "mem": jax.random.normal(ks[1], (BATCH, MEM_LEN, D_MODEL), f32),
        "norm_mix_g": gain(ks[2], (DEPTH, D_MODEL)),
        "w_in": dense(ks[3], (DEPTH, D_MODEL, IN_WIDTH), D_MODEL),
        "lb_logits": 0.5 * jax.random.normal(ks[4], (DEPTH + 1, HGRN_KEY_WIDTH), f32),
        "hgrn_norm_g": gain(ks[5], (DEPTH, HGRN_HEADS, HGRN_VAL_DIM)),
        "w_pool": dense(ks[6], (DEPTH, POOL_GROUPS, POOL_GROUP_DIM, POOL_GROUP_DIM), POOL_GROUP_DIM),
        "pool_scale": gain(ks[7], (DEPTH, POOL_WIDTH)),
        "w_out": dense(ks[8], (DEPTH, MIX_WIDTH, D_MODEL), MIX_WIDTH),
        "norm_x_g": gain(ks[9], (DEPTH, D_MODEL)),
        "norm_mem_g": gain(ks[10], (DEPTH, D_MODEL)),
        "w_xq": dense(ks[11], (DEPTH, D_MODEL, XATTN_HEADS, XATTN_HEAD_DIM), D_MODEL),
        "w_xk": dense(ks[12], (DEPTH, D_MODEL, XATTN_HEADS, XATTN_HEAD_DIM), D_MODEL),
        "w_xv": dense(ks[13], (DEPTH, D_MODEL, XATTN_HEADS, XATTN_HEAD_DIM), D_MODEL),
        "w_xo": dense(ks[14], (DEPTH, XATTN_HEADS, XATTN_HEAD_DIM, D_MODEL), D_MODEL),
        "norm_ffn_g": gain(ks[15], (DEPTH, D_MODEL)),
        "w_ff1": dense(ks[16], (DEPTH, D_MODEL, D_FF), D_MODEL),
        "w_ff2": dense(ks[17], (DEPTH, D_FF, D_MODEL), D_FF),
        "final_norm_g": gain(ks[18], (D_MODEL,)),
    }


def reference(x, mem, norm_mix_g, w_in, lb_logits, hgrn_norm_g, w_pool, pool_scale, w_out,
              norm_x_g, norm_mem_g, w_xq, w_xk, w_xv, w_xo, norm_ffn_g, w_ff1, w_ff2, final_norm_g):
    B, S, _ = x.shape
    f32 = jnp.float32
    lower_bounds = jnp.cumsum(jax.nn.softmax(lb_logits.astype(f32), axis=0), axis=0)
    split_at = [HGRN_KEY_WIDTH, 2 * HGRN_KEY_WIDTH,
                2 * HGRN_KEY_WIDTH + HGRN_WIDTH, 2 * HGRN_KEY_WIDTH + 2 * HGRN_WIDTH]
    for l in range(DEPTH):
        h = rmsnorm(x, norm_mix_g[l])
        z = jnp.einsum('bsd,de->bse', h, w_in[l])
        q_pre, f_pre, i_pre, g_pre, p = jnp.split(z, split_at, axis=-1)

        lb = lower_bounds[l]
        f = lb + (1.0 - lb) * jax.nn.sigmoid(f_pre.astype(f32))
        log_f = jnp.log(f)
        k = 1.0 - f
        q = jax.nn.silu(q_pre.astype(f32))
        hs = lambda a, d: a.reshape(B, S, HGRN_HEADS, d)
        o = hgrn2_chunkwise(hs(q, HGRN_KEY_DIM), hs(k, HGRN_KEY_DIM),
                            hs(i_pre.astype(f32), HGRN_VAL_DIM), hs(log_f, HGRN_KEY_DIM))
        o = o * lax.rsqrt(jnp.mean(o * o, axis=-1, keepdims=True) + EPS) * hgrn_norm_g[l].astype(f32)
        o_a = o.reshape(B, S, HGRN_WIDTH) * jax.nn.silu(g_pre.astype(f32))

        o_b = multiscale_pool(p, w_pool[l], pool_scale[l])

        mixed = jnp.concatenate([o_a, o_b], axis=-1).astype(x.dtype)
        x = x + jnp.einsum('bse,ed->bsd', mixed, w_out[l])

        hq = rmsnorm(x, norm_x_g[l])
        hm = rmsnorm(mem, norm_mem_g[l])
        xq = jnp.einsum('bsd,dhe->bshe', hq, w_xq[l])
        xk = jnp.einsum('bmd,dhe->bmhe', hm, w_xk[l])
        xv = jnp.einsum('bmd,dhe->bmhe', hm, w_xv[l])
        scores = jnp.einsum('bshe,bmhe->bhsm', xq, xk).astype(f32) * (XATTN_HEAD_DIM ** -0.5)
        probs = jax.nn.softmax(scores, axis=-1).astype(x.dtype)
        att = jnp.einsum('bhsm,bmhe->bshe', probs, xv)
        x = x + jnp.einsum('bshe,hed->bsd', att, w_xo[l])

        hf = rmsnorm(x, norm_ffn_g[l])
        u = jnp.square(jax.nn.relu(jnp.einsum('bsd,df->bsf', hf, w_ff1[l])))
        x = x + jnp.einsum('bsf,fd->bsd', u, w_ff2[l])
    return rmsnorm(x, final_norm_g)
```

```python
import functools

import jax
import jax.numpy as jnp
from jax import lax
from jax.experimental import pallas as pl
from jax.experimental.pallas import tpu as pltpu

D_MODEL = 1024
HEADS = 4
DK = 128
KW = HEADS * DK
CHUNK = 64
SUB = 16
NSUB = CHUNK // SUB
POOL_WINDOWS = (2, 4, 8, 16)
POOL_HIST = 16
IN_WIDTH = 5 * KW
MEM_LEN = 256
XHEADS = 4
XDIM = 256
D_FF = 4096
EPS = 1e-6

TS_MIX = 256
TM_ATT = 512
TM_FFN = 512
FF_BLOCK = 1024
VMEM_LIMIT = 56 * 1024 * 1024

F32 = jnp.float32
BF16 = jnp.bfloat16


def _rms(x, g):
    return x * lax.rsqrt(jnp.mean(x * x, axis=-1, keepdims=True) + EPS) * g


def _dot(a, b):
    return jnp.dot(a, b, preferred_element_type=F32)


def _dot_nt(a, b):
    return lax.dot_general(a, b, (((1,), (1,)), ((), ())), preferred_element_type=F32)


def _dot_tn(a, b):
    return lax.dot_general(a, b, (((0,), (0,)), ((), ())), preferred_element_type=F32)


def _sigmoid(x):
    return 1.0 / (1.0 + jnp.exp(-x))


def _hgrn_chunk_head(qp, fp, v, gp, lb, hg, st_ref, h):
    row = lax.broadcasted_iota(jnp.int32, (CHUNK, DK), 0)
    row_sub = jnp.bitwise_and(row, SUB - 1)
    lane64 = lax.broadcasted_iota(jnp.int32, (SUB, CHUNK), 1)

    sig = _sigmoid(fp)
    f = lb + (1.0 - lb) * sig
    lf = jnp.log(f)
    kk = 1.0 - f
    q = qp * _sigmoid(qp)

    c = lf
    for sh in (1, 2, 4, 8):
        c = c + jnp.where(row_sub >= sh, pltpu.roll(c, sh, 0), 0.0)
    d_tot = [c[SUB * j + SUB - 1:SUB * j + SUB, :] for j in range(NSUB)]
    p_start = [jnp.zeros((1, DK), F32)]
    for j in range(1, NSUB):
        p_start.append(p_start[-1] + d_tot[j - 1])
    b_last = p_start[-1] + d_tot[-1]

    def per_sub(vals):
        out = vals[NSUB - 1]
        for j in range(NSUB - 2, -1, -1):
            out = jnp.where(row < SUB * (j + 1), vals[j], out)
        return out

    b = c + per_sub(p_start)
    d_row = per_sub(d_tot)

    st = st_ref[h]
    v_bf = v.astype(BF16)

    o = _dot_nt((q * jnp.exp(b)).astype(BF16), st.astype(BF16))

    k_dec = kk * jnp.exp(b_last - b)
    st_ref[h] = st * jnp.exp(b_last) + _dot_tn(v_bf, k_dec.astype(BF16))

    k_hat = kk * jnp.exp(d_row - c)
    q_parts, k_parts = [], []
    for j in range(NSUB - 1):
        lo = SUB * (j + 1)
        qt = q[lo:, :] * jnp.exp(jnp.minimum(b[lo:, :] - p_start[j + 1], 0.0))
        q_parts.append(jnp.concatenate([jnp.zeros((lo, DK), F32), qt], axis=0))
        in_blk = jnp.logical_and(row >= SUB * j, row < lo)
        k_parts.append(jnp.where(in_blk, k_hat, 0.0))
    scores = _dot_nt(jnp.concatenate(q_parts, axis=1).astype(BF16),
                     jnp.concatenate(k_parts, axis=1).astype(BF16))

    ones = jnp.ones((DK, DK), BF16)
    t_idx = lax.broadcasted_iota(jnp.int32, (SUB, DK), 0)
    diag_rows = []
    for j in range(NSUB):
        cj = c[SUB * j:SUB * (j + 1), :]
        qj = q[SUB * j:SUB * (j + 1), :]
        kj = kk[SUB * j:SUB * (j + 1), :]
        pieces = []
        for s in range(SUB):
            r0 = 0 if s < 8 else 8
            diff = cj[r0:, :] - cj[s:s + 1, :]
            diff = jnp.where(t_idx[r0:, :] >= s, diff, -jnp.inf)
            pieces.append(jnp.exp(diff) * qj[r0:, :] * kj[s:s + 1, :])
        red = _dot(jnp.concatenate(pieces, axis=0).astype(BF16), ones)
        blk = jnp.zeros((SUB, CHUNK), F32)
        off = 0
        for s in range(SUB):
            r0 = 0 if s < 8 else 8
            n = SUB - r0
            piece = red[off:off + n, :CHUNK]
            off += n
            if r0:
                piece = jnp.concatenate([jnp.zeros((r0, CHUNK), F32), piece], axis=0)
            blk = blk + jnp.where(lane64 == SUB * j + s, piece, 0.0)
        diag_rows.append(blk)
    scores = scores + jnp.concatenate(diag_rows, axis=0)

    o = o + _dot(scores.astype(BF16), v_bf)
    o = o * lax.rsqrt(jnp.mean(o * o, axis=-1, keepdims=True) + EPS) * hg
    return o * (gp * _sigmoid(gp))


def _mix_kernel(x_ref, g_ref, w_in_ref, lbl_ref, hg_ref, w_pool_ref, ps_ref, w_out_ref,
                out_ref, z_ref, st_ref, pext_ref, mixed_ref):
    ts = x_ref.shape[1]
    si = pl.program_id(1)

    @pl.when(si == 0)
    def _():
        st_ref[...] = jnp.zeros_like(st_ref)
        pext_ref[0:POOL_HIST, :] = jnp.zeros((POOL_HIST, KW), F32)

    x = x_ref[0]
    z_ref[...] = _dot(_rms(x, g_ref[...]).astype(BF16), w_in_ref[...])

    l0 = lbl_ref[0:1, :]
    l1 = lbl_ref[1:2, :]
    lmax = jnp.maximum(l0, l1)
    e0 = jnp.exp(l0 - lmax)
    lb_all = e0 / (e0 + jnp.exp(l1 - lmax))

    def chunk_body(ci, carry):
        r0 = pl.multiple_of(ci * CHUNK, CHUNK)
        for h in range(HEADS):
            cols = slice(h * DK, (h + 1) * DK)
            o = _hgrn_chunk_head(
                z_ref[pl.ds(r0, CHUNK), h * DK:(h + 1) * DK],
                z_ref[pl.ds(r0, CHUNK), KW + h * DK:KW + (h + 1) * DK],
                z_ref[pl.ds(r0, CHUNK), 2 * KW + h * DK:2 * KW + (h + 1) * DK],
                z_ref[pl.ds(r0, CHUNK), 3 * KW + h * DK:3 * KW + (h + 1) * DK],
                lb_all[:, cols], hg_ref[:, cols], st_ref, h)
            mixed_ref[pl.ds(r0, CHUNK), cols] = o
        return carry

    lax.fori_loop(0, ts // CHUNK, chunk_body, 0)

    p = z_ref[:, 4 * KW:5 * KW]
    pext_ref[POOL_HIST:POOL_HIST + ts, :] = p
    pos = (si * ts + 1 + lax.broadcasted_iota(jnp.int32, (ts, 1), 0)).astype(F32)
    for gi, w in enumerate(POOL_WINDOWS):
        cols = slice(gi * DK, (gi + 1) * DK)
        acc = pext_ref[:, cols]
        sh = 1
        while sh < w:
            acc = acc + pltpu.roll(acc, sh, 0)
            sh *= 2
        win = acc[POOL_HIST:, :]
        pooled = win / jnp.minimum(pos, float(w)) - p[:, cols]
        y = _dot(pooled.astype(BF16), w_pool_ref[gi]) * ps_ref[:, cols]
        mixed_ref[:, KW + gi * DK:KW + (gi + 1) * DK] = y
    pext_ref[0:POOL_HIST, :] = pext_ref[ts:ts + POOL_HIST, :]

    out_ref[0] = x + _dot(mixed_ref[...].astype(BF16), w_out_ref[...])


def _const_spec(shape):
    nd = len(shape)
    return pl.BlockSpec(shape, lambda *_: (0,) * nd, pipeline_mode=pl.Buffered(1))


def _mix_call(x, g, w_in, lbl, hg, w_pool, ps, w_out):
    bsz, seq, d = x.shape
    ts = TS_MIX
    return pl.pallas_call(
        _mix_kernel,
        grid=(bsz, seq // ts),
        in_specs=[
            pl.BlockSpec((1, ts, d), lambda b, s: (b, s, 0)),
            _const_spec((1, d)),
            _const_spec((d, IN_WIDTH)),
            _const_spec((2, KW)),
            _const_spec((1, KW)),
            _const_spec((len(POOL_WINDOWS), DK, DK)),
            _const_spec((1, KW)),
            _const_spec((d, d)),
        ],
        out_specs=pl.BlockSpec((1, ts, d), lambda b, s: (b, s, 0)),
        out_shape=jax.ShapeDtypeStruct(x.shape, F32),
        scratch_shapes=[
            pltpu.VMEM((ts, IN_WIDTH), F32),
            pltpu.VMEM((HEADS, DK, DK), F32),
            pltpu.VMEM((POOL_HIST + ts, KW), F32),
            pltpu.VMEM((ts, d), F32),
        ],
        compiler_params=pltpu.CompilerParams(
            dimension_semantics=("arbitrary", "arbitrary"), vmem_limit_bytes=VMEM_LIMIT),
        name="mix",
    )(x, g, w_in, lbl, hg, w_pool, ps, w_out)


def _memkv_kernel(mem_ref, g_ref, wk_ref, wv_ref, kt_ref, v_ref):
    hm = _rms(mem_ref[0], g_ref[...]).astype(BF16)
    k = _dot(hm, wk_ref[...]) * (XDIM ** -0.5)
    kt_ref[0] = k.T.astype(BF16)
    v_ref[0] = _dot(hm, wv_ref[...]).astype(BF16)


def _memkv_call(mem, g, wk, wv):
    bsz, m, d = mem.shape
    return pl.pallas_call(
        _memkv_kernel,
        grid=(bsz,),
        in_specs=[
            pl.BlockSpec((1, m, d), lambda b: (b, 0, 0)),
            _const_spec((1, d)),
            _const_spec((d, d)),
            _const_spec((d, d)),
        ],
        out_specs=[
            pl.BlockSpec((1, d, m), lambda b: (b, 0, 0)),
            pl.BlockSpec((1, m, d), lambda b: (b, 0, 0)),
        ],
        out_shape=[
            jax.ShapeDtypeStruct((bsz, d, m), BF16),
            jax.ShapeDtypeStruct((bsz, m, d), BF16),
        ],
        compiler_params=pltpu.CompilerParams(
            dimension_semantics=("arbitrary",), vmem_limit_bytes=VMEM_LIMIT),
        name="memkv",
    )(mem, g, wk, wv)


def _xattn_kernel(x_ref, g_ref, wq_ref, kt_ref, v_ref, wo_ref, out_ref, att_ref):
    x = x_ref[0]
    xq = _dot(_rms(x, g_ref[...]).astype(BF16), wq_ref[...]).astype(BF16)
    for h in range(XHEADS):
        cols = slice(h * XDIM, (h + 1) * XDIM)
        s = _dot(xq[:, cols], kt_ref[0, cols, :])
        e = jnp.exp(s - jnp.max(s, axis=-1, keepdims=True))
        p = e * (1.0 / jnp.sum(e, axis=-1, keepdims=True))
        att_ref[:, cols] = _dot(p.astype(BF16), v_ref[0, :, cols]).astype(BF16)
    out_ref[0] = x + _dot(att_ref[...], wo_ref[...])


def _xattn_call(x, g, wq, kt, v, wo):
    bsz, seq, d = x.shape
    tm = TM_ATT
    return pl.pallas_call(
        _xattn_kernel,
        grid=(bsz, seq // tm),
        in_specs=[
            pl.BlockSpec((1, tm, d), lambda b, s: (b, s, 0)),
            _const_spec((1, d)),
            _const_spec((d, d)),
            pl.BlockSpec((1, d, MEM_LEN), lambda b, s: (b, 0, 0)),
            pl.BlockSpec((1, MEM_LEN, d), lambda b, s: (b, 0, 0)),
            _const_spec((d, d)),
        ],
        out_specs=pl.BlockSpec((1, tm, d), lambda b, s: (b, s, 0)),
        out_shape=jax.ShapeDtypeStruct(x.shape, F32),
        scratch_shapes=[pltpu.VMEM((tm, d), BF16)],
        compiler_params=pltpu.CompilerParams(
            dimension_semantics=("arbitrary", "arbitrary"), vmem_limit_bytes=VMEM_LIMIT),
        name="xattn",
    )(x, g, wq, kt, v, wo)


def _ffn_kernel(x_ref, g_ref, w1_ref, w2_ref, gf_ref, out_ref):
    x = x_ref[...]
    hf = _rms(x, g_ref[...]).astype(BF16)
    acc = x
    for j in range(D_FF // FF_BLOCK):
        cols = slice(j * FF_BLOCK, (j + 1) * FF_BLOCK)
        u = jnp.maximum(_dot(hf, w1_ref[:, cols]), 0.0)
        acc = acc + _dot((u * u).astype(BF16), w2_ref[cols, :])
    out_ref[...] = _rms(acc, gf_ref[...])


def _ffn_call(x2d, g, w1, w2, gf):
    t, d = x2d.shape
    tm = TM_FFN
    return pl.pallas_call(
        _ffn_kernel,
        grid=(t // tm,),
        in_specs=[
            pl.BlockSpec((tm, d), lambda i: (i, 0)),
            _const_spec((1, d)),
            _const_spec((d, D_FF)),
            _const_spec((D_FF, d)),
            _const_spec((1, d)),
        ],
        out_specs=pl.BlockSpec((tm, d), lambda i: (i, 0)),
        out_shape=jax.ShapeDtypeStruct(x2d.shape, F32),
        compiler_params=pltpu.CompilerParams(
            dimension_semantics=("arbitrary",), vmem_limit_bytes=VMEM_LIMIT),
        name="ffn",
    )(x2d, g, w1, w2, gf)


def kernel(x, mem, norm_mix_g, w_in, lb_logits, hgrn_norm_g, w_pool, pool_scale, w_out,
           norm_x_g, norm_mem_g, w_xq, w_xk, w_xv, w_xo, norm_ffn_g, w_ff1, w_ff2, final_norm_g):
    bsz, seq, d = x.shape
    depth = norm_mix_g.shape[0]
    assert depth == 1 and lb_logits.shape[0] == 2
    bf = lambda a: a.astype(BF16)
    row = lambda a: a.reshape(1, -1).astype(F32)
    for l in range(depth):
        x = _mix_call(x, row(norm_mix_g[l]), bf(w_in[l]), lb_logits.astype(F32),
                      row(hgrn_norm_g[l]), bf(w_pool[l]), row(pool_scale[l]), bf(w_out[l]))
        kt, v = _memkv_call(mem, row(norm_mem_g[l]), bf(w_xk[l].reshape(d, d)),
                            bf(w_xv[l].reshape(d, d)))
        x = _xattn_call(x, row(norm_x_g[l]), bf(w_xq[l].reshape(d, d)), kt, v,
                        bf(w_xo[l].reshape(d, d)))
        x = _ffn_call(x.reshape(bsz * seq, d), row(norm_ffn_g[l]), bf(w_ff1[l]), bf(w_ff2[l]),
                      row(final_norm_g)).reshape(bsz, seq, d)
    return x
```

```python
import jax
import jax.numpy as jnp
import numpy as np
from jax import lax
from jax.experimental import pallas as pl
from jax.experimental.pallas import tpu as pltpu

D_MODEL = 1024
HEADS = 4
DK = 128
KW = HEADS * DK
CHUNK = 64
SUB = 16
NSUB = CHUNK // SUB
POOL_WINDOWS = (2, 4, 8, 16)
POOL_HIST = 16
IN_WIDTH = 5 * KW
MEM_LEN = 256
XHEADS = 4
XDIM = 256
D_FF = 4096
EPS = 1e-6
LOG2E = 1.4426950408889634

TS_MIX = 256
TM_ATT = 512
TM_FFN = 512
FF_BLOCK = 1024
VMEM_LIMIT = 56 * 1024 * 1024

F32 = jnp.float32
BF16 = jnp.bfloat16


def _rms(x, g):
    return x * lax.rsqrt(jnp.mean(x * x, axis=-1, keepdims=True) + EPS) * g


def _dot(a, b):
    return jnp.dot(a, b, preferred_element_type=F32)


def _dot_nt(a, b):
    return lax.dot_general(a, b, (((1,), (1,)), ((), ())), preferred_element_type=F32)


def _dot_tn(a, b):
    return lax.dot_general(a, b, (((0,), (0,)), ((), ())), preferred_element_type=F32)


def _sigmoid(x):
    return 1.0 / (1.0 + jnp.exp(-x))


def _decay_matrix():
    t = np.arange(CHUNK)[:, None]
    s = np.arange(CHUNK)[None, :]
    same = (t // SUB) == (s // SUB)
    return np.concatenate([(s <= t) & same, s <= t, same], axis=0).astype(np.float32)


def _hgrn_tile(z_ref, ts, cm, lb_all, hg_all, st_ref, mixed_ref):
    n_chunks = ts // CHUNK
    qp = z_ref[:, 0:KW]
    fp = z_ref[:, KW:2 * KW]
    gp = z_ref[:, 3 * KW:4 * KW]

    f = lb_all + (1.0 - lb_all) * _sigmoid(fp)
    kk_all = 1.0 - f
    q_all = qp * _sigmoid(qp)
    gate_all = gp * _sigmoid(gp)

    lf2 = jnp.log(f) * LOG2E
    hi = lf2.astype(BF16)
    lo = (lf2 - hi.astype(F32)).astype(BF16)
    sums = []
    for ci in range(n_chunks):
        r = slice(ci * CHUNK, (ci + 1) * CHUNK)
        sums.append(_dot(cm, hi[r]) + _dot(cm, lo[r]))

    row = lax.broadcasted_iota(jnp.int32, (CHUNK, DK), 0)
    lane8 = lax.broadcasted_iota(jnp.int32, (8, CHUNK), 1)
    col_masks = [lane8 == n for n in range(CHUNK)]
    causal = (lax.broadcasted_iota(jnp.int32, (CHUNK, CHUNK), 0)
              >= lax.broadcasted_iota(jnp.int32, (CHUNK, CHUNK), 1))
    blk_masks = [jnp.logical_and(row >= SUB * j, row < SUB * (j + 1)) for j in range(NSUB - 1)]

    work = []
    for ci in range(n_chunks):
        r = slice(ci * CHUNK, (ci + 1) * CHUNK)
        c_all = sums[ci][0:CHUNK]
        b_all = sums[ci][CHUNK:2 * CHUNK]
        d_all = sums[ci][2 * CHUNK:3 * CHUNK]
        for h in range(HEADS):
            cols = slice(h * DK, (h + 1) * DK)
            q, kk = q_all[r, cols], kk_all[r, cols]
            c, b, d = c_all[:, cols], b_all[:, cols], d_all[:, cols]
            b_last = b[CHUNK - 1:CHUNK, :]
            v_bf = z_ref[r, 2 * KW + h * DK:2 * KW + (h + 1) * DK].astype(BF16)

            qe_bf = (q * jnp.exp2(b)).astype(BF16)
            k_dec = kk * jnp.exp2(b_last - b)
            upd = _dot_tn(v_bf, k_dec.astype(BF16))

            k_hat = kk * jnp.exp2(d - c)
            q_parts, k_parts = [], []
            for j in range(NSUB - 1):
                lo_r = SUB * (j + 1)
                qt = q[lo_r:, :] * jnp.exp2(b[lo_r:, :] - b[lo_r - 1:lo_r, :])
                q_parts.append(jnp.concatenate([jnp.zeros((lo_r, DK), F32), qt], axis=0))
                k_parts.append(jnp.where(blk_masks[j], k_hat, 0.0))
            s_off = _dot_nt(jnp.concatenate(q_parts, axis=1).astype(BF16),
                            jnp.concatenate(k_parts, axis=1).astype(BF16))

            pieces = []
            for j in range(NSUB):
                cj = c[SUB * j:SUB * (j + 1), :]
                qj = q[SUB * j:SUB * (j + 1), :]
                for s in range(SUB):
                    t0 = 0 if s < 8 else 8
                    pieces.append(qj[t0:, :] * jnp.exp2(cj[t0:, :] - cj[s:s + 1, :]))
            red = _dot_nt(jnp.concatenate(pieces, axis=0).astype(BF16), kk.astype(BF16))
            work.append((ci, h, qe_bf, jnp.exp2(b_last), upd, s_off, red, v_bf))

    state = [st_ref[h] for h in range(HEADS)]
    o_inter = []
    for ci, h, qe_bf, decay, upd, s_off, red, v_bf in work:
        o_inter.append(_dot_nt(qe_bf, state[h].astype(BF16)))
        state[h] = state[h] * decay + upd
    for h in range(HEADS):
        st_ref[h] = state[h]

    o_intra = []
    for ci, h, qe_bf, decay, upd, s_off, red, v_bf in work:
        diag_rows = []
        off = 0
        for j in range(NSUB):
            top = jnp.zeros((8, CHUNK), F32)
            bot = jnp.zeros((8, CHUNK), F32)
            for s in range(SUB):
                m = col_masks[SUB * j + s]
                if s < 8:
                    top = jnp.where(m, red[off:off + 8, :], top)
                    off += 8
                bot = jnp.where(m, red[off:off + 8, :], bot)
                off += 8
            diag_rows += [top, bot]
        scores = jnp.where(causal, jnp.concatenate(diag_rows, axis=0), 0.0) + s_off
        o_intra.append(_dot(scores.astype(BF16), v_bf))

    for (ci, h, *_), oa, ob in zip(work, o_inter, o_intra):
        cols = slice(h * DK, (h + 1) * DK)
        r = slice(ci * CHUNK, (ci + 1) * CHUNK)
        o = oa + ob
        o = o * lax.rsqrt(jnp.mean(o * o, axis=-1, keepdims=True) + EPS) * hg_all[:, cols]
        mixed_ref[r, cols] = o * gate_all[r, cols]


def _mix_kernel(x_ref, g_ref, w_in_ref, lbl_ref, hg_ref, w_pool_ref, ps_ref, w_out_ref, cm_ref,
                out_ref, z_ref, st_ref, pext_ref, mixed_ref):
    ts = x_ref.shape[1]
    si = pl.program_id(1)

    @pl.when(si == 0)
    def _():
        st_ref[...] = jnp.zeros_like(st_ref)
        pext_ref[0:POOL_HIST, :] = jnp.zeros((POOL_HIST, KW), F32)

    x = x_ref[0]
    z_ref[...] = _dot(_rms(x, g_ref[...]).astype(BF16), w_in_ref[...])

    l0 = lbl_ref[0:1, :]
    l1 = lbl_ref[1:2, :]
    lmax = jnp.maximum(l0, l1)
    e0 = jnp.exp(l0 - lmax)
    lb_all = e0 / (e0 + jnp.exp(l1 - lmax))

    _hgrn_tile(z_ref, ts, cm_ref[...], lb_all, hg_ref[...], st_ref, mixed_ref)

    p = z_ref[:, 4 * KW:5 * KW]
    pext_ref[POOL_HIST:POOL_HIST + ts, :] = p
    pos = (si * ts + 1 + lax.broadcasted_iota(jnp.int32, (ts, 1), 0)).astype(F32)
    for gi, w in enumerate(POOL_WINDOWS):
        cols = slice(gi * DK, (gi + 1) * DK)
        acc = pext_ref[:, cols]
        sh = 1
        while sh < w:
            acc = acc + pltpu.roll(acc, sh, 0)
            sh *= 2
        win = acc[POOL_HIST:, :]
        pooled = win / jnp.minimum(pos, float(w)) - p[:, cols]
        y = _dot(pooled.astype(BF16), w_pool_ref[gi]) * ps_ref[:, cols]
        mixed_ref[:, KW + gi * DK:KW + (gi + 1) * DK] = y
    pext_ref[0:POOL_HIST, :] = pext_ref[ts:ts + POOL_HIST, :]

    out_ref[0] = x + _dot(mixed_ref[...].astype(BF16), w_out_ref[...])


def _const_spec(shape):
    nd = len(shape)
    return pl.BlockSpec(shape, lambda *_: (0,) * nd, pipeline_mode=pl.Buffered(1))


def _mix_call(x, g, w_in, lbl, hg, w_pool, ps, w_out):
    bsz, seq, d = x.shape
    ts = TS_MIX
    cm = jnp.asarray(_decay_matrix(), BF16)
    return pl.pallas_call(
        _mix_kernel,
        grid=(bsz, seq // ts),
        in_specs=[
            pl.BlockSpec((1, ts, d), lambda b, s: (b, s, 0)),
            _const_spec((1, d)),
            _const_spec((d, IN_WIDTH)),
            _const_spec((2, KW)),
            _const_spec((1, KW)),
            _const_spec((len(POOL_WINDOWS), DK, DK)),
            _const_spec((1, KW)),
            _const_spec((d, d)),
            _const_spec((3 * CHUNK, CHUNK)),
        ],
        out_specs=pl.BlockSpec((1, ts, d), lambda b, s: (b, s, 0)),
        out_shape=jax.ShapeDtypeStruct(x.shape, F32),
        scratch_shapes=[
            pltpu.VMEM((ts, IN_WIDTH), F32),
            pltpu.VMEM((HEADS, DK, DK), F32),
            pltpu.VMEM((POOL_HIST + ts, KW), F32),
            pltpu.VMEM((ts, d), F32),
        ],
        compiler_params=pltpu.CompilerParams(
            dimension_semantics=("arbitrary", "arbitrary"), vmem_limit_bytes=VMEM_LIMIT),
        name="mix",
    )(x, g, w_in, lbl, hg, w_pool, ps, w_out, cm)


def _memkv_kernel(mem_ref, g_ref, wk_ref, wv_ref, kt_ref, v_ref):
    hm = _rms(mem_ref[0], g_ref[...]).astype(BF16)
    k = _dot(hm, wk_ref[...]) * (XDIM ** -0.5)
    kt_ref[0] = k.T.astype(BF16)
    v_ref[0] = _dot(hm, wv_ref[...]).astype(BF16)


def _memkv_call(mem, g, wk, wv):
    bsz, m, d = mem.shape
    return pl.pallas_call(
        _memkv_kernel,
        grid=(bsz,),
        in_specs=[
            pl.BlockSpec((1, m, d), lambda b: (b, 0, 0)),
            _const_spec((1, d)),
            _const_spec((d, d)),
            _const_spec((d, d)),
        ],
        out_specs=[
            pl.BlockSpec((1, d, m), lambda b: (b, 0, 0)),
            pl.BlockSpec((1, m, d), lambda b: (b, 0, 0)),
        ],
        out_shape=[
            jax.ShapeDtypeStruct((bsz, d, m), BF16),
            jax.ShapeDtypeStruct((bsz, m, d), BF16),
        ],
        compiler_params=pltpu.CompilerParams(
            dimension_semantics=("arbitrary",), vmem_limit_bytes=VMEM_LIMIT),
        name="memkv",
    )(mem, g, wk, wv)


def _xattn_kernel(x_ref, g_ref, wq_ref, kt_ref, v_ref, wo_ref, out_ref, att_ref):
    x = x_ref[0]
    xq = _dot(_rms(x, g_ref[...]).astype(BF16), wq_ref[...]).astype(BF16)
    for h in range(XHEADS):
        cols = slice(h * XDIM, (h + 1) * XDIM)
        s = _dot(xq[:, cols], kt_ref[0, cols, :])
        e = jnp.exp(s - jnp.max(s, axis=-1, keepdims=True))
        p = e * (1.0 / jnp.sum(e, axis=-1, keepdims=True))
        att_ref[:, cols] = _dot(p.astype(BF16), v_ref[0, :, cols]).astype(BF16)
    out_ref[0] = x + _dot(att_ref[...], wo_ref[...])


def _xattn_call(x, g, wq, kt, v, wo):
    bsz, seq, d = x.shape
    tm = TM_ATT
    return pl.pallas_call(
        _xattn_kernel,
        grid=(bsz, seq // tm),
        in_specs=[
            pl.BlockSpec((1, tm, d), lambda b, s: (b, s, 0)),
            _const_spec((1, d)),
            _const_spec((d, d)),
            pl.BlockSpec((1, d, MEM_LEN), lambda b, s: (b, 0, 0)),
            pl.BlockSpec((1, MEM_LEN, d), lambda b, s: (b, 0, 0)),
            _const_spec((d, d)),
        ],
        out_specs=pl.BlockSpec((1, tm, d), lambda b, s: (b, s, 0)),
        out_shape=jax.ShapeDtypeStruct(x.shape, F32),
        scratch_shapes=[pltpu.VMEM((tm, d), BF16)],
        compiler_params=pltpu.CompilerParams(
            dimension_semantics=("arbitrary", "arbitrary"), vmem_limit_bytes=VMEM_LIMIT),
        name="xattn",
    )(x, g, wq, kt, v, wo)


def _ffn_kernel(x_ref, g_ref, w1_ref, w2_ref, gf_ref, out_ref):
    x = x_ref[...]
    hf = _rms(x, g_ref[...]).astype(BF16)
    acc = x
    for j in range(D_FF // FF_BLOCK):
        cols = slice(j * FF_BLOCK, (j + 1) * FF_BLOCK)
        u = jnp.maximum(_dot(hf, w1_ref[:, cols]), 0.0)
        acc = acc + _dot((u * u).astype(BF16), w2_ref[cols, :])
    out_ref[...] = _rms(acc, gf_ref[...])


def _ffn_call(x2d, g, w1, w2, gf):
    t, d = x2d.shape
    tm = TM_FFN
    return pl.pallas_call(
        _ffn_kernel,
        grid=(t // tm,),
        in_specs=[
            pl.BlockSpec((tm, d), lambda i: (i, 0)),
            _const_spec((1, d)),
            _const_spec((d, D_FF)),
            _const_spec((D_FF, d)),
            _const_spec((1, d)),
        ],
        out_specs=pl.BlockSpec((tm, d), lambda i: (i, 0)),
        out_shape=jax.ShapeDtypeStruct(x2d.shape, F32),
        compiler_params=pltpu.CompilerParams(
            dimension_semantics=("arbitrary",), vmem_limit_bytes=VMEM_LIMIT),
        name="ffn",
    )(x2d, g, w1, w2, gf)


def kernel(x, mem, norm_mix_g, w_in, lb_logits, hgrn_norm_g, w_pool, pool_scale, w_out,
           norm_x_g, norm_mem_g, w_xq, w_xk, w_xv, w_xo, norm_ffn_g, w_ff1, w_ff2, final_norm_g):
    bsz, seq, d = x.shape
    depth = norm_mix_g.shape[0]
    assert depth == 1 and lb_logits.shape[0] == 2
    bf = lambda a: a.astype(BF16)
    row = lambda a: a.reshape(1, -1).astype(F32)
    for l in range(depth):
        x = _mix_call(x, row(norm_mix_g[l]), bf(w_in[l]), lb_logits.astype(F32),
                      row(hgrn_norm_g[l]), bf(w_pool[l]), row(pool_scale[l]), bf(w_out[l]))
        kt, v = _memkv_call(mem, row(norm_mem_g[l]), bf(w_xk[l].reshape(d, d)),
                            bf(w_xv[l].reshape(d, d)))
        x = _xattn_call(x, row(norm_x_g[l]), bf(w_xq[l].reshape(d, d)), kt, v,
                        bf(w_xo[l].reshape(d, d)))
        x = _ffn_call(x.reshape(bsz * seq, d), row(norm_ffn_g[l]), bf(w_ff1[l]), bf(w_ff2[l]),
                      row(final_norm_g)).reshape(bsz, seq, d)
    return x
```

```python
import functools

import jax
import jax.numpy as jnp
import numpy as np
from jax import lax
from jax.experimental import pallas as pl
from jax.experimental.pallas import tpu as pltpu

D_MODEL = 1024
HEADS = 4
DK = 128
KW = HEADS * DK
PW = 2 * DK
CHUNK = 64
SUB = 16
NSUB = CHUNK // SUB
POOL_WINDOWS = (2, 4, 8, 16)
POOL_HIST = 16
IN_WIDTH = 5 * KW
MEM_LEN = 256
XHEADS = 4
XDIM = 256
D_FF = 4096
EPS = 1e-6
LOG2E = 1.4426950408889634

TS_MIX = 256
IN_BLOCK = 256
TM_ATT = 512
TM_FFN = 512
FF_BLOCK = 1024
VMEM_LIMIT = 56 * 1024 * 1024

F32 = jnp.float32
BF16 = jnp.bfloat16


def _rms(x, g):
    return x * lax.rsqrt(jnp.mean(x * x, axis=-1, keepdims=True) + EPS) * g


def _dot(a, b):
    return jnp.dot(a, b, preferred_element_type=F32)


def _dot_nt(a, b):
    return lax.dot_general(a, b, (((1,), (1,)), ((), ())), preferred_element_type=F32)


def _dot_tn(a, b):
    return lax.dot_general(a, b, (((0,), (0,)), ((), ())), preferred_element_type=F32)


def _sigmoid(x):
    return 1.0 / (1.0 + jnp.exp(-x))


def _block_diag(a, b):
    z = jnp.zeros_like(a)
    return jnp.concatenate([jnp.concatenate([a, z], axis=1), jnp.concatenate([z, b], axis=1)], axis=0)


def _cumsum_matrix():
    t = np.arange(CHUNK)[:, None]
    s = np.arange(CHUNK)[None, :]
    return (s <= t).astype(np.float32)


def _hgrn_tile(z_ref, ts, cm, lb_all, hg_all, st_ref, mixed_ref, filler):
    n_chunks = ts // CHUNK
    qp = z_ref[:, 0:KW]
    fp = z_ref[:, KW:2 * KW]
    gp = z_ref[:, 3 * KW:4 * KW]

    f = lb_all + (1.0 - lb_all) * _sigmoid(fp)
    kk_all = 1.0 - f
    q_all = qp * _sigmoid(qp)
    gate_all = gp * _sigmoid(gp)

    lf2 = jnp.log(f) * LOG2E
    hi = lf2.astype(BF16)
    lo = (lf2 - hi.astype(F32)).astype(BF16)
    b_chunks = []
    for ci in range(n_chunks):
        r = slice(ci * CHUNK, (ci + 1) * CHUNK)
        b_chunks.append(_dot(cm, hi[r]) + _dot(cm, lo[r]))
    filler()
    filler()

    row = lax.broadcasted_iota(jnp.int32, (CHUNK, PW), 0)
    lane = jnp.bitwise_and(lax.broadcasted_iota(jnp.int32, (8, DK), 1), CHUNK - 1)
    col_masks = [lane == n for n in range(CHUNK)]
    causal = (lax.broadcasted_iota(jnp.int32, (CHUNK, DK), 0)
              >= jnp.bitwise_and(lax.broadcasted_iota(jnp.int32, (CHUNK, DK), 1), CHUNK - 1))
    blk_masks = [jnp.logical_and(row >= SUB * j, row < SUB * (j + 1)) for j in range(NSUB - 1)]

    work = []
    for ci in range(n_chunks):
        r = slice(ci * CHUNK, (ci + 1) * CHUNK)
        for hp in range(HEADS // 2):
            pc = slice(hp * PW, (hp + 1) * PW)
            q, kk, b = q_all[r, pc], kk_all[r, pc], b_chunks[ci][:, pc]
            b_last = b[CHUNK - 1:CHUNK, :]
            v_bf = z_ref[r, 2 * KW + hp * PW:2 * KW + (hp + 1) * PW].astype(BF16)
            kk_bf = kk.astype(BF16)

            qe_bf = (q * jnp.exp2(b)).astype(BF16)
            k_dec = (kk * jnp.exp2(b_last - b)).astype(BF16)
            upd = [_dot_tn(v_bf[:, i * DK:(i + 1) * DK], k_dec[:, i * DK:(i + 1) * DK]) for i in range(2)]

            k_hat = kk * jnp.exp2(jnp.concatenate(
                [b[SUB * j + SUB - 1:SUB * (j + 1), :] - b[SUB * j:SUB * (j + 1), :] for j in range(NSUB)], axis=0))
            q_parts, k_parts = [], []
            for j in range(NSUB - 1):
                lo_r = SUB * (j + 1)
                qt = q[lo_r:, :] * jnp.exp2(b[lo_r:, :] - b[lo_r - 1:lo_r, :])
                q_parts.append(jnp.concatenate([jnp.zeros((lo_r, PW), F32), qt], axis=0))
                kj = jnp.where(blk_masks[j], k_hat, 0.0).astype(BF16)
                k_parts.append(_block_diag(kj[:, :DK], kj[:, DK:]))
            s_off = _dot_nt(jnp.concatenate(q_parts, axis=1).astype(BF16),
                            jnp.concatenate(k_parts, axis=1))

            pieces = []
            for j in range(NSUB):
                bj = b[SUB * j:SUB * (j + 1), :]
                qj = q[SUB * j:SUB * (j + 1), :]
                for s in range(SUB):
                    t0 = 0 if s < 8 else 8
                    pieces.append(qj[t0:, :] * jnp.exp2(bj[t0:, :] - bj[s:s + 1, :]))
            red = _dot_nt(jnp.concatenate(pieces, axis=0).astype(BF16),
                          _block_diag(kk_bf[:, :DK], kk_bf[:, DK:]))
            work.append((ci, hp, qe_bf, jnp.exp2(b_last), upd, s_off, red, v_bf))
            filler()

    state = [st_ref[h] for h in range(HEADS)]
    o_inter = []
    for ci, hp, qe_bf, decay, upd, s_off, red, v_bf in work:
        h0, h1 = 2 * hp, 2 * hp + 1
        o_inter.append(_dot_nt(qe_bf, _block_diag(state[h0].astype(BF16), state[h1].astype(BF16))))
        state[h0] = state[h0] * decay[:, :DK] + upd[0]
        state[h1] = state[h1] * decay[:, DK:] + upd[1]
    for h in range(HEADS):
        st_ref[h] = state[h]

    o_intra = []
    for ci, hp, qe_bf, decay, upd, s_off, red, v_bf in work:
        diag_rows = []
        off = 0
        for j in range(NSUB):
            top = jnp.zeros((8, DK), F32)
            bot = jnp.zeros((8, DK), F32)
            for s in range(SUB):
                m = col_masks[SUB * j + s]
                if s < 8:
                    top = jnp.where(m, red[off:off + 8, :], top)
                    off += 8
                bot = jnp.where(m, red[off:off + 8, :], bot)
                off += 8
            diag_rows += [top, bot]
        scores = jnp.where(causal, jnp.concatenate(diag_rows, axis=0), 0.0) + s_off
        o_intra.append(_dot(scores.astype(BF16), _block_diag(v_bf[:, :DK], v_bf[:, DK:])))

    for (ci, hp, *_), oa, ob in zip(work, o_inter, o_intra):
        r = slice(ci * CHUNK, (ci + 1) * CHUNK)
        o_pair = ob + oa
        for i in range(2):
            cols = slice((2 * hp + i) * DK, (2 * hp + i + 1) * DK)
            o = o_pair[:, i * DK:(i + 1) * DK]
            o = o * lax.rsqrt(jnp.mean(o * o, axis=-1, keepdims=True) + EPS) * hg_all[:, cols]
            mixed_ref[r, cols] = o * gate_all[r, cols]


def _mix_kernel(xn_ref, xp_ref, g_ref, w_in_ref, lbl_ref, hg_ref, w_pool_ref, ps_ref, w_out_ref, cm_ref,
                out_ref, za_ref, zb_ref, st_ref, pext_ref, mixed_ref, *, tiles_per_seq):
    g = pl.program_id(0)
    si = lax.rem(g + tiles_per_seq - 1, tiles_per_seq)

    @pl.when(g == 0)
    def _():
        zb_ref[...] = jnp.zeros(zb_ref.shape, F32)

    @pl.when(jnp.logical_or(si == 0, g == 0))
    def _():
        st_ref[...] = jnp.zeros_like(st_ref)
        pext_ref[0:POOL_HIST, :] = jnp.zeros((POOL_HIST, KW), F32)

    args = (xn_ref, xp_ref, g_ref, w_in_ref, lbl_ref, hg_ref, w_pool_ref, ps_ref, w_out_ref, cm_ref,
            out_ref, st_ref, pext_ref, mixed_ref, si)

    @pl.when(lax.rem(g, 2) == 0)
    def _():
        _mix_step(*args, zn_ref=za_ref, zc_ref=zb_ref)

    @pl.when(lax.rem(g, 2) == 1)
    def _():
        _mix_step(*args, zn_ref=zb_ref, zc_ref=za_ref)


def _mix_step(xn_ref, xp_ref, g_ref, w_in_ref, lbl_ref, hg_ref, w_pool_ref, ps_ref, w_out_ref, cm_ref,
              out_ref, st_ref, pext_ref, mixed_ref, si, *, zn_ref, zc_ref):
    ts = xn_ref.shape[1]
    hn_bf = _rms(xn_ref[0], g_ref[...]).astype(BF16)
    blocks = list(range(IN_WIDTH // IN_BLOCK))

    def filler():
        if blocks:
            cols = slice(blocks[0] * IN_BLOCK, (blocks[0] + 1) * IN_BLOCK)
            blocks.pop(0)
            zn_ref[:, cols] = _dot(hn_bf, w_in_ref[:, cols])

    l0 = lbl_ref[0:1, :]
    l1 = lbl_ref[1:2, :]
    lmax = jnp.maximum(l0, l1)
    e0 = jnp.exp(l0 - lmax)
    lb_all = e0 / (e0 + jnp.exp(l1 - lmax))

    def pool():
        p = zc_ref[:, 4 * KW:5 * KW]
        pext_ref[POOL_HIST:POOL_HIST + ts, :] = p
        pos = (si * ts + 1 + lax.broadcasted_iota(jnp.int32, (ts, 1), 0)).astype(F32)
        for gi, w in enumerate(POOL_WINDOWS):
            cols = slice(gi * DK, (gi + 1) * DK)
            acc = pext_ref[:, cols]
            sh = 1
            while sh < w:
                acc = acc + pltpu.roll(acc, sh, 0)
                sh *= 2
            win = acc[POOL_HIST:, :]
            pooled = win / jnp.minimum(pos, float(w)) - p[:, cols]
            y = _dot(pooled.astype(BF16), w_pool_ref[gi]) * ps_ref[:, cols]
            mixed_ref[:, KW + gi * DK:KW + (gi + 1) * DK] = y
        pext_ref[0:POOL_HIST, :] = pext_ref[ts:ts + POOL_HIST, :]

    _hgrn_tile(zc_ref, ts, cm_ref[...], lb_all, hg_ref[...], st_ref, mixed_ref, filler)
    assert not blocks
    pool()
    out_ref[0] = xp_ref[0] + _dot(mixed_ref[...].astype(BF16), w_out_ref[...])


def _const_spec(shape):
    nd = len(shape)
    return pl.BlockSpec(shape, lambda *_: (0,) * nd, pipeline_mode=pl.Buffered(1))


def _mix_call(x, g, w_in, lbl, hg, w_pool, ps, w_out):
    bsz, seq, d = x.shape
    ts = TS_MIX
    ns = seq // ts
    n_tiles = bsz * ns
    cm = jnp.asarray(_cumsum_matrix(), BF16)

    def tile_next(i):
        t = jnp.minimum(i, n_tiles - 1)
        return (t // ns, t % ns, 0)

    def tile_prev(i):
        t = jnp.maximum(i - 1, 0)
        return (t // ns, t % ns, 0)

    return pl.pallas_call(
        functools.partial(_mix_kernel, tiles_per_seq=ns),
        grid=(n_tiles + 1,),
        in_specs=[
            pl.BlockSpec((1, ts, d), tile_next),
            pl.BlockSpec((1, ts, d), tile_prev),
            _const_spec((1, d)),
            _const_spec((d, IN_WIDTH)),
            _const_spec((2, KW)),
            _const_spec((1, KW)),
            _const_spec((len(POOL_WINDOWS), DK, DK)),
            _const_spec((1, KW)),
            _const_spec((d, d)),
            _const_spec((CHUNK, CHUNK)),
        ],
        out_specs=pl.BlockSpec((1, ts, d), tile_prev),
        out_shape=jax.ShapeDtypeStruct(x.shape, F32),
        scratch_shapes=[
            pltpu.VMEM((ts, IN_WIDTH), F32),
            pltpu.VMEM((ts, IN_WIDTH), F32),
            pltpu.VMEM((HEADS, DK, DK), F32),
            pltpu.VMEM((POOL_HIST + ts, KW), F32),
            pltpu.VMEM((ts, d), F32),
        ],
        compiler_params=pltpu.CompilerParams(
            dimension_semantics=("arbitrary",), vmem_limit_bytes=VMEM_LIMIT),
        name="mix",
    )(x, x, g, w_in, lbl, hg, w_pool, ps, w_out, cm)


def _memkv_kernel(mem_ref, g_ref, wk_ref, wv_ref, kt_ref, v_ref):
    hm = _rms(mem_ref[0], g_ref[...]).astype(BF16)
    k = _dot(hm, wk_ref[...]) * (XDIM ** -0.5)
    kt_ref[0] = k.T.astype(BF16)
    v_ref[0] = _dot(hm, wv_ref[...]).astype(BF16)


def _memkv_call(mem, g, wk, wv):
    bsz, m, d = mem.shape
    return pl.pallas_call(
        _memkv_kernel,
        grid=(bsz,),
        in_specs=[
            pl.BlockSpec((1, m, d), lambda b: (b, 0, 0)),
            _const_spec((1, d)),
            _const_spec((d, d)),
            _const_spec((d, d)),
        ],
        out_specs=[
            pl.BlockSpec((1, d, m), lambda b: (b, 0, 0)),
            pl.BlockSpec((1, m, d), lambda b: (b, 0, 0)),
        ],
        out_shape=[
            jax.ShapeDtypeStruct((bsz, d, m), BF16),
            jax.ShapeDtypeStruct((bsz, m, d), BF16),
        ],
        compiler_params=pltpu.CompilerParams(
            dimension_semantics=("arbitrary",), vmem_limit_bytes=VMEM_LIMIT),
        name="memkv",
    )(mem, g, wk, wv)


def _xattn_kernel(x_ref, g_ref, wq_ref, kt_ref, v_ref, wo_ref, out_ref, att_ref):
    x = x_ref[0]
    xq = _dot(_rms(x, g_ref[...]).astype(BF16), wq_ref[...]).astype(BF16)
    for h in range(XHEADS):
        cols = slice(h * XDIM, (h + 1) * XDIM)
        s = _dot(xq[:, cols], kt_ref[0, cols, :])
        e = jnp.exp(s - jnp.max(s, axis=-1, keepdims=True))
        p = e * (1.0 / jnp.sum(e, axis=-1, keepdims=True))
        att_ref[:, cols] = _dot(p.astype(BF16), v_ref[0, :, cols]).astype(BF16)
    out_ref[0] = x + _dot(att_ref[...], wo_ref[...])


def _xattn_call(x, g, wq, kt, v, wo):
    bsz, seq, d = x.shape
    tm = TM_ATT
    return pl.pallas_call(
        _xattn_kernel,
        grid=(bsz, seq // tm),
        in_specs=[
            pl.BlockSpec((1, tm, d), lambda b, s: (b, s, 0)),
            _const_spec((1, d)),
            _const_spec((d, d)),
            pl.BlockSpec((1, d, MEM_LEN), lambda b, s: (b, 0, 0)),
            pl.BlockSpec((1, MEM_LEN, d), lambda b, s: (b, 0, 0)),
            _const_spec((d, d)),
        ],
        out_specs=pl.BlockSpec((1, tm, d), lambda b, s: (b, s, 0)),
        out_shape=jax.ShapeDtypeStruct(x.shape, F32),
        scratch_shapes=[pltpu.VMEM((tm, d), BF16)],
        compiler_params=pltpu.CompilerParams(
            dimension_semantics=("arbitrary", "arbitrary"), vmem_limit_bytes=VMEM_LIMIT),
        name="xattn",
    )(x, g, wq, kt, v, wo)


def _ffn_kernel(x_ref, g_ref, w1_ref, w2_ref, gf_ref, out_ref):
    x = x_ref[...]
    hf = _rms(x, g_ref[...]).astype(BF16)
    acc = x
    for j in range(D_FF // FF_BLOCK):
        cols = slice(j * FF_BLOCK, (j + 1) * FF_BLOCK)
        u = jnp.maximum(_dot(hf, w1_ref[:, cols]), 0.0)
        acc = acc + _dot((u * u).astype(BF16), w2_ref[cols, :])
    out_ref[...] = _rms(acc, gf_ref[...])


def _ffn_call(x2d, g, w1, w2, gf):
    t, d = x2d.shape
    tm = TM_FFN
    return pl.pallas_call(
        _ffn_kernel,
        grid=(t // tm,),
        in_specs=[
            pl.BlockSpec((tm, d), lambda i: (i, 0)),
            _const_spec((1, d)),
            _const_spec((d, D_FF)),
            _const_spec((D_FF, d)),
            _const_spec((1, d)),
        ],
        out_specs=pl.BlockSpec((tm, d), lambda i: (i, 0)),
        out_shape=jax.ShapeDtypeStruct(x2d.shape, F32),
        compiler_params=pltpu.CompilerParams(
            dimension_semantics=("arbitrary",), vmem_limit_bytes=VMEM_LIMIT),
        name="ffn",
    )(x2d, g, w1, w2, gf)


def kernel(x, mem, norm_mix_g, w_in, lb_logits, hgrn_norm_g, w_pool, pool_scale, w_out,
           norm_x_g, norm_mem_g, w_xq, w_xk, w_xv, w_xo, norm_ffn_g, w_ff1, w_ff2, final_norm_g):
    bsz, seq, d = x.shape
    depth = norm_mix_g.shape[0]
    assert depth == 1 and lb_logits.shape[0] == 2
    bf = lambda a: a.astype(BF16)
    row = lambda a: a.reshape(1, -1).astype(F32)
    for l in range(depth):
        x = _mix_call(x, row(norm_mix_g[l]), bf(w_in[l]), lb_logits.astype(F32),
                      row(hgrn_norm_g[l]), bf(w_pool[l]), row(pool_scale[l]), bf(w_out[l]))
        kt, v = _memkv_call(mem, row(norm_mem_g[l]), bf(w_xk[l].reshape(d, d)),
                            bf(w_xv[l].reshape(d, d)))
        x = _xattn_call(x, row(norm_x_g[l]), bf(w_xq[l].reshape(d, d)), kt, v,
                        bf(w_xo[l].reshape(d, d)))
        x = _ffn_call(x.reshape(bsz * seq, d), row(norm_ffn_g[l]), bf(w_ff1[l]), bf(w_ff2[l]),
                      row(final_norm_g)).reshape(bsz, seq, d)
    return x
```

```python
import functools

import jax
import jax.numpy as jnp
import numpy as np
from jax import lax
from jax.experimental import pallas as pl
from jax.experimental.pallas import tpu as pltpu

D_MODEL = 1024
HEADS = 4
DK = 128
KW = HEADS * DK
PW = 2 * DK
CHUNK = 64
SUB = 16
NSUB = CHUNK // SUB
POOL_WINDOWS = (2, 4, 8, 16)
POOL_HIST = 16
IN_WIDTH = 5 * KW
MEM_LEN = 256
XHEADS = 4
XDIM = 256
D_FF = 4096
EPS = 1e-6
LOG2E = 1.4426950408889634

TS_MIX = 256
IN_BLOCK = 256
TM_ATT = 512
TM_FFN = 512
FF_BLOCK = 1024
VMEM_LIMIT = 56 * 1024 * 1024

F32 = jnp.float32
BF16 = jnp.bfloat16


def _rms(x, g):
    return x * lax.rsqrt(jnp.mean(x * x, axis=-1, keepdims=True) + EPS) * g


def _dot(a, b):
    return jnp.dot(a, b, preferred_element_type=F32)


def _dot_nt(a, b):
    return lax.dot_general(a, b, (((1,), (1,)), ((), ())), preferred_element_type=F32)


def _dot_tn(a, b):
    return lax.dot_general(a, b, (((0,), (0,)), ((), ())), preferred_element_type=F32)


def _sigmoid(x):
    return 1.0 / (1.0 + jnp.exp(-x))


def _block_diag(a, b):
    z = jnp.zeros_like(a)
    return jnp.concatenate([jnp.concatenate([a, z], axis=1), jnp.concatenate([z, b], axis=1)], axis=0)


def _cumsum_matrix():
    t = np.arange(CHUNK)[:, None]
    s = np.arange(CHUNK)[None, :]
    return (s <= t).astype(np.float32)


def _hgrn_tile(z_ref, ts, cm, lb_all, hg_all, st_ref, mixed_ref, filler):
    n_chunks = ts // CHUNK
    qp = z_ref[:, 0:KW]
    fp = z_ref[:, KW:2 * KW]
    gp = z_ref[:, 3 * KW:4 * KW]

    f = lb_all + (1.0 - lb_all) * _sigmoid(fp)
    kk_all = 1.0 - f
    q_all = qp * _sigmoid(qp)
    gate_all = gp * _sigmoid(gp)

    lf2 = jnp.log(f) * LOG2E
    hi = lf2.astype(BF16)
    lo = (lf2 - hi.astype(F32)).astype(BF16)
    b_chunks = []
    for ci in range(n_chunks):
        r = slice(ci * CHUNK, (ci + 1) * CHUNK)
        b_chunks.append(_dot(cm, hi[r]) + _dot(cm, lo[r]))
    filler()
    filler()

    row = lax.broadcasted_iota(jnp.int32, (CHUNK, PW), 0)
    lane = jnp.bitwise_and(lax.broadcasted_iota(jnp.int32, (8, DK), 1), CHUNK - 1)
    col_masks = [lane == n for n in range(CHUNK)]
    causal = (lax.broadcasted_iota(jnp.int32, (CHUNK, DK), 0)
              >= jnp.bitwise_and(lax.broadcasted_iota(jnp.int32, (CHUNK, DK), 1), CHUNK - 1))
    blk_masks = [jnp.logical_and(row >= SUB * j, row < SUB * (j + 1)) for j in range(NSUB - 1)]

    work = []
    for ci in range(n_chunks):
        r = slice(ci * CHUNK, (ci + 1) * CHUNK)
        for hp in range(HEADS // 2):
            pc = slice(hp * PW, (hp + 1) * PW)
            q, kk, b = q_all[r, pc], kk_all[r, pc], b_chunks[ci][:, pc]
            b_last = b[CHUNK - 1:CHUNK, :]
            v_bf = z_ref[r, 2 * KW + hp * PW:2 * KW + (hp + 1) * PW].astype(BF16)
            kk_bf = kk.astype(BF16)

            qe_bf = (q * jnp.exp2(b)).astype(BF16)
            k_dec = (kk * jnp.exp2(b_last - b)).astype(BF16)
            upd = [_dot_tn(v_bf[:, i * DK:(i + 1) * DK], k_dec[:, i * DK:(i + 1) * DK]) for i in range(2)]

            k_hat = kk * jnp.exp2(jnp.concatenate(
                [b[SUB * j + SUB - 1:SUB * (j + 1), :] - b[SUB * j:SUB * (j + 1), :] for j in range(NSUB)], axis=0))
            q_parts, k_parts = [], []
            for j in range(NSUB - 1):
                lo_r = SUB * (j + 1)
                qt = q[lo_r:, :] * jnp.exp2(b[lo_r:, :] - b[lo_r - 1:lo_r, :])
                q_parts.append(jnp.concatenate([jnp.zeros((lo_r, PW), F32), qt], axis=0))
                kj = jnp.where(blk_masks[j], k_hat, 0.0).astype(BF16)
                k_parts.append(_block_diag(kj[:, :DK], kj[:, DK:]))
            s_off = _dot_nt(jnp.concatenate(q_parts, axis=1).astype(BF16),
                            jnp.concatenate(k_parts, axis=1))

            pieces = []
            for j in range(NSUB):
                bj = b[SUB * j:SUB * (j + 1), :]
                qj = q[SUB * j:SUB * (j + 1), :]
                for s in range(SUB):
                    t0 = 0 if s < 8 else 8
                    pieces.append(qj[t0:, :] * jnp.exp2(bj[t0:, :] - bj[s:s + 1, :]))
            red = _dot_nt(jnp.concatenate(pieces, axis=0).astype(BF16),
                          _block_diag(kk_bf[:, :DK], kk_bf[:, DK:]))
            work.append((ci, hp, qe_bf, jnp.exp2(b_last), upd, s_off, red, v_bf))
            filler()

    state = [st_ref[h] for h in range(HEADS)]
    o_inter = []
    for ci, hp, qe_bf, decay, upd, s_off, red, v_bf in work:
        h0, h1 = 2 * hp, 2 * hp + 1
        o_inter.append(_dot_nt(qe_bf, _block_diag(state[h0].astype(BF16), state[h1].astype(BF16))))
        state[h0] = state[h0] * decay[:, :DK] + upd[0]
        state[h1] = state[h1] * decay[:, DK:] + upd[1]
    for h in range(HEADS):
        st_ref[h] = state[h]

    o_intra = []
    for ci, hp, qe_bf, decay, upd, s_off, red, v_bf in work:
        diag_rows = []
        off = 0
        for j in range(NSUB):
            top = jnp.zeros((8, DK), F32)
            bot = jnp.zeros((8, DK), F32)
            for s in range(SUB):
                m = col_masks[SUB * j + s]
                if s < 8:
                    top = jnp.where(m, red[off:off + 8, :], top)
                    off += 8
                bot = jnp.where(m, red[off:off + 8, :], bot)
                off += 8
            diag_rows += [top, bot]
        scores = jnp.where(causal, jnp.concatenate(diag_rows, axis=0), 0.0) + s_off
        o_intra.append(_dot(scores.astype(BF16), _block_diag(v_bf[:, :DK], v_bf[:, DK:])))

    for (ci, hp, *_), oa, ob in zip(work, o_inter, o_intra):
        r = slice(ci * CHUNK, (ci + 1) * CHUNK)
        o_pair = ob + oa
        for i in range(2):
            cols = slice((2 * hp + i) * DK, (2 * hp + i + 1) * DK)
            o = o_pair[:, i * DK:(i + 1) * DK]
            o = o * lax.rsqrt(jnp.mean(o * o, axis=-1, keepdims=True) + EPS) * hg_all[:, cols]
            mixed_ref[r, cols] = o * gate_all[r, cols]


def _mix_kernel(xn_ref, xp_ref, g_ref, w_in_f32, lbl_ref, hg_ref, w_pool_f32, ps_ref, w_out_f32, cm_ref,
                out_ref, za_ref, zb_ref, st_ref, pext_ref, mixed_ref, w_in_ref, w_pool_ref, w_out_ref,
                *, tiles_per_seq):
    g = pl.program_id(0)
    si = lax.rem(g + tiles_per_seq - 1, tiles_per_seq)

    @pl.when(g == 0)
    def _():
        zb_ref[...] = jnp.zeros(zb_ref.shape, F32)
        w_in_ref[...] = w_in_f32[...].astype(BF16)
        w_pool_ref[...] = w_pool_f32[...].astype(BF16)
        w_out_ref[...] = w_out_f32[...].astype(BF16)

    @pl.when(jnp.logical_or(si == 0, g == 0))
    def _():
        st_ref[...] = jnp.zeros_like(st_ref)
        pext_ref[0:POOL_HIST, :] = jnp.zeros((POOL_HIST, KW), F32)

    args = (xn_ref, xp_ref, g_ref, w_in_ref, lbl_ref, hg_ref, w_pool_ref, ps_ref, w_out_ref, cm_ref,
            out_ref, st_ref, pext_ref, mixed_ref, si)

    @pl.when(lax.rem(g, 2) == 0)
    def _():
        _mix_step(*args, zn_ref=za_ref, zc_ref=zb_ref)

    @pl.when(lax.rem(g, 2) == 1)
    def _():
        _mix_step(*args, zn_ref=zb_ref, zc_ref=za_ref)


def _mix_step(xn_ref, xp_ref, g_ref, w_in_ref, lbl_ref, hg_ref, w_pool_ref, ps_ref, w_out_ref, cm_ref,
              out_ref, st_ref, pext_ref, mixed_ref, si, *, zn_ref, zc_ref):
    ts = xn_ref.shape[1]
    hn_bf = _rms(xn_ref[0], g_ref[...]).astype(BF16)
    blocks = list(range(IN_WIDTH // IN_BLOCK))

    def filler():
        if blocks:
            cols = slice(blocks[0] * IN_BLOCK, (blocks[0] + 1) * IN_BLOCK)
            blocks.pop(0)
            zn_ref[:, cols] = _dot(hn_bf, w_in_ref[:, cols])

    l0 = lbl_ref[0:1, :]
    l1 = lbl_ref[1:2, :]
    lmax = jnp.maximum(l0, l1)
    e0 = jnp.exp(l0 - lmax)
    lb_all = e0 / (e0 + jnp.exp(l1 - lmax))

    def pool():
        p = zc_ref[:, 4 * KW:5 * KW]
        pext_ref[POOL_HIST:POOL_HIST + ts, :] = p
        pos = (si * ts + 1 + lax.broadcasted_iota(jnp.int32, (ts, 1), 0)).astype(F32)
        for gi, w in enumerate(POOL_WINDOWS):
            cols = slice(gi * DK, (gi + 1) * DK)
            acc = pext_ref[:, cols]
            sh = 1
            while sh < w:
                acc = acc + pltpu.roll(acc, sh, 0)
                sh *= 2
            win = acc[POOL_HIST:, :]
            pooled = win / jnp.minimum(pos, float(w)) - p[:, cols]
            y = _dot(pooled.astype(BF16), w_pool_ref[gi]) * ps_ref[:, cols]
            mixed_ref[:, KW + gi * DK:KW + (gi + 1) * DK] = y
        pext_ref[0:POOL_HIST, :] = pext_ref[ts:ts + POOL_HIST, :]

    _hgrn_tile(zc_ref, ts, cm_ref[...], lb_all, hg_ref[...], st_ref, mixed_ref, filler)
    assert not blocks
    pool()
    out_ref[0] = xp_ref[0] + _dot(mixed_ref[...].astype(BF16), w_out_ref[...])


def _const_spec(shape):
    nd = len(shape)
    return pl.BlockSpec(shape, lambda *_: (0,) * nd, pipeline_mode=pl.Buffered(1))


def _mix_call(x, g, w_in, lbl, hg, w_pool, ps, w_out):
    bsz, seq, d = x.shape
    ts = TS_MIX
    ns = seq // ts
    n_tiles = bsz * ns
    cm = jnp.asarray(_cumsum_matrix(), BF16)

    def tile_next(i):
        t = jnp.minimum(i, n_tiles - 1)
        return (t // ns, t % ns, 0)

    def tile_prev(i):
        t = jnp.maximum(i - 1, 0)
        return (t // ns, t % ns, 0)

    return pl.pallas_call(
        functools.partial(_mix_kernel, tiles_per_seq=ns),
        grid=(n_tiles + 1,),
        in_specs=[
            pl.BlockSpec((1, ts, d), tile_next),
            pl.BlockSpec((1, ts, d), tile_prev),
            _const_spec((1, d)),
            _const_spec((d, IN_WIDTH)),
            _const_spec((2, KW)),
            _const_spec((1, KW)),
            _const_spec((len(POOL_WINDOWS), DK, DK)),
            _const_spec((1, KW)),
            _const_spec((d, d)),
            _const_spec((CHUNK, CHUNK)),
        ],
        out_specs=pl.BlockSpec((1, ts, d), tile_prev),
        out_shape=jax.ShapeDtypeStruct(x.shape, F32),
        scratch_shapes=[
            pltpu.VMEM((ts, IN_WIDTH), F32),
            pltpu.VMEM((ts, IN_WIDTH), F32),
            pltpu.VMEM((HEADS, DK, DK), F32),
            pltpu.VMEM((POOL_HIST + ts, KW), F32),
            pltpu.VMEM((ts, d), F32),
            pltpu.VMEM((d, IN_WIDTH), BF16),
            pltpu.VMEM((len(POOL_WINDOWS), DK, DK), BF16),
            pltpu.VMEM((d, d), BF16),
        ],
        compiler_params=pltpu.CompilerParams(
            dimension_semantics=("arbitrary",), vmem_limit_bytes=VMEM_LIMIT),
        name="mix",
    )(x, x, g, w_in, lbl, hg, w_pool, ps, w_out, cm)


def _memkv_kernel(mem_ref, g_ref, wk_ref, wv_ref, kt_ref, v_ref):
    hm = _rms(mem_ref[0], g_ref[...]).astype(BF16)
    k = _dot(hm, wk_ref[...].astype(BF16)) * (XDIM ** -0.5)
    kt_ref[0] = k.T.astype(BF16)
    v_ref[0] = _dot(hm, wv_ref[...].astype(BF16)).astype(BF16)


def _memkv_call(mem, g, wk, wv):
    bsz, m, d = mem.shape
    return pl.pallas_call(
        _memkv_kernel,
        grid=(bsz,),
        in_specs=[
            pl.BlockSpec((1, m, d), lambda b: (b, 0, 0)),
            _const_spec((1, d)),
            _const_spec((d, d)),
            _const_spec((d, d)),
        ],
        out_specs=[
            pl.BlockSpec((1, d, m), lambda b: (b, 0, 0)),
            pl.BlockSpec((1, m, d), lambda b: (b, 0, 0)),
        ],
        out_shape=[
            jax.ShapeDtypeStruct((bsz, d, m), BF16),
            jax.ShapeDtypeStruct((bsz, m, d), BF16),
        ],
        compiler_params=pltpu.CompilerParams(
            dimension_semantics=("arbitrary",), vmem_limit_bytes=VMEM_LIMIT),
        name="memkv",
    )(mem, g, wk, wv)


def _xattn_kernel(x_ref, g_ref, wq_ref, kt_ref, v_ref, wo_ref, out_ref, wq_bf, wo_bf):
    @pl.when(jnp.logical_and(pl.program_id(0) == 0, pl.program_id(1) == 0))
    def _():
        wq_bf[...] = wq_ref[...].astype(BF16)
        wo_bf[...] = wo_ref[...].astype(BF16)

    hm = x_ref.shape[1] // 2
    head_cols = [slice(h * XDIM, (h + 1) * XDIM) for h in range(XHEADS)]
    rows = [slice(i * hm, (i + 1) * hm) for i in range(2)]
    xs = [x_ref[0, r, :] for r in rows]

    def q_proj(x):
        return _dot(_rms(x, g_ref[...]).astype(BF16), wq_bf[...]).astype(BF16)

    def scores(xq):
        return [_dot(xq[:, c], kt_ref[0, c, :]) for c in head_cols]

    def attend(ss):
        outs = []
        for s, c in zip(ss, head_cols):
            e = jnp.exp(s - jnp.max(s, axis=-1, keepdims=True))
            p = e * (1.0 / jnp.sum(e, axis=-1, keepdims=True))
            outs.append(_dot(p.astype(BF16), v_ref[0, :, c]).astype(BF16))
        return jnp.concatenate(outs, axis=1)

    def o_proj(i, att):
        out_ref[0, rows[i], :] = xs[i] + _dot(att, wo_bf[...])

    s_a = scores(q_proj(xs[0]))
    xq_b = q_proj(xs[1])
    att_a = attend(s_a)
    s_b = scores(xq_b)
    o_proj(0, att_a)
    o_proj(1, attend(s_b))


def _xattn_call(x, g, wq, kt, v, wo):
    bsz, seq, d = x.shape
    tm = TM_ATT
    return pl.pallas_call(
        _xattn_kernel,
        grid=(bsz, seq // tm),
        in_specs=[
            pl.BlockSpec((1, tm, d), lambda b, s: (b, s, 0)),
            _const_spec((1, d)),
            _const_spec((d, d)),
            pl.BlockSpec((1, d, MEM_LEN), lambda b, s: (b, 0, 0)),
            pl.BlockSpec((1, MEM_LEN, d), lambda b, s: (b, 0, 0)),
            _const_spec((d, d)),
        ],
        out_specs=pl.BlockSpec((1, tm, d), lambda b, s: (b, s, 0)),
        out_shape=jax.ShapeDtypeStruct(x.shape, F32),
        scratch_shapes=[pltpu.VMEM((d, d), BF16), pltpu.VMEM((d, d), BF16)],
        compiler_params=pltpu.CompilerParams(
            dimension_semantics=("arbitrary", "arbitrary"), vmem_limit_bytes=VMEM_LIMIT),
        name="xattn",
    )(x, g, wq, kt, v, wo)


def _ffn_kernel(x_ref, g_ref, w1_ref, w2_ref, gf_ref, out_ref):
    x = x_ref[...]
    hf = _rms(x, g_ref[...]).astype(BF16)
    acc = x
    for j in range(D_FF // FF_BLOCK):
        cols = slice(j * FF_BLOCK, (j + 1) * FF_BLOCK)
        u = jnp.maximum(_dot(hf, w1_ref[:, cols].astype(BF16)), 0.0)
        acc = acc + _dot((u * u).astype(BF16), w2_ref[cols, :].astype(BF16))
    out_ref[...] = _rms(acc, gf_ref[...])


def _ffn_call(x2d, g, w1, w2, gf):
    t, d = x2d.shape
    tm = TM_FFN
    return pl.pallas_call(
        _ffn_kernel,
        grid=(t // tm,),
        in_specs=[
            pl.BlockSpec((tm, d), lambda i: (i, 0)),
            _const_spec((1, d)),
            _const_spec((d, D_FF)),
            _const_spec((D_FF, d)),
            _const_spec((1, d)),
        ],
        out_specs=pl.BlockSpec((tm, d), lambda i: (i, 0)),
        out_shape=jax.ShapeDtypeStruct(x2d.shape, F32),
        compiler_params=pltpu.CompilerParams(
            dimension_semantics=("arbitrary",), vmem_limit_bytes=VMEM_LIMIT),
        name="ffn",
    )(x2d, g, w1, w2, gf)


def kernel(x, mem, norm_mix_g, w_in, lb_logits, hgrn_norm_g, w_pool, pool_scale, w_out,
           norm_x_g, norm_mem_g, w_xq, w_xk, w_xv, w_xo, norm_ffn_g, w_ff1, w_ff2, final_norm_g):
    bsz, seq, d = x.shape
    depth = norm_mix_g.shape[0]
    assert depth == 1 and lb_logits.shape[0] == 2
    row = lambda a: a.reshape(1, -1).astype(F32)
    x = _mix_call(x, row(norm_mix_g), w_in.reshape(d, IN_WIDTH), lb_logits.astype(F32),
                  row(hgrn_norm_g), w_pool.reshape(len(POOL_WINDOWS), DK, DK), row(pool_scale),
                  w_out.reshape(d, d))
    kt, v = _memkv_call(mem, row(norm_mem_g), w_xk.reshape(d, d), w_xv.reshape(d, d))
    x = _xattn_call(x, row(norm_x_g), w_xq.reshape(d, d), kt, v, w_xo.reshape(d, d))
    x = _ffn_call(x.reshape(bsz * seq, d), row(norm_ffn_g), w_ff1.reshape(d, D_FF),
                  w_ff2.reshape(D_FF, d), row(final_norm_g)).reshape(bsz, seq, d)
    return x
```

```python
import functools

import jax
import jax.numpy as jnp
import numpy as np
from jax import lax
from jax.experimental import pallas as pl
from jax.experimental.pallas import tpu as pltpu

D_MODEL = 1024
HEADS = 4
DK = 128
KW = HEADS * DK
PW = 2 * DK
CHUNK = 64
LEVELS = (32, 16, 8, 4, 2, 1)
POOL_WINDOWS = (2, 4, 8, 16)
POOL_HIST = 16
IN_WIDTH = 5 * KW
MEM_LEN = 256
XHEADS = 4
XDIM = 256
D_FF = 4096
EPS = 1e-6
LOG2E = 1.4426950408889634

TS_MIX = 256
IN_BLOCK = 256
TM_TAIL = 512
FF_BLOCK = 1024
VMEM_LIMIT = 56 * 1024 * 1024
TAIL_VMEM_LIMIT = 60 * 1024 * 1024

F32 = jnp.float32
BF16 = jnp.bfloat16


def _rms(x, g):
    return x * lax.rsqrt(jnp.mean(x * x, axis=-1, keepdims=True) + EPS) * g


def _dot(a, b):
    return jnp.dot(a, b, preferred_element_type=F32)


def _dot_nt(a, b):
    return lax.dot_general(a, b, (((1,), (1,)), ((), ())), preferred_element_type=F32)


def _dot_tn(a, b):
    return lax.dot_general(a, b, (((0,), (0,)), ((), ())), preferred_element_type=F32)


def _sigmoid(x):
    return 1.0 / (1.0 + jnp.exp(-x))


def _block_diag(a, b):
    z = jnp.zeros_like(a)
    return jnp.concatenate([jnp.concatenate([a, z], axis=1), jnp.concatenate([z, b], axis=1)], axis=0)


def _cumsum_matrix():
    t = np.arange(CHUNK)[:, None]
    s = np.arange(CHUNK)[None, :]
    return (s <= t).astype(np.float32)


def _hgrn_tile(z_ref, ts, cm, lb_all, hg_all, st_ref, mixed_ref, filler):
    n_chunks = ts // CHUNK
    qp = z_ref[:, 0:KW]
    fp = z_ref[:, KW:2 * KW]
    gp = z_ref[:, 3 * KW:4 * KW]

    f = lb_all + (1.0 - lb_all) * _sigmoid(fp)
    kk_all = 1.0 - f
    q_all = qp * _sigmoid(qp)
    gate_all = gp * _sigmoid(gp)

    lf2 = jnp.log(f) * LOG2E
    hi = lf2.astype(BF16)
    lo = (lf2 - hi.astype(F32)).astype(BF16)
    b_chunks = []
    for ci in range(n_chunks):
        r = slice(ci * CHUNK, (ci + 1) * CHUNK)
        b_chunks.append(_dot(cm, hi[r]) + _dot(cm, lo[r]))
    filler()
    filler()

    row = lax.broadcasted_iota(jnp.int32, (CHUNK, PW), 0)
    upper = {L: jnp.bitwise_and(row, 2 * L - 1) >= L for L in LEVELS}
    out_row = lax.broadcasted_iota(jnp.int32, (CHUNK, DK), 0)
    out_col = jnp.bitwise_and(lax.broadcasted_iota(jnp.int32, (CHUNK, DK), 1), CHUNK - 1)
    same_block = {L: jnp.bitwise_and(jnp.bitwise_xor(out_row, out_col), -2 * L) == 0 for L in LEVELS[1:]}
    diagonal = out_row == out_col

    def ref_rows(a, first, period, width):
        parts = [jnp.broadcast_to(a[r:r + 1, :], (width, a.shape[1])) for r in range(first, CHUNK, period)]
        return parts[0] if len(parts) == 1 else jnp.concatenate(parts, axis=0)

    def level_ref(b, L):
        if L >= 4:
            return ref_rows(b, L - 1, 2 * L, 2 * L)
        return jnp.where(jnp.bitwise_and(row, 7) < 4, ref_rows(b, 1, 8, 8), ref_rows(b, 5, 8, 8))

    def pair_dot(qe, ke):
        ke = ke.astype(BF16)
        return _dot_nt(qe.astype(BF16), _block_diag(ke[:, :DK], ke[:, DK:]))

    work = []
    for ci in range(n_chunks):
        r = slice(ci * CHUNK, (ci + 1) * CHUNK)
        for hp in range(HEADS // 2):
            pc = slice(hp * PW, (hp + 1) * PW)
            q, kk, b = q_all[r, pc], kk_all[r, pc], b_chunks[ci][:, pc]
            b_last = b[CHUNK - 1:CHUNK, :]
            v_bf = z_ref[r, 2 * KW + hp * PW:2 * KW + (hp + 1) * PW].astype(BF16)

            qe_bf = (q * jnp.exp2(b)).astype(BF16)
            k_dec = (kk * jnp.exp2(b_last - b)).astype(BF16)
            upd = [_dot_tn(v_bf[:, i * DK:(i + 1) * DK], k_dec[:, i * DK:(i + 1) * DK]) for i in range(2)]

            level_scores = []
            for L in LEVELS:
                if L == 1:
                    x = jnp.where(upper[L], q * f[r, pc], kk)
                else:
                    x = jnp.where(upper[L], q, kk) * jnp.exp2(-jnp.abs(b - level_ref(b, L)))
                level_scores.append(pair_dot(jnp.where(upper[L], x, 0.0), jnp.where(upper[L], 0.0, x)))
            level_scores.append(pair_dot(q, kk))
            work.append((ci, hp, qe_bf, jnp.exp2(b_last), upd, level_scores, v_bf))
            filler()

    state = [st_ref[h] for h in range(HEADS)]
    o_inter = []
    for ci, hp, qe_bf, decay, upd, level_scores, v_bf in work:
        h0, h1 = 2 * hp, 2 * hp + 1
        o_inter.append(_dot_nt(qe_bf, _block_diag(state[h0].astype(BF16), state[h1].astype(BF16))))
        state[h0] = state[h0] * decay[:, :DK] + upd[0]
        state[h1] = state[h1] * decay[:, DK:] + upd[1]
    for h in range(HEADS):
        st_ref[h] = state[h]

    o_intra = []
    for ci, hp, qe_bf, decay, upd, level_scores, v_bf in work:
        scores = jnp.where(diagonal, level_scores[-1], 0.0)
        for L, s_l in zip(LEVELS[1:], level_scores[1:-1]):
            scores = scores + jnp.where(same_block[L], s_l, 0.0)
        scores = scores + level_scores[0]
        o_intra.append(_dot(scores.astype(BF16), _block_diag(v_bf[:, :DK], v_bf[:, DK:])))

    for (ci, hp, *_), oa, ob in zip(work, o_inter, o_intra):
        r = slice(ci * CHUNK, (ci + 1) * CHUNK)
        o_pair = ob + oa
        for i in range(2):
            cols = slice((2 * hp + i) * DK, (2 * hp + i + 1) * DK)
            o = o_pair[:, i * DK:(i + 1) * DK]
            o = o * lax.rsqrt(jnp.mean(o * o, axis=-1, keepdims=True) + EPS) * hg_all[:, cols]
            mixed_ref[r, cols] = o * gate_all[r, cols]


def _mix_kernel(xn_ref, xp_ref, g_ref, w_in_f32, lbl_ref, hg_ref, w_pool_f32, ps_ref, w_out_f32, cm_ref,
                out_ref, za_ref, zb_ref, st_ref, pext_ref, mixed_ref, w_in_ref, w_pool_ref, w_out_ref,
                *, tiles_per_seq):
    g = pl.program_id(0)
    si = lax.rem(g + tiles_per_seq - 1, tiles_per_seq)

    @pl.when(g == 0)
    def _():
        zb_ref[...] = jnp.zeros(zb_ref.shape, F32)
        w_in_ref[...] = w_in_f32[...].astype(BF16)
        w_pool_ref[...] = w_pool_f32[...].astype(BF16)
        w_out_ref[...] = w_out_f32[...].astype(BF16)

    @pl.when(jnp.logical_or(si == 0, g == 0))
    def _():
        st_ref[...] = jnp.zeros_like(st_ref)
        pext_ref[0:POOL_HIST, :] = jnp.zeros((POOL_HIST, KW), F32)

    args = (xn_ref, xp_ref, g_ref, w_in_ref, lbl_ref, hg_ref, w_pool_ref, ps_ref, w_out_ref, cm_ref,
            out_ref, st_ref, pext_ref, mixed_ref, si)

    @pl.when(lax.rem(g, 2) == 0)
    def _():
        _mix_step(*args, zn_ref=za_ref, zc_ref=zb_ref)

    @pl.when(lax.rem(g, 2) == 1)
    def _():
        _mix_step(*args, zn_ref=zb_ref, zc_ref=za_ref)


def _mix_step(xn_ref, xp_ref, g_ref, w_in_ref, lbl_ref, hg_ref, w_pool_ref, ps_ref, w_out_ref, cm_ref,
              out_ref, st_ref, pext_ref, mixed_ref, si, *, zn_ref, zc_ref):
    ts = xn_ref.shape[1]
    hn_bf = _rms(xn_ref[0], g_ref[...]).astype(BF16)
    blocks = list(range(IN_WIDTH // IN_BLOCK))

    def filler():
        if blocks:
            cols = slice(blocks[0] * IN_BLOCK, (blocks[0] + 1) * IN_BLOCK)
            blocks.pop(0)
            zn_ref[:, cols] = _dot(hn_bf, w_in_ref[:, cols])

    l0 = lbl_ref[0:1, :]
    l1 = lbl_ref[1:2, :]
    lmax = jnp.maximum(l0, l1)
    e0 = jnp.exp(l0 - lmax)
    lb_all = e0 / (e0 + jnp.exp(l1 - lmax))

    def pool():
        p = zc_ref[:, 4 * KW:5 * KW]
        pext_ref[POOL_HIST:POOL_HIST + ts, :] = p
        pos = (si * ts + 1 + lax.broadcasted_iota(jnp.int32, (ts, 1), 0)).astype(F32)
        for gi, w in enumerate(POOL_WINDOWS):
            cols = slice(gi * DK, (gi + 1) * DK)
            acc = pext_ref[:, cols]
            sh = 1
            while sh < w:
                acc = acc + pltpu.roll(acc, sh, 0)
                sh *= 2
            win = acc[POOL_HIST:, :]
            pooled = win / jnp.minimum(pos, float(w)) - p[:, cols]
            y = _dot(pooled.astype(BF16), w_pool_ref[gi]) * ps_ref[:, cols]
            mixed_ref[:, KW + gi * DK:KW + (gi + 1) * DK] = y
        pext_ref[0:POOL_HIST, :] = pext_ref[ts:ts + POOL_HIST, :]

    _hgrn_tile(zc_ref, ts, cm_ref[...], lb_all, hg_ref[...], st_ref, mixed_ref, filler)
    assert not blocks
    pool()
    out_ref[0] = xp_ref[0] + _dot(mixed_ref[...].astype(BF16), w_out_ref[...])


def _const_spec(shape):
    nd = len(shape)
    return pl.BlockSpec(shape, lambda *_: (0,) * nd, pipeline_mode=pl.Buffered(1))


def _mix_call(x, g, w_in, lbl, hg, w_pool, ps, w_out):
    bsz, seq, d = x.shape
    ts = TS_MIX
    ns = seq // ts
    n_tiles = bsz * ns
    cm = jnp.asarray(_cumsum_matrix(), BF16)

    def tile_next(i):
        t = jnp.minimum(i, n_tiles - 1)
        return (t // ns, t % ns, 0)

    def tile_prev(i):
        t = jnp.maximum(i - 1, 0)
        return (t // ns, t % ns, 0)

    return pl.pallas_call(
        functools.partial(_mix_kernel, tiles_per_seq=ns),
        grid=(n_tiles + 1,),
        in_specs=[
            pl.BlockSpec((1, ts, d), tile_next),
            pl.BlockSpec((1, ts, d), tile_prev),
            _const_spec((1, d)),
            _const_spec((d, IN_WIDTH)),
            _const_spec((2, KW)),
            _const_spec((1, KW)),
            _const_spec((len(POOL_WINDOWS), DK, DK)),
            _const_spec((1, KW)),
            _const_spec((d, d)),
            _const_spec((CHUNK, CHUNK)),
        ],
        out_specs=pl.BlockSpec((1, ts, d), tile_prev),
        out_shape=jax.ShapeDtypeStruct(x.shape, F32),
        scratch_shapes=[
            pltpu.VMEM((ts, IN_WIDTH), F32),
            pltpu.VMEM((ts, IN_WIDTH), F32),
            pltpu.VMEM((HEADS, DK, DK), F32),
            pltpu.VMEM((POOL_HIST + ts, KW), F32),
            pltpu.VMEM((ts, d), F32),
            pltpu.VMEM((d, IN_WIDTH), BF16),
            pltpu.VMEM((len(POOL_WINDOWS), DK, DK), BF16),
            pltpu.VMEM((d, d), BF16),
        ],
        compiler_params=pltpu.CompilerParams(
            dimension_semantics=("arbitrary",), vmem_limit_bytes=VMEM_LIMIT),
        name="mix",
    )(x, x, g, w_in, lbl, hg, w_pool, ps, w_out, cm)


def _memkv_kernel(mem_ref, g_ref, wk_ref, wv_ref, kt_ref, v_ref):
    hm = _rms(mem_ref[0], g_ref[...]).astype(BF16)
    k = _dot(hm, wk_ref[...].astype(BF16)) * (XDIM ** -0.5)
    kt_ref[0] = k.T.astype(BF16)
    v_ref[0] = _dot(hm, wv_ref[...].astype(BF16)).astype(BF16)


def _memkv_call(mem, g, wk, wv):
    bsz, m, d = mem.shape
    return pl.pallas_call(
        _memkv_kernel,
        grid=(bsz,),
        in_specs=[
            pl.BlockSpec((1, m, d), lambda b: (b, 0, 0)),
            _const_spec((1, d)),
            _const_spec((d, d)),
            _const_spec((d, d)),
        ],
        out_specs=[
            pl.BlockSpec((1, d, m), lambda b: (b, 0, 0)),
            pl.BlockSpec((1, m, d), lambda b: (b, 0, 0)),
        ],
        out_shape=[
            jax.ShapeDtypeStruct((bsz, d, m), BF16),
            jax.ShapeDtypeStruct((bsz, m, d), BF16),
        ],
        compiler_params=pltpu.CompilerParams(
            dimension_semantics=("arbitrary",), vmem_limit_bytes=VMEM_LIMIT),
        name="memkv",
    )(mem, g, wk, wv)


def _tail_kernel(x_ref, gx_ref, wq_ref, kt_ref, v_ref, wo_ref, gf_ref, w1_ref, w2_ref, gl_ref, out_ref):
    hm = x_ref.shape[1] // 2
    head_cols = [slice(h * XDIM, (h + 1) * XDIM) for h in range(XHEADS)]
    rows = [slice(i * hm, (i + 1) * hm) for i in range(2)]
    xs = [x_ref[0, r, :] for r in rows]

    def q_proj(x):
        return _dot(_rms(x, gx_ref[...]).astype(BF16), wq_ref[...].astype(BF16)).astype(BF16)

    def scores(xq):
        return [_dot(xq[:, c], kt_ref[0, c, :]) for c in head_cols]

    def attend(ss):
        outs = []
        for s, c in zip(ss, head_cols):
            e = jnp.exp(s - jnp.max(s, axis=-1, keepdims=True))
            p = e * (1.0 / jnp.sum(e, axis=-1, keepdims=True))
            outs.append(_dot(p.astype(BF16), v_ref[0, :, c]).astype(BF16))
        return jnp.concatenate(outs, axis=1)

    def o_proj(x, att):
        return x + _dot(att, wo_ref[...].astype(BF16))

    def mlp(i, x):
        hf = _rms(x, gf_ref[...]).astype(BF16)
        acc = x
        for j in range(D_FF // FF_BLOCK):
            cols = slice(j * FF_BLOCK, (j + 1) * FF_BLOCK)
            u = jnp.maximum(_dot(hf, w1_ref[:, cols].astype(BF16)), 0.0)
            acc = acc + _dot((u * u).astype(BF16), w2_ref[cols, :].astype(BF16))
        out_ref[0, rows[i], :] = _rms(acc, gl_ref[...])

    s_a = scores(q_proj(xs[0]))
    xq_b = q_proj(xs[1])
    att_a = attend(s_a)
    s_b = scores(xq_b)
    x2_a = o_proj(xs[0], att_a)
    mlp(0, x2_a)
    x2_b = o_proj(xs[1], attend(s_b))
    mlp(1, x2_b)


def _tail_call(x, gx, wq, kt, v, wo, gf, w1, w2, gl):
    bsz, seq, d = x.shape
    tm = TM_TAIL
    return pl.pallas_call(
        _tail_kernel,
        grid=(bsz, seq // tm),
        in_specs=[
            pl.BlockSpec((1, tm, d), lambda b, s: (b, s, 0)),
            _const_spec((1, d)),
            _const_spec((d, d)),
            pl.BlockSpec((1, d, MEM_LEN), lambda b, s: (b, 0, 0)),
            pl.BlockSpec((1, MEM_LEN, d), lambda b, s: (b, 0, 0)),
            _const_spec((d, d)),
            _const_spec((1, d)),
            _const_spec((d, D_FF)),
            _const_spec((D_FF, d)),
            _const_spec((1, d)),
        ],
        out_specs=pl.BlockSpec((1, tm, d), lambda b, s: (b, s, 0)),
        out_shape=jax.ShapeDtypeStruct(x.shape, F32),
        compiler_params=pltpu.CompilerParams(
            dimension_semantics=("arbitrary", "arbitrary"), vmem_limit_bytes=TAIL_VMEM_LIMIT),
        name="tail",
    )(x, gx, wq, kt, v, wo, gf, w1, w2, gl)


def kernel(x, mem, norm_mix_g, w_in, lb_logits, hgrn_norm_g, w_pool, pool_scale, w_out,
           norm_x_g, norm_mem_g, w_xq, w_xk, w_xv, w_xo, norm_ffn_g, w_ff1, w_ff2, final_norm_g):
    bsz, seq, d = x.shape
    depth = norm_mix_g.shape[0]
    assert depth == 1 and lb_logits.shape[0] == 2
    row = lambda a: a.reshape(1, -1).astype(F32)
    x = _mix_call(x, row(norm_mix_g), w_in.reshape(d, IN_WIDTH), lb_logits.astype(F32),
                  row(hgrn_norm_g), w_pool.reshape(len(POOL_WINDOWS), DK, DK), row(pool_scale),
                  w_out.reshape(d, d))
    kt, v = _memkv_call(mem, row(norm_mem_g), w_xk.reshape(d, d), w_xv.reshape(d, d))
    return _tail_call(x, row(norm_x_g), w_xq.reshape(d, d), kt, v, w_xo.reshape(d, d),
                      row(norm_ffn_g), w_ff1.reshape(d, D_FF), w_ff2.reshape(D_FF, d), row(final_norm_g))
```

```python
import functools

import jax
import jax.numpy as jnp
import numpy as np
from jax import lax
from jax.experimental import pallas as pl
from jax.experimental.pallas import tpu as pltpu

D_MODEL = 1024
HEADS = 4
DK = 128
KW = HEADS * DK
PW = 2 * DK
CHUNK = 64
LEVELS = (32, 16, 8, 4, 2, 1)
POOL_WINDOWS = (2, 4, 8, 16)
POOL_HIST = 16
IN_WIDTH = 5 * KW
MEM_LEN = 256
XHEADS = 4
XDIM = 256
D_FF = 4096
EPS = 1e-6
LOG2E = 1.4426950408889634

TS_MIX = 512
IN_BLOCK = 256
TM_TAIL = 512
FF_BLOCK = 1024
VMEM_LIMIT = 56 * 1024 * 1024
TAIL_VMEM_LIMIT = 60 * 1024 * 1024

F32 = jnp.float32
BF16 = jnp.bfloat16


def _rms(x, g):
    return x * lax.rsqrt(jnp.mean(x * x, axis=-1, keepdims=True) + EPS) * g


def _dot(a, b):
    return jnp.dot(a, b, preferred_element_type=F32)


def _dot_nt(a, b):
    return lax.dot_general(a, b, (((1,), (1,)), ((), ())), preferred_element_type=F32)


def _dot_tn(a, b):
    return lax.dot_general(a, b, (((0,), (0,)), ((), ())), preferred_element_type=F32)


def _sigmoid(x):
    return 1.0 / (1.0 + jnp.exp(-x))


def _block_diag(a, b):
    z = jnp.zeros_like(a)
    return jnp.concatenate([jnp.concatenate([a, z], axis=1), jnp.concatenate([z, b], axis=1)], axis=0)


def _cumsum_matrix():
    t = np.arange(CHUNK)[:, None]
    s = np.arange(CHUNK)[None, :]
    return (s <= t).astype(np.float32)


def _hgrn_tile(z_ref, ts, cm, lb_all, hg_all, st_ref, mixed_ref, filler):
    n_chunks = ts // CHUNK
    qp = z_ref[:, 0:KW]
    fp = z_ref[:, KW:2 * KW]
    gp = z_ref[:, 3 * KW:4 * KW]

    f = lb_all + (1.0 - lb_all) * _sigmoid(fp)
    kk_all = 1.0 - f
    q_all = qp * _sigmoid(qp)
    gate_all = gp * _sigmoid(gp)

    lf2 = jnp.log(f) * LOG2E
    hi = lf2.astype(BF16)
    lo = (lf2 - hi.astype(F32)).astype(BF16)
    b_chunks = []
    for ci in range(n_chunks):
        r = slice(ci * CHUNK, (ci + 1) * CHUNK)
        b_chunks.append(_dot(cm, hi[r]) + _dot(cm, lo[r]))
    filler()
    filler()

    row = lax.broadcasted_iota(jnp.int32, (CHUNK, PW), 0)
    upper = {L: jnp.bitwise_and(row, 2 * L - 1) >= L for L in LEVELS}
    out_row = lax.broadcasted_iota(jnp.int32, (CHUNK, DK), 0)
    out_col = jnp.bitwise_and(lax.broadcasted_iota(jnp.int32, (CHUNK, DK), 1), CHUNK - 1)
    same_block = {L: jnp.bitwise_and(jnp.bitwise_xor(out_row, out_col), -2 * L) == 0 for L in LEVELS[1:]}
    diagonal = out_row == out_col

    def ref_rows(a, first, period, width):
        parts = [jnp.broadcast_to(a[r:r + 1, :], (width, a.shape[1])) for r in range(first, CHUNK, period)]
        return parts[0] if len(parts) == 1 else jnp.concatenate(parts, axis=0)

    def level_ref(b, L):
        if L >= 4:
            return ref_rows(b, L - 1, 2 * L, 2 * L)
        return jnp.where(jnp.bitwise_and(row, 7) < 4, ref_rows(b, 1, 8, 8), ref_rows(b, 5, 8, 8))

    def pair_dot(qe, ke):
        ke = ke.astype(BF16)
        return _dot_nt(qe.astype(BF16), _block_diag(ke[:, :DK], ke[:, DK:]))

    work = []
    for ci in range(n_chunks):
        r = slice(ci * CHUNK, (ci + 1) * CHUNK)
        for hp in range(HEADS // 2):
            pc = slice(hp * PW, (hp + 1) * PW)
            q, kk, b = q_all[r, pc], kk_all[r, pc], b_chunks[ci][:, pc]
            b_last = b[CHUNK - 1:CHUNK, :]
            v_bf = z_ref[r, 2 * KW + hp * PW:2 * KW + (hp + 1) * PW].astype(BF16)

            qe_bf = (q * jnp.exp2(b)).astype(BF16)
            k_dec = (kk * jnp.exp2(b_last - b)).astype(BF16)
            upd = [_dot_tn(v_bf[:, i * DK:(i + 1) * DK], k_dec[:, i * DK:(i + 1) * DK]) for i in range(2)]

            level_scores = []
            for L in LEVELS:
                if L == 1:
                    x = jnp.where(upper[L], q * f[r, pc], kk)
                else:
                    x = jnp.where(upper[L], q, kk) * jnp.exp2(-jnp.abs(b - level_ref(b, L)))
                level_scores.append(pair_dot(jnp.where(upper[L], x, 0.0), jnp.where(upper[L], 0.0, x)))
            level_scores.append(pair_dot(q, kk))
            work.append((ci, hp, qe_bf, jnp.exp2(b_last), upd, level_scores, v_bf))
            if len(work) % (n_chunks // 4) == 0:
                filler()

    state = [st_ref[h] for h in range(HEADS)]
    o_inter = []
    for ci, hp, qe_bf, decay, upd, level_scores, v_bf in work:
        h0, h1 = 2 * hp, 2 * hp + 1
        o_inter.append(_dot_nt(qe_bf, _block_diag(state[h0].astype(BF16), state[h1].astype(BF16))))
        state[h0] = state[h0] * decay[:, :DK] + upd[0]
        state[h1] = state[h1] * decay[:, DK:] + upd[1]
    for h in range(HEADS):
        st_ref[h] = state[h]

    o_intra = []
    for ci, hp, qe_bf, decay, upd, level_scores, v_bf in work:
        scores = jnp.where(diagonal, level_scores[-1], 0.0)
        for L, s_l in zip(LEVELS[1:], level_scores[1:-1]):
            scores = scores + jnp.where(same_block[L], s_l, 0.0)
        scores = scores + level_scores[0]
        o_intra.append(_dot(scores.astype(BF16), _block_diag(v_bf[:, :DK], v_bf[:, DK:])))

    for (ci, hp, *_), oa, ob in zip(work, o_inter, o_intra):
        r = slice(ci * CHUNK, (ci + 1) * CHUNK)
        o_pair = ob + oa
        for i in range(2):
            cols = slice((2 * hp + i) * DK, (2 * hp + i + 1) * DK)
            o = o_pair[:, i * DK:(i + 1) * DK]
            o = o * lax.rsqrt(jnp.mean(o * o, axis=-1, keepdims=True) + EPS) * hg_all[:, cols]
            mixed_ref[r, cols] = o * gate_all[r, cols]


def _mix_kernel(xn_ref, xp_ref, g_ref, w_in_f32, lbl_ref, hg_ref, w_pool_f32, ps_ref, w_out_f32, cm_ref,
                out_ref, za_ref, zb_ref, st_ref, pext_ref, mixed_ref, w_in_ref, w_pool_ref, w_out_ref,
                *, tiles_per_seq):
    g = pl.program_id(0)
    si = lax.rem(g + tiles_per_seq - 1, tiles_per_seq)

    @pl.when(g == 0)
    def _():
        zb_ref[...] = jnp.zeros(zb_ref.shape, F32)
        w_in_ref[...] = w_in_f32[...].astype(BF16)
        w_pool_ref[...] = w_pool_f32[...].astype(BF16)
        w_out_ref[...] = w_out_f32[...].astype(BF16)

    @pl.when(jnp.logical_or(si == 0, g == 0))
    def _():
        st_ref[...] = jnp.zeros_like(st_ref)
        pext_ref[0:POOL_HIST, :] = jnp.zeros((POOL_HIST, KW), F32)

    args = (xn_ref, xp_ref, g_ref, w_in_ref, lbl_ref, hg_ref, w_pool_ref, ps_ref, w_out_ref, cm_ref,
            out_ref, st_ref, pext_ref, mixed_ref, si)

    @pl.when(lax.rem(g, 2) == 0)
    def _():
        _mix_step(*args, zn_ref=za_ref, zc_ref=zb_ref)

    @pl.when(lax.rem(g, 2) == 1)
    def _():
        _mix_step(*args, zn_ref=zb_ref, zc_ref=za_ref)


def _mix_step(xn_ref, xp_ref, g_ref, w_in_ref, lbl_ref, hg_ref, w_pool_ref, ps_ref, w_out_ref, cm_ref,
              out_ref, st_ref, pext_ref, mixed_ref, si, *, zn_ref, zc_ref):
    ts = xn_ref.shape[1]
    hn_bf = _rms(xn_ref[0], g_ref[...]).astype(BF16)
    blocks = list(range(IN_WIDTH // IN_BLOCK))

    def filler():
        if blocks:
            cols = slice(blocks[0] * IN_BLOCK, (blocks[0] + 1) * IN_BLOCK)
            blocks.pop(0)
            zn_ref[:, cols] = _dot(hn_bf, w_in_ref[:, cols])

    l0 = lbl_ref[0:1, :]
    l1 = lbl_ref[1:2, :]
    lmax = jnp.maximum(l0, l1)
    e0 = jnp.exp(l0 - lmax)
    lb_all = e0 / (e0 + jnp.exp(l1 - lmax))

    def pool():
        p = zc_ref[:, 4 * KW:5 * KW]
        pext_ref[POOL_HIST:POOL_HIST + ts, :] = p
        pos = (si * ts + 1 + lax.broadcasted_iota(jnp.int32, (ts, 1), 0)).astype(F32)
        for gi, w in enumerate(POOL_WINDOWS):
            cols = slice(gi * DK, (gi + 1) * DK)
            acc = pext_ref[:, cols]
            sh = 1
            while sh < w:
                acc = acc + pltpu.roll(acc, sh, 0)
                sh *= 2
            win = acc[POOL_HIST:, :]
            pooled = win / jnp.minimum(pos, float(w)) - p[:, cols]
            y = _dot(pooled.astype(BF16), w_pool_ref[gi]) * ps_ref[:, cols]
            mixed_ref[:, KW + gi * DK:KW + (gi + 1) * DK] = y
        pext_ref[0:POOL_HIST, :] = pext_ref[ts:ts + POOL_HIST, :]

    _hgrn_tile(zc_ref, ts, cm_ref[...], lb_all, hg_ref[...], st_ref, mixed_ref, filler)
    assert not blocks
    pool()
    out_ref[0] = xp_ref[0] + _dot(mixed_ref[...].astype(BF16), w_out_ref[...])


def _const_spec(shape):
    nd = len(shape)
    return pl.BlockSpec(shape, lambda *_: (0,) * nd, pipeline_mode=pl.Buffered(1))


def _mix_call(x, g, w_in, lbl, hg, w_pool, ps, w_out):
    bsz, seq, d = x.shape
    ts = TS_MIX
    ns = seq // ts
    n_tiles = bsz * ns
    cm = jnp.asarray(_cumsum_matrix(), BF16)

    def tile_next(i):
        t = jnp.minimum(i, n_tiles - 1)
        return (t // ns, t % ns, 0)

    def tile_prev(i):
        t = jnp.maximum(i - 1, 0)
        return (t // ns, t % ns, 0)

    return pl.pallas_call(
        functools.partial(_mix_kernel, tiles_per_seq=ns),
        grid=(n_tiles + 1,),
        in_specs=[
            pl.BlockSpec((1, ts, d), tile_next),
            pl.BlockSpec((1, ts, d), tile_prev),
            _const_spec((1, d)),
            _const_spec((d, IN_WIDTH)),
            _const_spec((2, KW)),
            _const_spec((1, KW)),
            _const_spec((len(POOL_WINDOWS), DK, DK)),
            _const_spec((1, KW)),
            _const_spec((d, d)),
            _const_spec((CHUNK, CHUNK)),
        ],
        out_specs=pl.BlockSpec((1, ts, d), tile_prev),
        out_shape=jax.ShapeDtypeStruct(x.shape, F32),
        scratch_shapes=[
            pltpu.VMEM((ts, IN_WIDTH), F32),
            pltpu.VMEM((ts, IN_WIDTH), F32),
            pltpu.VMEM((HEADS, DK, DK), F32),
            pltpu.VMEM((POOL_HIST + ts, KW), F32),
            pltpu.VMEM((ts, d), F32),
            pltpu.VMEM((d, IN_WIDTH), BF16),
            pltpu.VMEM((len(POOL_WINDOWS), DK, DK), BF16),
            pltpu.VMEM((d, d), BF16),
        ],
        compiler_params=pltpu.CompilerParams(
            dimension_semantics=("arbitrary",), vmem_limit_bytes=VMEM_LIMIT),
        name="mix",
    )(x, x, g, w_in, lbl, hg, w_pool, ps, w_out, cm)


def _memkv_kernel(mem_ref, g_ref, wk_ref, wv_ref, kt_ref, v_ref):
    hm = _rms(mem_ref[0], g_ref[...]).astype(BF16)
    k = _dot(hm, wk_ref[...].astype(BF16)) * (XDIM ** -0.5)
    kt_ref[0] = k.T.astype(BF16)
    v_ref[0] = _dot(hm, wv_ref[...].astype(BF16)).astype(BF16)


def _memkv_call(mem, g, wk, wv):
    bsz, m, d = mem.shape
    return pl.pallas_call(
        _memkv_kernel,
        grid=(bsz,),
        in_specs=[
            pl.BlockSpec((1, m, d), lambda b: (b, 0, 0)),
            _const_spec((1, d)),
            _const_spec((d, d)),
            _const_spec((d, d)),
        ],
        out_specs=[
            pl.BlockSpec((1, d, m), lambda b: (b, 0, 0)),
            pl.BlockSpec((1, m, d), lambda b: (b, 0, 0)),
        ],
        out_shape=[
            jax.ShapeDtypeStruct((bsz, d, m), BF16),
            jax.ShapeDtypeStruct((bsz, m, d), BF16),
        ],
        compiler_params=pltpu.CompilerParams(
            dimension_semantics=("arbitrary",), vmem_limit_bytes=VMEM_LIMIT),
        name="memkv",
    )(mem, g, wk, wv)


def _tail_kernel(x_ref, gx_ref, wq_ref, kt_ref, v_ref, wo_ref, gf_ref, w1_ref, w2_ref, gl_ref, out_ref):
    hm = x_ref.shape[1] // 2
    head_cols = [slice(h * XDIM, (h + 1) * XDIM) for h in range(XHEADS)]
    rows = [slice(i * hm, (i + 1) * hm) for i in range(2)]
    xs = [x_ref[0, r, :] for r in rows]

    def q_proj(x):
        return _dot(_rms(x, gx_ref[...]).astype(BF16), wq_ref[...].astype(BF16)).astype(BF16)

    def scores(xq):
        return [_dot(xq[:, c], kt_ref[0, c, :]) for c in head_cols]

    def attend(ss):
        outs = []
        for s, c in zip(ss, head_cols):
            e = jnp.exp(s - jnp.max(s, axis=-1, keepdims=True))
            p = e * (1.0 / jnp.sum(e, axis=-1, keepdims=True))
            outs.append(_dot(p.astype(BF16), v_ref[0, :, c]).astype(BF16))
        return jnp.concatenate(outs, axis=1)

    def o_proj(x, att):
        return x + _dot(att, wo_ref[...].astype(BF16))

    def mlp(i, x):
        hf = _rms(x, gf_ref[...]).astype(BF16)
        acc = x
        for j in range(D_FF // FF_BLOCK):
            cols = slice(j * FF_BLOCK, (j + 1) * FF_BLOCK)
            u = jnp.maximum(_dot(hf, w1_ref[:, cols].astype(BF16)), 0.0)
            acc = acc + _dot((u * u).astype(BF16), w2_ref[cols, :].astype(BF16))
        out_ref[0, rows[i], :] = _rms(acc, gl_ref[...])

    s_a = scores(q_proj(xs[0]))
    xq_b = q_proj(xs[1])
    att_a = attend(s_a)
    s_b = scores(xq_b)
    x2_a = o_proj(xs[0], att_a)
    mlp(0, x2_a)
    x2_b = o_proj(xs[1], attend(s_b))
    mlp(1, x2_b)


def _tail_call(x, gx, wq, kt, v, wo, gf, w1, w2, gl):
    bsz, seq, d = x.shape
    tm = TM_TAIL
    return pl.pallas_call(
        _tail_kernel,
        grid=(bsz, seq // tm),
        in_specs=[
            pl.BlockSpec((1, tm, d), lambda b, s: (b, s, 0)),
            _const_spec((1, d)),
            _const_spec((d, d)),
            pl.BlockSpec((1, d, MEM_LEN), lambda b, s: (b, 0, 0)),
            pl.BlockSpec((1, MEM_LEN, d), lambda b, s: (b, 0, 0)),
            _const_spec((d, d)),
            _const_spec((1, d)),
            _const_spec((d, D_FF)),
            _const_spec((D_FF, d)),
            _const_spec((1, d)),
        ],
        out_specs=pl.BlockSpec((1, tm, d), lambda b, s: (b, s, 0)),
        out_shape=jax.ShapeDtypeStruct(x.shape, F32),
        compiler_params=pltpu.CompilerParams(
            dimension_semantics=("arbitrary", "arbitrary"), vmem_limit_bytes=TAIL_VMEM_LIMIT),
        name="tail",
    )(x, gx, wq, kt, v, wo, gf, w1, w2, gl)


def kernel(x, mem, norm_mix_g, w_in, lb_logits, hgrn_norm_g, w_pool, pool_scale, w_out,
           norm_x_g, norm_mem_g, w_xq, w_xk, w_xv, w_xo, norm_ffn_g, w_ff1, w_ff2, final_norm_g):
    bsz, seq, d = x.shape
    depth = norm_mix_g.shape[0]
    assert depth == 1 and lb_logits.shape[0] == 2
    row = lambda a: a.reshape(1, -1).astype(F32)
    x = _mix_call(x, row(norm_mix_g), w_in.reshape(d, IN_WIDTH), lb_logits.astype(F32),
                  row(hgrn_norm_g), w_pool.reshape(len(POOL_WINDOWS), DK, DK), row(pool_scale),
                  w_out.reshape(d, d))
    kt, v = _memkv_call(mem, row(norm_mem_g), w_xk.reshape(d, d), w_xv.reshape(d, d))
    return _tail_call(x, row(norm_x_g), w_xq.reshape(d, d), kt, v, w_xo.reshape(d, d),
                      row(norm_ffn_g), w_ff1.reshape(d, D_FF), w_ff2.reshape(D_FF, d), row(final_norm_g))
```

```python
import functools

import jax
import jax.numpy as jnp
import numpy as np
from jax import lax
from jax.experimental import pallas as pl
from jax.experimental.pallas import tpu as pltpu

D_MODEL = 1024
HEADS = 4
DK = 128
KW = HEADS * DK
PW = 2 * DK
CHUNK = 64
LEVELS = (32, 16, 8, 4, 2, 1)
POOL_WINDOWS = (2, 4, 8, 16)
POOL_HIST = 16
IN_WIDTH = 5 * KW
MEM_LEN = 256
XHEADS = 4
XDIM = 256
D_FF = 4096
EPS = 1e-6
LOG2E = 1.4426950408889634

TS_MIX = 512
SUB_TILE = 256
IN_BLOCK = 256
TM_TAIL = 512
FF_BLOCK = 1024
VMEM_LIMIT = 56 * 1024 * 1024
TAIL_VMEM_LIMIT = 60 * 1024 * 1024

F32 = jnp.float32
BF16 = jnp.bfloat16


def _rms(x, g):
    return x * lax.rsqrt(jnp.mean(x * x, axis=-1, keepdims=True) + EPS) * g


def _dot(a, b):
    return jnp.dot(a, b, preferred_element_type=F32)


def _dot_nt(a, b):
    return lax.dot_general(a, b, (((1,), (1,)), ((), ())), preferred_element_type=F32)


def _dot_tn(a, b):
    return lax.dot_general(a, b, (((0,), (0,)), ((), ())), preferred_element_type=F32)


def _sigmoid(x):
    return 1.0 / (1.0 + jnp.exp(-x))


def _block_diag(a, b):
    z = jnp.zeros_like(a)
    return jnp.concatenate([jnp.concatenate([a, z], axis=1), jnp.concatenate([z, b], axis=1)], axis=0)


def _cumsum_matrix():
    t = np.arange(CHUNK)[:, None]
    s = np.arange(CHUNK)[None, :]
    return (s <= t).astype(np.float32)


def _hgrn_tile(z_ref, ts, cm, lb_all, hg_all, st_ref, mixed_ref, filler):
    n_chunks = ts // CHUNK
    qp = z_ref[:, 0:KW]
    fp = z_ref[:, KW:2 * KW]
    gp = z_ref[:, 3 * KW:4 * KW]

    f = lb_all + (1.0 - lb_all) * _sigmoid(fp)
    kk_all = 1.0 - f
    q_all = qp * _sigmoid(qp)
    gate_all = gp * _sigmoid(gp)

    lf2 = jnp.log(f) * LOG2E
    hi = lf2.astype(BF16)
    lo = (lf2 - hi.astype(F32)).astype(BF16)
    b_chunks = []
    for ci in range(n_chunks):
        r = slice(ci * CHUNK, (ci + 1) * CHUNK)
        b_chunks.append(_dot(cm, hi[r]) + _dot(cm, lo[r]))
    filler()
    filler()

    row = lax.broadcasted_iota(jnp.int32, (CHUNK, PW), 0)
    upper = {L: jnp.bitwise_and(row, 2 * L - 1) >= L for L in LEVELS}
    out_row = lax.broadcasted_iota(jnp.int32, (CHUNK, DK), 0)
    out_col = jnp.bitwise_and(lax.broadcasted_iota(jnp.int32, (CHUNK, DK), 1), CHUNK - 1)
    same_block = {L: jnp.bitwise_and(jnp.bitwise_xor(out_row, out_col), -2 * L) == 0 for L in LEVELS[1:]}
    diagonal = out_row == out_col

    def ref_rows(a, first, period, width):
        parts = [jnp.broadcast_to(a[r:r + 1, :], (width, a.shape[1])) for r in range(first, CHUNK, period)]
        return parts[0] if len(parts) == 1 else jnp.concatenate(parts, axis=0)

    def level_ref(b, L):
        if L >= 4:
            return ref_rows(b, L - 1, 2 * L, 2 * L)
        return jnp.where(jnp.bitwise_and(row, 7) < 4, ref_rows(b, 1, 8, 8), ref_rows(b, 5, 8, 8))

    def pair_dot(qe, ke):
        rhs = _block_diag(ke[:, :DK].T, ke[:, DK:].T).astype(BF16)
        return _dot(qe.astype(BF16), rhs)

    work = []
    for ci in range(n_chunks):
        r = slice(ci * CHUNK, (ci + 1) * CHUNK)
        for hp in range(HEADS // 2):
            pc = slice(hp * PW, (hp + 1) * PW)
            q, kk, b = q_all[r, pc], kk_all[r, pc], b_chunks[ci][:, pc]
            b_last = b[CHUNK - 1:CHUNK, :]
            v_bf = z_ref[r, 2 * KW + hp * PW:2 * KW + (hp + 1) * PW].astype(BF16)

            qe_bf = (q * jnp.exp2(b)).astype(BF16)
            k_dec = (kk * jnp.exp2(b_last - b)).astype(BF16)
            upd = [_dot_tn(v_bf[:, i * DK:(i + 1) * DK], k_dec[:, i * DK:(i + 1) * DK]) for i in range(2)]

            level_scores = []
            for L in LEVELS:
                if L == 1:
                    x = jnp.where(upper[L], q * f[r, pc], kk)
                else:
                    x = jnp.where(upper[L], q, kk) * jnp.exp2(-jnp.abs(b - level_ref(b, L)))
                level_scores.append(pair_dot(jnp.where(upper[L], x, 0.0), jnp.where(upper[L], 0.0, x)))
            level_scores.append(pair_dot(q, kk))
            work.append((ci, hp, qe_bf, jnp.exp2(b_last), upd, level_scores, v_bf))
            if len(work) % (n_chunks // 4) == 0:
                filler()

    state = [st_ref[h] for h in range(HEADS)]
    o_inter = []
    for ci, hp, qe_bf, decay, upd, level_scores, v_bf in work:
        h0, h1 = 2 * hp, 2 * hp + 1
        o_inter.append(_dot_nt(qe_bf, _block_diag(state[h0].astype(BF16), state[h1].astype(BF16))))
        state[h0] = state[h0] * decay[:, :DK] + upd[0]
        state[h1] = state[h1] * decay[:, DK:] + upd[1]
    for h in range(HEADS):
        st_ref[h] = state[h]

    o_intra = []
    for ci, hp, qe_bf, decay, upd, level_scores, v_bf in work:
        scores = jnp.where(diagonal, level_scores[-1], 0.0)
        for L, s_l in zip(LEVELS[1:], level_scores[1:-1]):
            scores = scores + jnp.where(same_block[L], s_l, 0.0)
        scores = scores + level_scores[0]
        o_intra.append(_dot(scores.astype(BF16), _block_diag(v_bf[:, :DK], v_bf[:, DK:])))

    for (ci, hp, *_), oa, ob in zip(work, o_inter, o_intra):
        r = slice(ci * CHUNK, (ci + 1) * CHUNK)
        o_pair = ob + oa
        for i in range(2):
            cols = slice((2 * hp + i) * DK, (2 * hp + i + 1) * DK)
            o = o_pair[:, i * DK:(i + 1) * DK]
            o = o * lax.rsqrt(jnp.mean(o * o, axis=-1, keepdims=True) + EPS) * hg_all[:, cols]
            mixed_ref[r, cols] = o * gate_all[r, cols]


def _mix_kernel(xn_ref, xp_ref, g_ref, w_in_f32, lbl_ref, hg_ref, w_pool_f32, ps_ref, w_out_f32, cm_ref,
                out_ref, za_ref, zb_ref, st_ref, pext_ref, mixed_ref, w_in_ref, w_pool_ref, w_out_ref,
                *, tiles_per_seq):
    g = pl.program_id(0)
    si = lax.rem(g + tiles_per_seq - 1, tiles_per_seq)

    @pl.when(g == 0)
    def _():
        zb_ref[...] = jnp.zeros(zb_ref.shape, F32)
        w_in_ref[...] = w_in_f32[...].astype(BF16)
        w_pool_ref[...] = w_pool_f32[...].astype(BF16)
        w_out_ref[...] = w_out_f32[...].astype(BF16)

    @pl.when(jnp.logical_or(si == 0, g == 0))
    def _():
        st_ref[...] = jnp.zeros_like(st_ref)
        pext_ref[0:POOL_HIST, :] = jnp.zeros((POOL_HIST, KW), F32)

    args = (xn_ref, xp_ref, g_ref, w_in_ref, lbl_ref, hg_ref, w_pool_ref, ps_ref, w_out_ref, cm_ref,
            out_ref, st_ref, pext_ref, mixed_ref, si)

    @pl.when(lax.rem(g, 2) == 0)
    def _():
        _mix_step(*args, zn_ref=za_ref, zc_ref=zb_ref)

    @pl.when(lax.rem(g, 2) == 1)
    def _():
        _mix_step(*args, zn_ref=zb_ref, zc_ref=za_ref)


def _mix_step(xn_ref, xp_ref, g_ref, w_in_ref, lbl_ref, hg_ref, w_pool_ref, ps_ref, w_out_ref, cm_ref,
              out_ref, st_ref, pext_ref, mixed_ref, si, *, zn_ref, zc_ref):
    ts = xn_ref.shape[1]
    d = xn_ref.shape[2]
    n_sub = ts // SUB_TILE
    hn_bf = _rms(xn_ref[0], g_ref[...]).astype(BF16)

    def in_proj_unit(blk):
        def run():
            cols = slice(blk * IN_BLOCK, (blk + 1) * IN_BLOCK)
            zn_ref[:, cols] = _dot(hn_bf, w_in_ref[:, cols])
        return run

    def out_proj_unit(sub, blk):
        def run():
            r = slice(sub * SUB_TILE, (sub + 1) * SUB_TILE)
            cols = slice(blk * IN_BLOCK, (blk + 1) * IN_BLOCK)
            out_ref[0, r, cols] = xp_ref[0, r, cols] + _dot(mixed_ref[r, :].astype(BF16), w_out_ref[:, cols])
        return run

    in_units = [in_proj_unit(blk) for blk in range(IN_WIDTH // IN_BLOCK)]
    per_sub = -(-len(in_units) // n_sub)

    l0 = lbl_ref[0:1, :]
    l1 = lbl_ref[1:2, :]
    lmax = jnp.maximum(l0, l1)
    e0 = jnp.exp(l0 - lmax)
    lb_all = e0 / (e0 + jnp.exp(l1 - lmax))

    def pool(sub):
        r = slice(sub * SUB_TILE, (sub + 1) * SUB_TILE)
        p = zc_ref[r, 4 * KW:5 * KW]
        pext_ref[POOL_HIST:, :] = p
        pos = (si * ts + sub * SUB_TILE + 1 + lax.broadcasted_iota(jnp.int32, (SUB_TILE, 1), 0)).astype(F32)
        for gi, w in enumerate(POOL_WINDOWS):
            cols = slice(gi * DK, (gi + 1) * DK)
            acc = pext_ref[:, cols]
            sh = 1
            while sh < w:
                acc = acc + pltpu.roll(acc, sh, 0)
                sh *= 2
            win = acc[POOL_HIST:, :]
            pooled = win / jnp.minimum(pos, float(w)) - p[:, cols]
            y = _dot(pooled.astype(BF16), w_pool_ref[gi]) * ps_ref[:, cols]
            mixed_ref[r, KW + gi * DK:KW + (gi + 1) * DK] = y
        pext_ref[0:POOL_HIST, :] = pext_ref[SUB_TILE:SUB_TILE + POOL_HIST, :]

    pending = []
    for sub in range(n_sub):
        units = pending + in_units[sub * per_sub:(sub + 1) * per_sub]

        def filler():
            if units:
                units.pop(0)()

        r = pl.ds(sub * SUB_TILE, SUB_TILE)
        _hgrn_tile(zc_ref.at[r], SUB_TILE, cm_ref[...], lb_all, hg_ref[...], st_ref, mixed_ref.at[r], filler)
        while units:
            filler()
        pool(sub)
        pending = [out_proj_unit(sub, blk) for blk in range(d // IN_BLOCK)]
    for unit in pending:
        unit()


def _const_spec(shape):
    nd = len(shape)
    return pl.BlockSpec(shape, lambda *_: (0,) * nd, pipeline_mode=pl.Buffered(1))


def _mix_call(x, g, w_in, lbl, hg, w_pool, ps, w_out):
    bsz, seq, d = x.shape
    ts = TS_MIX
    ns = seq // ts
    n_tiles = bsz * ns
    cm = jnp.asarray(_cumsum_matrix(), BF16)

    def tile_next(i):
        t = jnp.minimum(i, n_tiles - 1)
        return (t // ns, t % ns, 0)

    def tile_prev(i):
        t = jnp.maximum(i - 1, 0)
        return (t // ns, t % ns, 0)

    return pl.pallas_call(
        functools.partial(_mix_kernel, tiles_per_seq=ns),
        grid=(n_tiles + 1,),
        in_specs=[
            pl.BlockSpec((1, ts, d), tile_next),
            pl.BlockSpec((1, ts, d), tile_prev),
            _const_spec((1, d)),
            _const_spec((d, IN_WIDTH)),
            _const_spec((2, KW)),
            _const_spec((1, KW)),
            _const_spec((len(POOL_WINDOWS), DK, DK)),
            _const_spec((1, KW)),
            _const_spec((d, d)),
            _const_spec((CHUNK, CHUNK)),
        ],
        out_specs=pl.BlockSpec((1, ts, d), tile_prev),
        out_shape=jax.ShapeDtypeStruct(x.shape, F32),
        scratch_shapes=[
            pltpu.VMEM((ts, IN_WIDTH), F32),
            pltpu.VMEM((ts, IN_WIDTH), F32),
            pltpu.VMEM((HEADS, DK, DK), F32),
            pltpu.VMEM((POOL_HIST + SUB_TILE, KW), F32),
            pltpu.VMEM((ts, d), F32),
            pltpu.VMEM((d, IN_WIDTH), BF16),
            pltpu.VMEM((len(POOL_WINDOWS), DK, DK), BF16),
            pltpu.VMEM((d, d), BF16),
        ],
        compiler_params=pltpu.CompilerParams(
            dimension_semantics=("arbitrary",), vmem_limit_bytes=VMEM_LIMIT),
        name="mix",
    )(x, x, g, w_in, lbl, hg, w_pool, ps, w_out, cm)


def _memkv_kernel(mem_ref, g_ref, wk_ref, wv_ref, kt_ref, v_ref):
    hm = _rms(mem_ref[0], g_ref[...]).astype(BF16)
    k = _dot(hm, wk_ref[...].astype(BF16)) * (XDIM ** -0.5)
    kt_ref[0] = k.T.astype(BF16)
    v_ref[0] = _dot(hm, wv_ref[...].astype(BF16)).astype(BF16)


def _memkv_call(mem, g, wk, wv):
    bsz, m, d = mem.shape
    return pl.pallas_call(
        _memkv_kernel,
        grid=(bsz,),
        in_specs=[
            pl.BlockSpec((1, m, d), lambda b: (b, 0, 0)),
            _const_spec((1, d)),
            _const_spec((d, d)),
            _const_spec((d, d)),
        ],
        out_specs=[
            pl.BlockSpec((1, d, m), lambda b: (b, 0, 0)),
            pl.BlockSpec((1, m, d), lambda b: (b, 0, 0)),
        ],
        out_shape=[
            jax.ShapeDtypeStruct((bsz, d, m), BF16),
            jax.ShapeDtypeStruct((bsz, m, d), BF16),
        ],
        compiler_params=pltpu.CompilerParams(
            dimension_semantics=("arbitrary",), vmem_limit_bytes=VMEM_LIMIT),
        name="memkv",
    )(mem, g, wk, wv)


def _tail_kernel(x_ref, gx_ref, wq_ref, kt_ref, v_ref, wo_ref, gf_ref, w1_ref, w2_ref, gl_ref, out_ref):
    hm = x_ref.shape[1] // 2
    head_cols = [slice(h * XDIM, (h + 1) * XDIM) for h in range(XHEADS)]
    rows = [slice(i * hm, (i + 1) * hm) for i in range(2)]
    xs = [x_ref[0, r, :] for r in rows]

    def q_proj(x):
        return _dot(_rms(x, gx_ref[...]).astype(BF16), wq_ref[...].astype(BF16)).astype(BF16)

    def scores(xq):
        return [_dot(xq[:, c], kt_ref[0, c, :]) for c in head_cols]

    def attend(ss):
        outs = []
        for s, c in zip(ss, head_cols):
            e = jnp.exp(s - jnp.max(s, axis=-1, keepdims=True))
            p = e * (1.0 / jnp.sum(e, axis=-1, keepdims=True))
            outs.append(_dot(p.astype(BF16), v_ref[0, :, c]).astype(BF16))
        return jnp.concatenate(outs, axis=1)

    def o_proj(x, att):
        return x + _dot(att, wo_ref[...].astype(BF16))

    def mlp(i, x):
        hf = _rms(x, gf_ref[...]).astype(BF16)
        acc = x
        for j in range(D_FF // FF_BLOCK):
            cols = slice(j * FF_BLOCK, (j + 1) * FF_BLOCK)
            u = jnp.maximum(_dot(hf, w1_ref[:, cols].astype(BF16)), 0.0)
            acc = acc + _dot((u * u).astype(BF16), w2_ref[cols, :].astype(BF16))
        out_ref[0, rows[i], :] = _rms(acc, gl_ref[...])

    s_a = scores(q_proj(xs[0]))
    xq_b = q_proj(xs[1])
    att_a = attend(s_a)
    s_b = scores(xq_b)
    x2_a = o_proj(xs[0], att_a)
    x2_b = o_proj(xs[1], attend(s_b))
    mlp(0, x2_a)
    mlp(1, x2_b)


def _tail_call(x, gx, wq, kt, v, wo, gf, w1, w2, gl):
    bsz, seq, d = x.shape
    tm = TM_TAIL
    return pl.pallas_call(
        _tail_kernel,
        grid=(bsz, seq // tm),
        in_specs=[
            pl.BlockSpec((1, tm, d), lambda b, s: (b, s, 0)),
            _const_spec((1, d)),
            _const_spec((d, d)),
            pl.BlockSpec((1, d, MEM_LEN), lambda b, s: (b, 0, 0)),
            pl.BlockSpec((1, MEM_LEN, d), lambda b, s: (b, 0, 0)),
            _const_spec((d, d)),
            _const_spec((1, d)),
            _const_spec((d, D_FF)),
            _const_spec((D_FF, d)),
            _const_spec((1, d)),
        ],
        out_specs=pl.BlockSpec((1, tm, d), lambda b, s: (b, s, 0)),
        out_shape=jax.ShapeDtypeStruct(x.shape, F32),
        compiler_params=pltpu.CompilerParams(
            dimension_semantics=("arbitrary", "arbitrary"), vmem_limit_bytes=TAIL_VMEM_LIMIT),
        name="tail",
    )(x, gx, wq, kt, v, wo, gf, w1, w2, gl)


def kernel(x, mem, norm_mix_g, w_in, lb_logits, hgrn_norm_g, w_pool, pool_scale, w_out,
           norm_x_g, norm_mem_g, w_xq, w_xk, w_xv, w_xo, norm_ffn_g, w_ff1, w_ff2, final_norm_g):
    bsz, seq, d = x.shape
    depth = norm_mix_g.shape[0]
    assert depth == 1 and lb_logits.shape[0] == 2
    row = lambda a: a.reshape(1, -1).astype(F32)
    x = _mix_call(x, row(norm_mix_g), w_in.reshape(d, IN_WIDTH), lb_logits.astype(F32),
                  row(hgrn_norm_g), w_pool.reshape(len(POOL_WINDOWS), DK, DK), row(pool_scale),
                  w_out.reshape(d, d))
    kt, v = _memkv_call(mem, row(norm_mem_g), w_xk.reshape(d, d), w_xv.reshape(d, d))
    return _tail_call(x, row(norm_x_g), w_xq.reshape(d, d), kt, v, w_xo.reshape(d, d),
                      row(norm_ffn_g), w_ff1.reshape(d, D_FF), w_ff2.reshape(D_FF, d), row(final_norm_g))
```

```python
import functools

import jax
import jax.numpy as jnp
import numpy as np
from jax import lax
from jax.experimental import pallas as pl
from jax.experimental.pallas import tpu as pltpu

D_MODEL = 1024
HEADS = 4
DK = 128
KW = HEADS * DK
PW = 2 * DK
CHUNK = 64
LEVELS = (32, 16, 8, 4, 2, 1)
POOL_WINDOWS = (2, 4, 8, 16)
POOL_HIST = 16
IN_WIDTH = 5 * KW
MEM_LEN = 256
XHEADS = 4
XDIM = 256
D_FF = 4096
EPS = 1e-6
LOG2E = 1.4426950408889634

TS_MIX = 512
SUB_TILE = 256
IN_BLOCK = 256
TM_TAIL = 512
FF_BLOCK = 1024
VMEM_LIMIT = 56 * 1024 * 1024
TAIL_VMEM_LIMIT = 60 * 1024 * 1024

F32 = jnp.float32
BF16 = jnp.bfloat16


def _rms(x, g):
    return x * lax.rsqrt(jnp.mean(x * x, axis=-1, keepdims=True) + EPS) * g


def _dot(a, b):
    return jnp.dot(a, b, preferred_element_type=F32)


def _dot_nt(a, b):
    return lax.dot_general(a, b, (((1,), (1,)), ((), ())), preferred_element_type=F32)


def _dot_tn(a, b):
    return lax.dot_general(a, b, (((0,), (0,)), ((), ())), preferred_element_type=F32)


def _sigmoid(x):
    return 1.0 / (1.0 + jnp.exp(-x))


def _block_diag(a, b):
    z = jnp.zeros_like(a)
    return jnp.concatenate([jnp.concatenate([a, z], axis=1), jnp.concatenate([z, b], axis=1)], axis=0)


def _cumsum_matrix():
    t = np.arange(CHUNK)[:, None]
    s = np.arange(CHUNK)[None, :]
    return (s <= t).astype(np.float32)


def _hgrn_tile(z_ref, ts, cm, lb_all, hg_all, st_ref, mixed_ref, filler):
    n_chunks = ts // CHUNK
    qp = z_ref[:, 0:KW]
    fp = z_ref[:, KW:2 * KW]
    gp = z_ref[:, 3 * KW:4 * KW]

    f = lb_all + (1.0 - lb_all) * _sigmoid(fp)
    kk_all = 1.0 - f
    q_all = qp * _sigmoid(qp)
    gate_all = gp * _sigmoid(gp)

    lf2 = jnp.log(f) * LOG2E
    hi = lf2.astype(BF16)
    lo = (lf2 - hi.astype(F32)).astype(BF16)
    b_chunks = []
    for ci in range(n_chunks):
        r = slice(ci * CHUNK, (ci + 1) * CHUNK)
        b_chunks.append(_dot(cm, hi[r]) + _dot(cm, lo[r]))
    filler()
    filler()

    row = lax.broadcasted_iota(jnp.int32, (CHUNK, PW), 0)
    upper = {L: jnp.bitwise_and(row, 2 * L - 1) >= L for L in LEVELS}
    out_row = lax.broadcasted_iota(jnp.int32, (CHUNK, DK), 0)
    out_col = jnp.bitwise_and(lax.broadcasted_iota(jnp.int32, (CHUNK, DK), 1), CHUNK - 1)
    same_block = {L: jnp.bitwise_and(jnp.bitwise_xor(out_row, out_col), -2 * L) == 0 for L in LEVELS[1:]}
    diagonal = out_row == out_col

    def ref_rows(a, first, period, width):
        parts = [jnp.broadcast_to(a[r:r + 1, :], (width, a.shape[1])) for r in range(first, CHUNK, period)]
        return parts[0] if len(parts) == 1 else jnp.concatenate(parts, axis=0)

    def level_ref(b, L):
        if L >= 4:
            return ref_rows(b, L - 1, 2 * L, 2 * L)
        return jnp.where(jnp.bitwise_and(row, 7) < 4, ref_rows(b, 1, 8, 8), ref_rows(b, 5, 8, 8))

    def pair_dot(qe, ke):
        rhs = _block_diag(ke[:, :DK].T, ke[:, DK:].T).astype(BF16)
        return _dot(qe.astype(BF16), rhs)

    work = []
    for ci in range(n_chunks):
        r = slice(ci * CHUNK, (ci + 1) * CHUNK)
        for hp in range(HEADS // 2):
            pc = slice(hp * PW, (hp + 1) * PW)
            q, kk, b = q_all[r, pc], kk_all[r, pc], b_chunks[ci][:, pc]
            b_last = b[CHUNK - 1:CHUNK, :]
            v_bf = z_ref[r, 2 * KW + hp * PW:2 * KW + (hp + 1) * PW].astype(BF16)

            qe_bf = (q * jnp.exp2(b)).astype(BF16)
            k_dec = (kk * jnp.exp2(b_last - b)).astype(BF16)
            upd = [_dot_tn(v_bf[:, i * DK:(i + 1) * DK], k_dec[:, i * DK:(i + 1) * DK]) for i in range(2)]

            level_scores = []
            for L in LEVELS:
                if L == 1:
                    x = jnp.where(upper[L], q * f[r, pc], kk)
                else:
                    x = jnp.where(upper[L], q, kk) * jnp.exp2(-jnp.abs(b - level_ref(b, L)))
                level_scores.append(pair_dot(jnp.where(upper[L], x, 0.0), jnp.where(upper[L], 0.0, x)))
            level_scores.append(pair_dot(q, kk))
            work.append((ci, hp, qe_bf, jnp.exp2(b_last), upd, level_scores, v_bf))
            if len(work) % (n_chunks // 4) == 0:
                filler()

    state = [st_ref[h] for h in range(HEADS)]
    o_inter = []
    for ci, hp, qe_bf, decay, upd, level_scores, v_bf in work:
        h0, h1 = 2 * hp, 2 * hp + 1
        o_inter.append(_dot(qe_bf, _block_diag(state[h0].T, state[h1].T).astype(BF16)))
        state[h0] = state[h0] * decay[:, :DK] + upd[0]
        state[h1] = state[h1] * decay[:, DK:] + upd[1]
    for h in range(HEADS):
        st_ref[h] = state[h]

    o_intra = []
    for ci, hp, qe_bf, decay, upd, level_scores, v_bf in work:
        scores = jnp.where(diagonal, level_scores[-1], 0.0)
        for L, s_l in zip(LEVELS[1:], level_scores[1:-1]):
            scores = scores + jnp.where(same_block[L], s_l, 0.0)
        scores = scores + level_scores[0]
        o_intra.append(_dot(scores.astype(BF16), _block_diag(v_bf[:, :DK], v_bf[:, DK:])))

    for (ci, hp, *_), oa, ob in zip(work, o_inter, o_intra):
        r = slice(ci * CHUNK, (ci + 1) * CHUNK)
        o_pair = ob + oa
        for i in range(2):
            cols = slice((2 * hp + i) * DK, (2 * hp + i + 1) * DK)
            o = o_pair[:, i * DK:(i + 1) * DK]
            o = o * lax.rsqrt(jnp.mean(o * o, axis=-1, keepdims=True) + EPS) * hg_all[:, cols]
            mixed_ref[r, cols] = o * gate_all[r, cols]


def _mix_kernel(xn_ref, xp_ref, g_ref, w_in_f32, lbl_ref, hg_ref, w_pool_f32, ps_ref, w_out_f32, cm_ref,
                out_ref, za_ref, zb_ref, st_ref, pext_ref, mixed_ref, w_in_ref, w_pool_ref, w_out_ref,
                *, tiles_per_seq):
    g = pl.program_id(0)
    si = lax.rem(g + tiles_per_seq - 1, tiles_per_seq)

    @pl.when(g == 0)
    def _():
        zb_ref[...] = jnp.zeros(zb_ref.shape, F32)
        w_in_ref[...] = w_in_f32[...].astype(BF16)
        w_pool_ref[...] = w_pool_f32[...].astype(BF16)
        w_out_ref[...] = w_out_f32[...].astype(BF16)

    @pl.when(jnp.logical_or(si == 0, g == 0))
    def _():
        st_ref[...] = jnp.zeros_like(st_ref)
        pext_ref[0:POOL_HIST, :] = jnp.zeros((POOL_HIST, KW), F32)

    args = (xn_ref, xp_ref, g_ref, w_in_ref, lbl_ref, hg_ref, w_pool_ref, ps_ref, w_out_ref, cm_ref,
            out_ref, st_ref, pext_ref, mixed_ref, si)

    @pl.when(lax.rem(g, 2) == 0)
    def _():
        _mix_step(*args, zn_ref=za_ref, zc_ref=zb_ref)

    @pl.when(lax.rem(g, 2) == 1)
    def _():
        _mix_step(*args, zn_ref=zb_ref, zc_ref=za_ref)


def _mix_step(xn_ref, xp_ref, g_ref, w_in_ref, lbl_ref, hg_ref, w_pool_ref, ps_ref, w_out_ref, cm_ref,
              out_ref, st_ref, pext_ref, mixed_ref, si, *, zn_ref, zc_ref):
    ts = xn_ref.shape[1]
    d = xn_ref.shape[2]
    n_sub = ts // SUB_TILE
    hn_bf = _rms(xn_ref[0], g_ref[...]).astype(BF16)

    def in_proj_unit(blk):
        def run():
            cols = slice(blk * IN_BLOCK, (blk + 1) * IN_BLOCK)
            zn_ref[:, cols] = _dot(hn_bf, w_in_ref[:, cols])
        return run

    def out_proj_unit(sub, blk):
        def run():
            r = slice(sub * SUB_TILE, (sub + 1) * SUB_TILE)
            cols = slice(blk * IN_BLOCK, (blk + 1) * IN_BLOCK)
            out_ref[0, r, cols] = xp_ref[0, r, cols] + _dot(mixed_ref[r, :].astype(BF16), w_out_ref[:, cols])
        return run

    in_units = [in_proj_unit(blk) for blk in range(IN_WIDTH // IN_BLOCK)]
    per_sub = -(-len(in_units) // n_sub)

    l0 = lbl_ref[0:1, :]
    l1 = lbl_ref[1:2, :]
    lmax = jnp.maximum(l0, l1)
    e0 = jnp.exp(l0 - lmax)
    lb_all = e0 / (e0 + jnp.exp(l1 - lmax))

    def pool(sub):
        r = slice(sub * SUB_TILE, (sub + 1) * SUB_TILE)
        p = zc_ref[r, 4 * KW:5 * KW]
        pext_ref[POOL_HIST:, :] = p
        pos = (si * ts + sub * SUB_TILE + 1 + lax.broadcasted_iota(jnp.int32, (SUB_TILE, 1), 0)).astype(F32)
        for gi, w in enumerate(POOL_WINDOWS):
            cols = slice(gi * DK, (gi + 1) * DK)
            acc = pext_ref[:, cols]
            sh = 1
            while sh < w:
                acc = acc + pltpu.roll(acc, sh, 0)
                sh *= 2
            win = acc[POOL_HIST:, :]
            pooled = win / jnp.minimum(pos, float(w)) - p[:, cols]
            y = _dot(pooled.astype(BF16), w_pool_ref[gi]) * ps_ref[:, cols]
            mixed_ref[r, KW + gi * DK:KW + (gi + 1) * DK] = y
        pext_ref[0:POOL_HIST, :] = pext_ref[SUB_TILE:SUB_TILE + POOL_HIST, :]

    pending = []
    for sub in range(n_sub):
        units = pending + in_units[sub * per_sub:(sub + 1) * per_sub]

        def filler():
            if units:
                units.pop(0)()

        r = pl.ds(sub * SUB_TILE, SUB_TILE)
        _hgrn_tile(zc_ref.at[r], SUB_TILE, cm_ref[...], lb_all, hg_ref[...], st_ref, mixed_ref.at[r], filler)
        while units:
            filler()
        pool(sub)
        pending = [out_proj_unit(sub, blk) for blk in range(d // IN_BLOCK)]
    for unit in pending:
        unit()


def _const_spec(shape):
    nd = len(shape)
    return pl.BlockSpec(shape, lambda *_: (0,) * nd, pipeline_mode=pl.Buffered(1))


def _mix_call(x, g, w_in, lbl, hg, w_pool, ps, w_out):
    bsz, seq, d = x.shape
    ts = TS_MIX
    ns = seq // ts
    n_tiles = bsz * ns
    cm = jnp.asarray(_cumsum_matrix(), BF16)

    def tile_next(i):
        t = jnp.minimum(i, n_tiles - 1)
        return (t // ns, t % ns, 0)

    def tile_prev(i):
        t = jnp.maximum(i - 1, 0)
        return (t // ns, t % ns, 0)

    return pl.pallas_call(
        functools.partial(_mix_kernel, tiles_per_seq=ns),
        grid=(n_tiles + 1,),
        in_specs=[
            pl.BlockSpec((1, ts, d), tile_next),
            pl.BlockSpec((1, ts, d), tile_prev),
            _const_spec((1, d)),
            _const_spec((d, IN_WIDTH)),
            _const_spec((2, KW)),
            _const_spec((1, KW)),
            _const_spec((len(POOL_WINDOWS), DK, DK)),
            _const_spec((1, KW)),
            _const_spec((d, d)),
            _const_spec((CHUNK, CHUNK)),
        ],
        out_specs=pl.BlockSpec((1, ts, d), tile_prev),
        out_shape=jax.ShapeDtypeStruct(x.shape, F32),
        scratch_shapes=[
            pltpu.VMEM((ts, IN_WIDTH), F32),
            pltpu.VMEM((ts, IN_WIDTH), F32),
            pltpu.VMEM((HEADS, DK, DK), F32),
            pltpu.VMEM((POOL_HIST + SUB_TILE, KW), F32),
            pltpu.VMEM((ts, d), F32),
            pltpu.VMEM((d, IN_WIDTH), BF16),
            pltpu.VMEM((len(POOL_WINDOWS), DK, DK), BF16),
            pltpu.VMEM((d, d), BF16),
        ],
        compiler_params=pltpu.CompilerParams(
            dimension_semantics=("arbitrary",), vmem_limit_bytes=VMEM_LIMIT),
        name="mix",
    )(x, x, g, w_in, lbl, hg, w_pool, ps, w_out, cm)


def _memkv_kernel(mem_ref, g_ref, wk_ref, wv_ref, kt_ref, v_ref):
    hm = _rms(mem_ref[0], g_ref[...]).astype(BF16)
    k = _dot(hm, wk_ref[...].astype(BF16)) * (XDIM ** -0.5)
    kt_ref[0] = k.T.astype(BF16)
    v_ref[0] = _dot(hm, wv_ref[...].astype(BF16)).astype(BF16)


def _memkv_call(mem, g, wk, wv):
    bsz, m, d = mem.shape
    return pl.pallas_call(
        _memkv_kernel,
        grid=(bsz,),
        in_specs=[
            pl.BlockSpec((1, m, d), lambda b: (b, 0, 0)),
            _const_spec((1, d)),
            _const_spec((d, d)),
            _const_spec((d, d)),
        ],
        out_specs=[
            pl.BlockSpec((1, d, m), lambda b: (b, 0, 0)),
            pl.BlockSpec((1, m, d), lambda b: (b, 0, 0)),
        ],
        out_shape=[
            jax.ShapeDtypeStruct((bsz, d, m), BF16),
            jax.ShapeDtypeStruct((bsz, m, d), BF16),
        ],
        compiler_params=pltpu.CompilerParams(
            dimension_semantics=("arbitrary",), vmem_limit_bytes=VMEM_LIMIT),
        name="memkv",
    )(mem, g, wk, wv)


def _tail_kernel(x_ref, gx_ref, wq_ref, kt_ref, v_ref, wo_ref, gf_ref, w1_ref, w2_ref, gl_ref, out_ref):
    hm = x_ref.shape[1] // 2
    head_cols = [slice(h * XDIM, (h + 1) * XDIM) for h in range(XHEADS)]
    rows = [slice(i * hm, (i + 1) * hm) for i in range(2)]
    xs = [x_ref[0, r, :] for r in rows]

    def q_proj(x):
        return _dot(_rms(x, gx_ref[...]).astype(BF16), wq_ref[...].astype(BF16)).astype(BF16)

    def scores(xq):
        return [_dot(xq[:, c], kt_ref[0, c, :]) for c in head_cols]

    def attend(ss):
        outs = []
        for s, c in zip(ss, head_cols):
            e = jnp.exp(s - jnp.max(s, axis=-1, keepdims=True))
            p = e * (1.0 / jnp.sum(e, axis=-1, keepdims=True))
            outs.append(_dot(p.astype(BF16), v_ref[0, :, c]).astype(BF16))
        return jnp.concatenate(outs, axis=1)

    def o_proj(x, att):
        return x + _dot(att, wo_ref[...].astype(BF16))

    def mlp(i, x):
        hf = _rms(x, gf_ref[...]).astype(BF16)
        acc = x
        for j in range(D_FF // FF_BLOCK):
            cols = slice(j * FF_BLOCK, (j + 1) * FF_BLOCK)
            u = jnp.maximum(_dot(hf, w1_ref[:, cols].astype(BF16)), 0.0)
            acc = acc + _dot((u * u).astype(BF16), w2_ref[cols, :].astype(BF16))
        out_ref[0, rows[i], :] = _rms(acc, gl_ref[...])

    s_a = scores(q_proj(xs[0]))
    xq_b = q_proj(xs[1])
    att_a = attend(s_a)
    s_b = scores(xq_b)
    x2_a = o_proj(xs[0], att_a)
    x2_b = o_proj(xs[1], attend(s_b))
    mlp(0, x2_a)
    mlp(1, x2_b)


def _tail_call(x, gx, wq, kt, v, wo, gf, w1, w2, gl):
    bsz, seq, d = x.shape
    tm = TM_TAIL
    return pl.pallas_call(
        _tail_kernel,
        grid=(bsz, seq // tm),
        in_specs=[
            pl.BlockSpec((1, tm, d), lambda b, s: (b, s, 0)),
            _const_spec((1, d)),
            _const_spec((d, d)),
            pl.BlockSpec((1, d, MEM_LEN), lambda b, s: (b, 0, 0)),
            pl.BlockSpec((1, MEM_LEN, d), lambda b, s: (b, 0, 0)),
            _const_spec((d, d)),
            _const_spec((1, d)),
            _const_spec((d, D_FF)),
            _const_spec((D_FF, d)),
            _const_spec((1, d)),
        ],
        out_specs=pl.BlockSpec((1, tm, d), lambda b, s: (b, s, 0)),
        out_shape=jax.ShapeDtypeStruct(x.shape, F32),
        compiler_params=pltpu.CompilerParams(
            dimension_semantics=("arbitrary", "arbitrary"), vmem_limit_bytes=TAIL_VMEM_LIMIT),
        name="tail",
    )(x, gx, wq, kt, v, wo, gf, w1, w2, gl)


def kernel(x, mem, norm_mix_g, w_in, lb_logits, hgrn_norm_g, w_pool, pool_scale, w_out,
           norm_x_g, norm_mem_g, w_xq, w_xk, w_xv, w_xo, norm_ffn_g, w_ff1, w_ff2, final_norm_g):
    bsz, seq, d = x.shape
    depth = norm_mix_g.shape[0]
    assert depth == 1 and lb_logits.shape[0] == 2
    row = lambda a: a.reshape(1, -1).astype(F32)
    x = _mix_call(x, row(norm_mix_g), w_in.reshape(d, IN_WIDTH), lb_logits.astype(F32),
                  row(hgrn_norm_g), w_pool.reshape(len(POOL_WINDOWS), DK, DK), row(pool_scale),
                  w_out.reshape(d, d))
    kt, v = _memkv_call(mem, row(norm_mem_g), w_xk.reshape(d, d), w_xv.reshape(d, d))
    return _tail_call(x, row(norm_x_g), w_xq.reshape(d, d), kt, v, w_xo.reshape(d, d),
                      row(norm_ffn_g), w_ff1.reshape(d, D_FF), w_ff2.reshape(D_FF, d), row(final_norm_g))
```

```python
import functools

import jax
import jax.numpy as jnp
import numpy as np
from jax import lax
from jax.experimental import pallas as pl
from jax.experimental.pallas import tpu as pltpu

D_MODEL = 1024
HEADS = 4
DK = 128
KW = HEADS * DK
PW = 2 * DK
CHUNK = 64
LEVELS = (32, 16, 8, 4, 2, 1)
POOL_WINDOWS = (2, 4, 8, 16)
POOL_HIST = 16
IN_WIDTH = 5 * KW
MEM_LEN = 256
XHEADS = 4
XDIM = 256
D_FF = 4096
EPS = 1e-6
LOG2E = 1.4426950408889634

TS_MIX = 512
SUB_TILE = 256
IN_BLOCK = 256
TM_TAIL = 512
FF_BLOCK = 1024
VMEM_LIMIT = 56 * 1024 * 1024
TAIL_VMEM_LIMIT = 60 * 1024 * 1024

F32 = jnp.float32
BF16 = jnp.bfloat16


def _rms(x, g):
    return x * lax.rsqrt(jnp.mean(x * x, axis=-1, keepdims=True) + EPS) * g


def _dot(a, b):
    return jnp.dot(a, b, preferred_element_type=F32)


def _dot_nt(a, b):
    return lax.dot_general(a, b, (((1,), (1,)), ((), ())), preferred_element_type=F32)


def _dot_tn(a, b):
    return lax.dot_general(a, b, (((0,), (0,)), ((), ())), preferred_element_type=F32)


def _sigmoid(x):
    return 1.0 / (1.0 + jnp.exp(-x))


def _block_diag(a, b):
    z = jnp.zeros_like(a)
    return jnp.concatenate([jnp.concatenate([a, z], axis=1), jnp.concatenate([z, b], axis=1)], axis=0)


def _cumsum_matrix():
    t = np.arange(CHUNK)[:, None]
    s = np.arange(CHUNK)[None, :]
    return (s <= t).astype(np.float32)


def _hgrn_tile(z_ref, ts, cm, lb_all, hg_all, st_ref, mixed_ref):
    n_chunks = ts // CHUNK
    qp = z_ref[:, 0:KW]
    fp = z_ref[:, KW:2 * KW]
    gp = z_ref[:, 3 * KW:4 * KW]

    f = lb_all + (1.0 - lb_all) * _sigmoid(fp)
    kk_all = 1.0 - f
    q_all = qp * _sigmoid(qp)
    gate_all = gp * _sigmoid(gp)

    lf2 = jnp.log(f) * LOG2E
    hi = lf2.astype(BF16)
    lo = (lf2 - hi.astype(F32)).astype(BF16)
    b_chunks = []
    for ci in range(n_chunks):
        r = slice(ci * CHUNK, (ci + 1) * CHUNK)
        b_chunks.append(_dot(cm, hi[r]) + _dot(cm, lo[r]))
    yield "prep"

    row = lax.broadcasted_iota(jnp.int32, (CHUNK, PW), 0)
    upper = {L: jnp.bitwise_and(row, 2 * L - 1) >= L for L in LEVELS}
    out_row = lax.broadcasted_iota(jnp.int32, (CHUNK, DK), 0)
    out_col = jnp.bitwise_and(lax.broadcasted_iota(jnp.int32, (CHUNK, DK), 1), CHUNK - 1)
    same_block = {L: jnp.bitwise_and(jnp.bitwise_xor(out_row, out_col), -2 * L) == 0 for L in LEVELS[1:]}
    diagonal = out_row == out_col

    def ref_rows(a, first, period, width):
        parts = [jnp.broadcast_to(a[r:r + 1, :], (width, a.shape[1])) for r in range(first, CHUNK, period)]
        return parts[0] if len(parts) == 1 else jnp.concatenate(parts, axis=0)

    def level_ref(b, L):
        if L >= 4:
            return ref_rows(b, L - 1, 2 * L, 2 * L)
        return jnp.where(jnp.bitwise_and(row, 7) < 4, ref_rows(b, 1, 8, 8), ref_rows(b, 5, 8, 8))

    def pair_dot(qe, ke):
        rhs = _block_diag(ke[:, :DK], ke[:, DK:]).T.astype(BF16)
        return _dot(qe.astype(BF16), rhs)

    work = []
    for ci in range(n_chunks):
        r = slice(ci * CHUNK, (ci + 1) * CHUNK)
        for hp in range(HEADS // 2):
            pc = slice(hp * PW, (hp + 1) * PW)
            q, kk, b = q_all[r, pc], kk_all[r, pc], b_chunks[ci][:, pc]
            b_last = b[CHUNK - 1:CHUNK, :]
            v_bf = z_ref[r, 2 * KW + hp * PW:2 * KW + (hp + 1) * PW].astype(BF16)

            qe_bf = (q * jnp.exp2(b)).astype(BF16)
            k_dec = (kk * jnp.exp2(b_last - b)).astype(BF16)
            upd = [_dot_tn(v_bf[:, i * DK:(i + 1) * DK], k_dec[:, i * DK:(i + 1) * DK]) for i in range(2)]

            level_scores = []
            for L in LEVELS:
                if L == 1:
                    x = jnp.where(upper[L], q * f[r, pc], kk)
                else:
                    x = jnp.where(upper[L], q, kk) * jnp.exp2(-jnp.abs(b - level_ref(b, L)))
                level_scores.append(pair_dot(jnp.where(upper[L], x, 0.0), jnp.where(upper[L], 0.0, x)))
            level_scores.append(pair_dot(q, kk))
            work.append((ci, hp, qe_bf, jnp.exp2(b_last), upd, level_scores, v_bf))
            yield "levels"

    state = [st_ref[h] for h in range(HEADS)]
    o_inter = []
    for ci, hp, qe_bf, decay, upd, level_scores, v_bf in work:
        h0, h1 = 2 * hp, 2 * hp + 1
        o_inter.append(_dot_nt(qe_bf, _block_diag(state[h0].astype(BF16), state[h1].astype(BF16))))
        state[h0] = state[h0] * decay[:, :DK] + upd[0]
        state[h1] = state[h1] * decay[:, DK:] + upd[1]
    for h in range(HEADS):
        st_ref[h] = state[h]
    yield "state"

    o_intra = []
    for ci, hp, qe_bf, decay, upd, level_scores, v_bf in work:
        scores = jnp.where(diagonal, level_scores[-1], 0.0)
        for L, s_l in zip(LEVELS[1:], level_scores[1:-1]):
            scores = scores + jnp.where(same_block[L], s_l, 0.0)
        scores = scores + level_scores[0]
        o_intra.append(_dot(scores.astype(BF16), _block_diag(v_bf[:, :DK], v_bf[:, DK:])))
        yield "values"

    for (ci, hp, *_), oa, ob in zip(work, o_inter, o_intra):
        r = slice(ci * CHUNK, (ci + 1) * CHUNK)
        o_pair = ob + oa
        for i in range(2):
            cols = slice((2 * hp + i) * DK, (2 * hp + i + 1) * DK)
            o = o_pair[:, i * DK:(i + 1) * DK]
            o = o * lax.rsqrt(jnp.mean(o * o, axis=-1, keepdims=True) + EPS) * hg_all[:, cols]
            mixed_ref[r, cols] = o * gate_all[r, cols]
        yield "norm"


def _mix_kernel(xn_ref, xp_ref, g_ref, w_in_f32, lbl_ref, hg_ref, w_pool_f32, ps_ref, w_out_f32, cm_ref,
                out_ref, za_ref, zb_ref, st_ref, pext_ref, mixed_ref, w_in_ref, w_pool_ref, w_out_ref,
                *, tiles_per_seq):
    g = pl.program_id(0)
    si = lax.rem(g + tiles_per_seq - 1, tiles_per_seq)

    @pl.when(g == 0)
    def _():
        zb_ref[...] = jnp.zeros(zb_ref.shape, F32)
        w_in_ref[...] = w_in_f32[...].astype(BF16)
        w_pool_ref[...] = w_pool_f32[...].astype(BF16)
        w_out_ref[...] = w_out_f32[...].astype(BF16)

    @pl.when(jnp.logical_or(si == 0, g == 0))
    def _():
        st_ref[...] = jnp.zeros_like(st_ref)
        pext_ref[0:POOL_HIST, :] = jnp.zeros((POOL_HIST, KW), F32)

    args = (xn_ref, xp_ref, g_ref, w_in_ref, lbl_ref, hg_ref, w_pool_ref, ps_ref, w_out_ref, cm_ref,
            out_ref, st_ref, pext_ref, mixed_ref, si)

    @pl.when(lax.rem(g, 2) == 0)
    def _():
        _mix_step(*args, zn_ref=za_ref, zc_ref=zb_ref)

    @pl.when(lax.rem(g, 2) == 1)
    def _():
        _mix_step(*args, zn_ref=zb_ref, zc_ref=za_ref)


def _mix_step(xn_ref, xp_ref, g_ref, w_in_ref, lbl_ref, hg_ref, w_pool_ref, ps_ref, w_out_ref, cm_ref,
              out_ref, st_ref, pext_ref, mixed_ref, si, *, zn_ref, zc_ref):
    ts = xn_ref.shape[1]
    d = xn_ref.shape[2]
    n_sub = ts // SUB_TILE
    hn_bf = _rms(xn_ref[0], g_ref[...]).astype(BF16)

    def in_proj_unit(blk):
        def run():
            cols = slice(blk * IN_BLOCK, (blk + 1) * IN_BLOCK)
            zn_ref[:, cols] = _dot(hn_bf, w_in_ref[:, cols])
        return run

    def out_proj_unit(sub, blk):
        def run():
            r = slice(sub * SUB_TILE, (sub + 1) * SUB_TILE)
            cols = slice(blk * IN_BLOCK, (blk + 1) * IN_BLOCK)
            out_ref[0, r, cols] = xp_ref[0, r, cols] + _dot(mixed_ref[r, :].astype(BF16), w_out_ref[:, cols])
        return run

    in_units = [in_proj_unit(blk) for blk in range(IN_WIDTH // IN_BLOCK)]

    l0 = lbl_ref[0:1, :]
    l1 = lbl_ref[1:2, :]
    lmax = jnp.maximum(l0, l1)
    e0 = jnp.exp(l0 - lmax)
    lb_all = e0 / (e0 + jnp.exp(l1 - lmax))

    def pool(sub):
        r = slice(sub * SUB_TILE, (sub + 1) * SUB_TILE)
        p = zc_ref[r, 4 * KW:5 * KW]
        pext_ref[POOL_HIST:, :] = p
        pos = (si * ts + sub * SUB_TILE + 1 + lax.broadcasted_iota(jnp.int32, (SUB_TILE, 1), 0)).astype(F32)
        for gi, w in enumerate(POOL_WINDOWS):
            cols = slice(gi * DK, (gi + 1) * DK)
            acc = pext_ref[:, cols]
            sh = 1
            while sh < w:
                acc = acc + pltpu.roll(acc, sh, 0)
                sh *= 2
            win = acc[POOL_HIST:, :]
            pooled = win / jnp.minimum(pos, float(w)) - p[:, cols]
            y = _dot(pooled.astype(BF16), w_pool_ref[gi]) * ps_ref[:, cols]
            mixed_ref[r, KW + gi * DK:KW + (gi + 1) * DK] = y
        pext_ref[0:POOL_HIST, :] = pext_ref[SUB_TILE:SUB_TILE + POOL_HIST, :]

    def hgrn(sub):
        r = pl.ds(sub * SUB_TILE, SUB_TILE)
        return _hgrn_tile(zc_ref.at[r], SUB_TILE, cm_ref[...], lb_all, hg_ref[...], st_ref, mixed_ref.at[r])

    def step(gen, label):
        got = next(gen)
        assert got == label, (got, label)

    def fill(n=1):
        for _ in range(n):
            if in_units:
                in_units.pop(0)()

    n_items = (SUB_TILE // CHUNK) * (HEADS // 2)
    gens = [hgrn(sub) for sub in range(n_sub)]
    out_units = []
    step(gens[0], "prep")
    fill(2)
    for i in range(n_items):
        step(gens[0], "levels")
        if i % 2 == 1:
            fill()
    for sub in range(n_sub):
        cur = gens[sub]
        nxt = gens[sub + 1] if sub + 1 < n_sub else None
        if nxt is not None:
            step(nxt, "prep")
        step(cur, "state")
        for i in range(n_items):
            if nxt is not None:
                step(nxt, "levels")
            step(cur, "values")
            if i % 2 == 1:
                (out_units.pop(0) if out_units else fill)()
        for i in range(n_items):
            step(cur, "norm")
        pool(sub)
        out_units = [out_proj_unit(sub, blk) for blk in range(d // IN_BLOCK)]
    fill(len(in_units))
    for unit in out_units:
        unit()


def _const_spec(shape):
    nd = len(shape)
    return pl.BlockSpec(shape, lambda *_: (0,) * nd, pipeline_mode=pl.Buffered(1))


def _mix_call(x, g, w_in, lbl, hg, w_pool, ps, w_out):
    bsz, seq, d = x.shape
    ts = TS_MIX
    ns = seq // ts
    n_tiles = bsz * ns
    cm = jnp.asarray(_cumsum_matrix(), BF16)

    def tile_next(i):
        t = jnp.minimum(i, n_tiles - 1)
        return (t // ns, t % ns, 0)

    def tile_prev(i):
        t = jnp.maximum(i - 1, 0)
        return (t // ns, t % ns, 0)

    return pl.pallas_call(
        functools.partial(_mix_kernel, tiles_per_seq=ns),
        grid=(n_tiles + 1,),
        in_specs=[
            pl.BlockSpec((1, ts, d), tile_next),
            pl.BlockSpec((1, ts, d), tile_prev),
            _const_spec((1, d)),
            _const_spec((d, IN_WIDTH)),
            _const_spec((2, KW)),
            _const_spec((1, KW)),
            _const_spec((len(POOL_WINDOWS), DK, DK)),
            _const_spec((1, KW)),
            _const_spec((d, d)),
            _const_spec((CHUNK, CHUNK)),
        ],
        out_specs=pl.BlockSpec((1, ts, d), tile_prev),
        out_shape=jax.ShapeDtypeStruct(x.shape, F32),
        scratch_shapes=[
            pltpu.VMEM((ts, IN_WIDTH), F32),
            pltpu.VMEM((ts, IN_WIDTH), F32),
            pltpu.VMEM((HEADS, DK, DK), F32),
            pltpu.VMEM((POOL_HIST + SUB_TILE, KW), F32),
            pltpu.VMEM((ts, d), F32),
            pltpu.VMEM((d, IN_WIDTH), BF16),
            pltpu.VMEM((len(POOL_WINDOWS), DK, DK), BF16),
            pltpu.VMEM((d, d), BF16),
        ],
        compiler_params=pltpu.CompilerParams(
            dimension_semantics=("arbitrary",), vmem_limit_bytes=VMEM_LIMIT),
        name="mix",
    )(x, x, g, w_in, lbl, hg, w_pool, ps, w_out, cm)


def _memkv_kernel(mem_ref, g_ref, wk_ref, wv_ref, kt_ref, v_ref):
    hm = _rms(mem_ref[0], g_ref[...]).astype(BF16)
    k = _dot(hm, wk_ref[...].astype(BF16)) * (XDIM ** -0.5)
    kt_ref[0] = k.T.astype(BF16)
    v_ref[0] = _dot(hm, wv_ref[...].astype(BF16)).astype(BF16)


def _memkv_call(mem, g, wk, wv):
    bsz, m, d = mem.shape
    return pl.pallas_call(
        _memkv_kernel,
        grid=(bsz,),
        in_specs=[
            pl.BlockSpec((1, m, d), lambda b: (b, 0, 0)),
            _const_spec((1, d)),
            _const_spec((d, d)),
            _const_spec((d, d)),
        ],
        out_specs=[
            pl.BlockSpec((1, d, m), lambda b: (b, 0, 0)),
            pl.BlockSpec((1, m, d), lambda b: (b, 0, 0)),
        ],
        out_shape=[
            jax.ShapeDtypeStruct((bsz, d, m), BF16),
            jax.ShapeDtypeStruct((bsz, m, d), BF16),
        ],
        compiler_params=pltpu.CompilerParams(
            dimension_semantics=("arbitrary",), vmem_limit_bytes=VMEM_LIMIT),
        name="memkv",
    )(mem, g, wk, wv)


def _tail_kernel(x_ref, gx_ref, wq_ref, kt_ref, v_ref, wo_ref, gf_ref, w1_ref, w2_ref, gl_ref, out_ref):
    hm = x_ref.shape[1] // 2
    head_cols = [slice(h * XDIM, (h + 1) * XDIM) for h in range(XHEADS)]
    rows = [slice(i * hm, (i + 1) * hm) for i in range(2)]
    xs = [x_ref[0, r, :] for r in rows]

    def q_proj(x):
        return _dot(_rms(x, gx_ref[...]).astype(BF16), wq_ref[...].astype(BF16)).astype(BF16)

    def scores(xq):
        return [_dot(xq[:, c], kt_ref[0, c, :]) for c in head_cols]

    def attend(ss):
        outs = []
        for s, c in zip(ss, head_cols):
            e = jnp.exp(s - jnp.max(s, axis=-1, keepdims=True))
            p = e * (1.0 / jnp.sum(e, axis=-1, keepdims=True))
            outs.append(_dot(p.astype(BF16), v_ref[0, :, c]).astype(BF16))
        return jnp.concatenate(outs, axis=1)

    def o_proj(x, att):
        return x + _dot(att, wo_ref[...].astype(BF16))

    def mlp(i, x):
        hf = _rms(x, gf_ref[...]).astype(BF16)
        acc = x
        for j in range(D_FF // FF_BLOCK):
            cols = slice(j * FF_BLOCK, (j + 1) * FF_BLOCK)
            u = jnp.maximum(_dot(hf, w1_ref[:, cols].astype(BF16)), 0.0)
            acc = acc + _dot((u * u).astype(BF16), w2_ref[cols, :].astype(BF16))
        out_ref[0, rows[i], :] = _rms(acc, gl_ref[...])

    s_a = scores(q_proj(xs[0]))
    xq_b = q_proj(xs[1])
    att_a = attend(s_a)
    s_b = scores(xq_b)
    x2_a = o_proj(xs[0], att_a)
    x2_b = o_proj(xs[1], attend(s_b))
    mlp(0, x2_a)
    mlp(1, x2_b)


def _tail_call(x, gx, wq, kt, v, wo, gf, w1, w2, gl):
    bsz, seq, d = x.shape
    tm = TM_TAIL
    return pl.pallas_call(
        _tail_kernel,
        grid=(bsz, seq // tm),
        in_specs=[
            pl.BlockSpec((1, tm, d), lambda b, s: (b, s, 0)),
            _const_spec((1, d)),
            _const_spec((d, d)),
            pl.BlockSpec((1, d, MEM_LEN), lambda b, s: (b, 0, 0)),
            pl.BlockSpec((1, MEM_LEN, d), lambda b, s: (b, 0, 0)),
            _const_spec((d, d)),
            _const_spec((1, d)),
            _const_spec((d, D_FF)),
            _const_spec((D_FF, d)),
            _const_spec((1, d)),
        ],
        out_specs=pl.BlockSpec((1, tm, d), lambda b, s: (b, s, 0)),
        out_shape=jax.ShapeDtypeStruct(x.shape, F32),
        compiler_params=pltpu.CompilerParams(
            dimension_semantics=("arbitrary", "arbitrary"), vmem_limit_bytes=TAIL_VMEM_LIMIT),
        name="tail",
    )(x, gx, wq, kt, v, wo, gf, w1, w2, gl)


def kernel(x, mem, norm_mix_g, w_in, lb_logits, hgrn_norm_g, w_pool, pool_scale, w_out,
           norm_x_g, norm_mem_g, w_xq, w_xk, w_xv, w_xo, norm_ffn_g, w_ff1, w_ff2, final_norm_g):
    bsz, seq, d = x.shape
    depth = norm_mix_g.shape[0]
    assert depth == 1 and lb_logits.shape[0] == 2
    row = lambda a: a.reshape(1, -1).astype(F32)
    x = _mix_call(x, row(norm_mix_g), w_in.reshape(d, IN_WIDTH), lb_logits.astype(F32),
                  row(hgrn_norm_g), w_pool.reshape(len(POOL_WINDOWS), DK, DK), row(pool_scale),
                  w_out.reshape(d, d))
    kt, v = _memkv_call(mem, row(norm_mem_g), w_xk.reshape(d, d), w_xv.reshape(d, d))
    return _tail_call(x, row(norm_x_g), w_xq.reshape(d, d), kt, v, w_xo.reshape(d, d),
                      row(norm_ffn_g), w_ff1.reshape(d, D_FF), w_ff2.reshape(D_FF, d), row(final_norm_g))
```

```python
import functools

import jax
import jax.numpy as jnp
import numpy as np
from jax import lax
from jax.experimental import pallas as pl
from jax.experimental.pallas import tpu as pltpu

D_MODEL = 1024
HEADS = 4
DK = 128
KW = HEADS * DK
PW = 2 * DK
CHUNK = 64
LEVELS = (32, 16, 8, 4, 2, 1)
POOL_WINDOWS = (2, 4, 8, 16)
POOL_HIST = 16
IN_WIDTH = 5 * KW
MEM_LEN = 256
XHEADS = 4
XDIM = 256
D_FF = 4096
EPS = 1e-6
LOG2E = 1.4426950408889634

TS_MIX = 512
SUB_TILE = 256
IN_BLOCK = 256
TM_TAIL = 512
FF_BLOCK = 1024
VMEM_LIMIT = 56 * 1024 * 1024
TAIL_VMEM_LIMIT = 60 * 1024 * 1024

F32 = jnp.float32
BF16 = jnp.bfloat16


def _rms(x, g):
    return x * lax.rsqrt(jnp.mean(x * x, axis=-1, keepdims=True) + EPS) * g


def _dot(a, b):
    return jnp.dot(a, b, preferred_element_type=F32)


def _dot_nt(a, b):
    return lax.dot_general(a, b, (((1,), (1,)), ((), ())), preferred_element_type=F32)


def _dot_tn(a, b):
    return lax.dot_general(a, b, (((0,), (0,)), ((), ())), preferred_element_type=F32)


def _sigmoid(x):
    return 1.0 / (1.0 + jnp.exp(-x))


def _block_diag(a, b):
    z = jnp.zeros_like(a)
    return jnp.concatenate([jnp.concatenate([a, z], axis=1), jnp.concatenate([z, b], axis=1)], axis=0)


def _cumsum_matrix():
    t = np.arange(CHUNK)[:, None]
    s = np.arange(CHUNK)[None, :]
    return (s <= t).astype(np.float32)


def _hgrn_tile(z_ref, ts, cm, lb_all, hg_all, st_ref, mixed_ref):
    n_chunks = ts // CHUNK
    qp = z_ref[:, 0:KW]
    fp = z_ref[:, KW:2 * KW]
    gp = z_ref[:, 3 * KW:4 * KW]

    f = lb_all + (1.0 - lb_all) * _sigmoid(fp)
    kk_all = 1.0 - f
    q_all = qp * _sigmoid(qp)
    gate_all = gp * _sigmoid(gp)

    lf2 = jnp.log(f) * LOG2E
    hi = lf2.astype(BF16)
    lo = (lf2 - hi.astype(F32)).astype(BF16)
    b_chunks = []
    for ci in range(n_chunks):
        r = slice(ci * CHUNK, (ci + 1) * CHUNK)
        b_chunks.append(_dot(cm, hi[r]) + _dot(cm, lo[r]))
    yield "prep"

    row = lax.broadcasted_iota(jnp.int32, (CHUNK, PW), 0)
    upper = {L: jnp.bitwise_and(row, 2 * L - 1) >= L for L in LEVELS}
    out_row = lax.broadcasted_iota(jnp.int32, (CHUNK, DK), 0)
    out_col = jnp.bitwise_and(lax.broadcasted_iota(jnp.int32, (CHUNK, DK), 1), CHUNK - 1)
    same_block = {L: jnp.bitwise_and(jnp.bitwise_xor(out_row, out_col), -2 * L) == 0 for L in LEVELS[1:]}
    diagonal = out_row == out_col

    def ref_rows(a, first, period, width):
        parts = [jnp.broadcast_to(a[r:r + 1, :], (width, a.shape[1])) for r in range(first, CHUNK, period)]
        return parts[0] if len(parts) == 1 else jnp.concatenate(parts, axis=0)

    def level_ref(b, L):
        if L >= 4:
            return ref_rows(b, L - 1, 2 * L, 2 * L)
        return jnp.where(jnp.bitwise_and(row, 7) < 4, ref_rows(b, 1, 8, 8), ref_rows(b, 5, 8, 8))

    def pair_dot(qe, ke):
        rhs = _block_diag(ke[:, :DK], ke[:, DK:]).T.astype(BF16)
        return _dot(qe.astype(BF16), rhs)

    work = []
    for ci in range(n_chunks):
        r = slice(ci * CHUNK, (ci + 1) * CHUNK)
        for hp in range(HEADS // 2):
            pc = slice(hp * PW, (hp + 1) * PW)
            q, kk, b = q_all[r, pc], kk_all[r, pc], b_chunks[ci][:, pc]
            b_last = b[CHUNK - 1:CHUNK, :]
            v_bf = z_ref[r, 2 * KW + hp * PW:2 * KW + (hp + 1) * PW].astype(BF16)

            qe_bf = (q * jnp.exp2(b)).astype(BF16)
            k_dec = (kk * jnp.exp2(b_last - b)).astype(BF16)
            upd = [_dot_tn(v_bf[:, i * DK:(i + 1) * DK], k_dec[:, i * DK:(i + 1) * DK]) for i in range(2)]

            level_scores = []
            for L in LEVELS:
                if L == 1:
                    x = jnp.where(upper[L], q * f[r, pc], kk)
                else:
                    x = jnp.where(upper[L], q, kk) * jnp.exp2(-jnp.abs(b - level_ref(b, L)))
                level_scores.append(pair_dot(jnp.where(upper[L], x, 0.0), jnp.where(upper[L], 0.0, x)))
            level_scores.append(pair_dot(q, kk))
            work.append((ci, hp, qe_bf, jnp.exp2(b_last), upd, level_scores, v_bf))
            yield "levels"

    state = [st_ref[h] for h in range(HEADS)]
    o_inter = []
    for ci, hp, qe_bf, decay, upd, level_scores, v_bf in work:
        h0, h1 = 2 * hp, 2 * hp + 1
        o_inter.append(_dot_nt(qe_bf, _block_diag(state[h0].astype(BF16), state[h1].astype(BF16))))
        state[h0] = state[h0] * decay[:, :DK] + upd[0]
        state[h1] = state[h1] * decay[:, DK:] + upd[1]
    for h in range(HEADS):
        st_ref[h] = state[h]
    yield "state"

    o_intra = []
    for ci, hp, qe_bf, decay, upd, level_scores, v_bf in work:
        scores = jnp.where(diagonal, level_scores[-1], 0.0)
        for L, s_l in zip(LEVELS[1:], level_scores[1:-1]):
            scores = scores + jnp.where(same_block[L], s_l, 0.0)
        scores = scores + level_scores[0]
        o_intra.append(_dot(scores.astype(BF16), _block_diag(v_bf[:, :DK], v_bf[:, DK:])))
        yield "values"

    for (ci, hp, *_), oa, ob in zip(work, o_inter, o_intra):
        r = slice(ci * CHUNK, (ci + 1) * CHUNK)
        o_pair = ob + oa
        for i in range(2):
            cols = slice((2 * hp + i) * DK, (2 * hp + i + 1) * DK)
            o = o_pair[:, i * DK:(i + 1) * DK]
            o = o * lax.rsqrt(jnp.mean(o * o, axis=-1, keepdims=True) + EPS) * hg_all[:, cols]
            mixed_ref[r, cols] = o * gate_all[r, cols]
        yield "norm"


def _mix_kernel(xn_ref, xp_ref, g_ref, w_in_f32, lbl_ref, hg_ref, w_pool_f32, ps_ref, w_out_f32, cm_ref,
                out_ref, za_ref, zb_ref, st_ref, pext_ref, mixed_ref, w_in_ref, w_pool_ref, w_out_ref,
                *, tiles_per_seq):
    g = pl.program_id(0)
    si = lax.rem(g + tiles_per_seq - 1, tiles_per_seq)

    @pl.when(g == 0)
    def _():
        zb_ref[...] = jnp.zeros(zb_ref.shape, F32)
        w_in_ref[...] = w_in_f32[...].astype(BF16)
        w_pool_ref[...] = w_pool_f32[...].astype(BF16)
        w_out_ref[...] = w_out_f32[...].astype(BF16)

    @pl.when(jnp.logical_or(si == 0, g == 0))
    def _():
        st_ref[...] = jnp.zeros_like(st_ref)
        pext_ref[0:POOL_HIST, :] = jnp.zeros((POOL_HIST, KW), F32)

    args = (xn_ref, xp_ref, g_ref, w_in_ref, lbl_ref, hg_ref, w_pool_ref, ps_ref, w_out_ref, cm_ref,
            out_ref, st_ref, pext_ref, mixed_ref, si)

    @pl.when(lax.rem(g, 2) == 0)
    def _():
        _mix_step(*args, zn_ref=za_ref, zc_ref=zb_ref)

    @pl.when(lax.rem(g, 2) == 1)
    def _():
        _mix_step(*args, zn_ref=zb_ref, zc_ref=za_ref)


def _mix_step(xn_ref, xp_ref, g_ref, w_in_ref, lbl_ref, hg_ref, w_pool_ref, ps_ref, w_out_ref, cm_ref,
              out_ref, st_ref, pext_ref, mixed_ref, si, *, zn_ref, zc_ref):
    ts = xn_ref.shape[1]
    d = xn_ref.shape[2]
    n_sub = ts // SUB_TILE
    hn_bf = _rms(xn_ref[0], g_ref[...]).astype(BF16)

    def in_proj_unit(blk):
        def run():
            cols = slice(blk * IN_BLOCK, (blk + 1) * IN_BLOCK)
            zn_ref[:, cols] = _dot(hn_bf, w_in_ref[:, cols])
        return run

    def out_proj_unit(sub, blk):
        def run():
            r = slice(sub * SUB_TILE, (sub + 1) * SUB_TILE)
            cols = slice(blk * IN_BLOCK, (blk + 1) * IN_BLOCK)
            out_ref[0, r, cols] = xp_ref[0, r, cols] + _dot(mixed_ref[r, :].astype(BF16), w_out_ref[:, cols])
        return run

    in_units = [in_proj_unit(blk) for blk in range(IN_WIDTH // IN_BLOCK)]

    l0 = lbl_ref[0:1, :]
    l1 = lbl_ref[1:2, :]
    lmax = jnp.maximum(l0, l1)
    e0 = jnp.exp(l0 - lmax)
    lb_all = e0 / (e0 + jnp.exp(l1 - lmax))

    def pool(sub):
        r = slice(sub * SUB_TILE, (sub + 1) * SUB_TILE)
        p = zc_ref[r, 4 * KW:5 * KW]
        pext_ref[POOL_HIST:, :] = p
        pos = (si * ts + sub * SUB_TILE + 1 + lax.broadcasted_iota(jnp.int32, (SUB_TILE, 1), 0)).astype(F32)
        for gi, w in enumerate(POOL_WINDOWS):
            cols = slice(gi * DK, (gi + 1) * DK)
            acc = pext_ref[:, cols]
            sh = 1
            while sh < w:
                acc = acc + pltpu.roll(acc, sh, 0)
                sh *= 2
            win = acc[POOL_HIST:, :]
            pooled = win / jnp.minimum(pos, float(w)) - p[:, cols]
            y = _dot(pooled.astype(BF16), w_pool_ref[gi]) * ps_ref[:, cols]
            mixed_ref[r, KW + gi * DK:KW + (gi + 1) * DK] = y
        pext_ref[0:POOL_HIST, :] = pext_ref[SUB_TILE:SUB_TILE + POOL_HIST, :]

    def hgrn(sub):
        r = pl.ds(sub * SUB_TILE, SUB_TILE)
        return _hgrn_tile(zc_ref.at[r], SUB_TILE, cm_ref[...], lb_all, hg_ref[...], st_ref, mixed_ref.at[r])

    def step(gen, label):
        got = next(gen)
        assert got == label, (got, label)

    def fill(n=1):
        for _ in range(n):
            if in_units:
                in_units.pop(0)()

    n_items = (SUB_TILE // CHUNK) * (HEADS // 2)
    gens = [hgrn(sub) for sub in range(n_sub)]
    out_units = []
    step(gens[0], "prep")
    fill(2)
    for i in range(n_items):
        step(gens[0], "levels")
        if i % 2 == 1:
            fill()
    for sub in range(n_sub):
        cur = gens[sub]
        nxt = gens[sub + 1] if sub + 1 < n_sub else None
        if nxt is not None:
            step(nxt, "prep")
        step(cur, "state")
        for i in range(n_items):
            if nxt is not None:
                step(nxt, "levels")
            step(cur, "values")
            if i % 2 == 1:
                (out_units.pop(0) if out_units else fill)()
        for i in range(n_items):
            step(cur, "norm")
        pool(sub)
        out_units = [out_proj_unit(sub, blk) for blk in range(d // IN_BLOCK)]
    fill(len(in_units))
    for unit in out_units:
        unit()


def _const_spec(shape):
    nd = len(shape)
    return pl.BlockSpec(shape, lambda *_: (0,) * nd, pipeline_mode=pl.Buffered(1))


def _mix_call(x, g, w_in, lbl, hg, w_pool, ps, w_out):
    bsz, seq, d = x.shape
    ts = TS_MIX
    ns = seq // ts
    n_tiles = bsz * ns
    cm = jnp.asarray(_cumsum_matrix(), BF16)

    def tile_next(i):
        t = jnp.minimum(i, n_tiles - 1)
        return (t // ns, t % ns, 0)

    def tile_prev(i):
        t = jnp.maximum(i - 1, 0)
        return (t // ns, t % ns, 0)

    return pl.pallas_call(
        functools.partial(_mix_kernel, tiles_per_seq=ns),
        grid=(n_tiles + 1,),
        in_specs=[
            pl.BlockSpec((1, ts, d), tile_next),
            pl.BlockSpec((1, ts, d), tile_prev),
            _const_spec((1, d)),
            _const_spec((d, IN_WIDTH)),
            _const_spec((2, KW)),
            _const_spec((1, KW)),
            _const_spec((len(POOL_WINDOWS), DK, DK)),
            _const_spec((1, KW)),
            _const_spec((d, d)),
            _const_spec((CHUNK, CHUNK)),
        ],
        out_specs=pl.BlockSpec((1, ts, d), tile_prev),
        out_shape=jax.ShapeDtypeStruct(x.shape, F32),
        scratch_shapes=[
            pltpu.VMEM((ts, IN_WIDTH), F32),
            pltpu.VMEM((ts, IN_WIDTH), F32),
            pltpu.VMEM((HEADS, DK, DK), F32),
            pltpu.VMEM((POOL_HIST + SUB_TILE, KW), F32),
            pltpu.VMEM((ts, d), F32),
            pltpu.VMEM((d, IN_WIDTH), BF16),
            pltpu.VMEM((len(POOL_WINDOWS), DK, DK), BF16),
            pltpu.VMEM((d, d), BF16),
        ],
        compiler_params=pltpu.CompilerParams(
            dimension_semantics=("arbitrary",), vmem_limit_bytes=VMEM_LIMIT),
        name="mix",
    )(x, x, g, w_in, lbl, hg, w_pool, ps, w_out, cm)


def _memkv_kernel(mem_ref, g_ref, wq_ref, wk_ref, wv_ref, kt_ref, v_ref, wq_out_ref, wk_bf, wv_bf):
    def dense(w_ref):
        return jnp.concatenate([w_ref[:, h, :] for h in range(XHEADS)], axis=1).astype(BF16)

    @pl.when(pl.program_id(0) == 0)
    def _():
        wq_out_ref[...] = dense(wq_ref)
        wk_bf[...] = dense(wk_ref)
        wv_bf[...] = dense(wv_ref)

    hm = _rms(mem_ref[0], g_ref[...]).astype(BF16)
    k = _dot(hm, wk_bf[...]) * (XDIM ** -0.5)
    kt_ref[0] = k.T.astype(BF16)
    v_ref[0] = _dot(hm, wv_bf[...]).astype(BF16)


def _memkv_call(mem, g, wq, wk, wv):
    bsz, m, d = mem.shape
    return pl.pallas_call(
        _memkv_kernel,
        grid=(bsz,),
        in_specs=[
            pl.BlockSpec((1, m, d), lambda b: (b, 0, 0)),
            _const_spec((1, d)),
            _const_spec((d, XHEADS, XDIM)),
            _const_spec((d, XHEADS, XDIM)),
            _const_spec((d, XHEADS, XDIM)),
        ],
        out_specs=[
            pl.BlockSpec((1, d, m), lambda b: (b, 0, 0)),
            pl.BlockSpec((1, m, d), lambda b: (b, 0, 0)),
            pl.BlockSpec((d, d), lambda b: (0, 0)),
        ],
        out_shape=[
            jax.ShapeDtypeStruct((bsz, d, m), BF16),
            jax.ShapeDtypeStruct((bsz, m, d), BF16),
            jax.ShapeDtypeStruct((d, d), BF16),
        ],
        scratch_shapes=[pltpu.VMEM((d, d), BF16), pltpu.VMEM((d, d), BF16)],
        compiler_params=pltpu.CompilerParams(
            dimension_semantics=("arbitrary",), vmem_limit_bytes=VMEM_LIMIT),
        name="memkv",
    )(mem, g, wq, wk, wv)


def _tail_kernel(x_ref, gx_ref, wq_ref, kt_ref, v_ref, wo_ref, gf_ref, w1_ref, w2_ref, gl_ref, out_ref):
    hm = x_ref.shape[1] // 2
    head_cols = [slice(h * XDIM, (h + 1) * XDIM) for h in range(XHEADS)]
    rows = [slice(i * hm, (i + 1) * hm) for i in range(2)]
    xs = [x_ref[0, r, :] for r in rows]

    def q_proj(x):
        return _dot(_rms(x, gx_ref[...]).astype(BF16), wq_ref[...]).astype(BF16)

    def scores(xq):
        return [_dot(xq[:, c], kt_ref[0, c, :]) for c in head_cols]

    def attend(ss):
        outs = []
        for s, c in zip(ss, head_cols):
            e = jnp.exp(s - jnp.max(s, axis=-1, keepdims=True))
            p = e * (1.0 / jnp.sum(e, axis=-1, keepdims=True))
            outs.append(_dot(p.astype(BF16), v_ref[0, :, c]).astype(BF16))
        return jnp.concatenate(outs, axis=1)

    def o_proj(x, att):
        return x + _dot(att, wo_ref[...].astype(BF16))

    def mlp(i, x):
        hf = _rms(x, gf_ref[...]).astype(BF16)
        acc = x
        for j in range(D_FF // FF_BLOCK):
            cols = slice(j * FF_BLOCK, (j + 1) * FF_BLOCK)
            u = jnp.maximum(_dot(hf, w1_ref[:, cols].astype(BF16)), 0.0)
            acc = acc + _dot((u * u).astype(BF16), w2_ref[cols, :].astype(BF16))
        out_ref[0, rows[i], :] = _rms(acc, gl_ref[...])

    s_a = scores(q_proj(xs[0]))
    xq_b = q_proj(xs[1])
    att_a = attend(s_a)
    s_b = scores(xq_b)
    x2_a = o_proj(xs[0], att_a)
    x2_b = o_proj(xs[1], attend(s_b))
    mlp(0, x2_a)
    mlp(1, x2_b)


def _tail_call(x, gx, wq, kt, v, wo, gf, w1, w2, gl):
    bsz, seq, d = x.shape
    tm = TM_TAIL
    return pl.pallas_call(
        _tail_kernel,
        grid=(bsz, seq // tm),
        in_specs=[
            pl.BlockSpec((1, tm, d), lambda b, s: (b, s, 0)),
            _const_spec((1, d)),
            _const_spec((d, d)),
            pl.BlockSpec((1, d, MEM_LEN), lambda b, s: (b, 0, 0)),
            pl.BlockSpec((1, MEM_LEN, d), lambda b, s: (b, 0, 0)),
            _const_spec((d, d)),
            _const_spec((1, d)),
            _const_spec((d, D_FF)),
            _const_spec((D_FF, d)),
            _const_spec((1, d)),
        ],
        out_specs=pl.BlockSpec((1, tm, d), lambda b, s: (b, s, 0)),
        out_shape=jax.ShapeDtypeStruct(x.shape, F32),
        compiler_params=pltpu.CompilerParams(
            dimension_semantics=("arbitrary", "arbitrary"), vmem_limit_bytes=TAIL_VMEM_LIMIT),
        name="tail",
    )(x, gx, wq, kt, v, wo, gf, w1, w2, gl)


def kernel(x, mem, norm_mix_g, w_in, lb_logits, hgrn_norm_g, w_pool, pool_scale, w_out,
           norm_x_g, norm_mem_g, w_xq, w_xk, w_xv, w_xo, norm_ffn_g, w_ff1, w_ff2, final_norm_g):
    bsz, seq, d = x.shape
    depth = norm_mix_g.shape[0]
    assert depth == 1 and lb_logits.shape[0] == 2
    row = lambda a: a.reshape(1, -1).astype(F32)
    x = _mix_call(x, row(norm_mix_g), w_in.reshape(d, IN_WIDTH), lb_logits.astype(F32),
                  row(hgrn_norm_g), w_pool.reshape(len(POOL_WINDOWS), DK, DK), row(pool_scale),
                  w_out.reshape(d, d))
    split = (d, XHEADS, XDIM)
    kt, v, wq = _memkv_call(mem, row(norm_mem_g), w_xq.reshape(split), w_xk.reshape(split), w_xv.reshape(split))
    return _tail_call(x, row(norm_x_g), wq, kt, v, w_xo.reshape(d, d),
                      row(norm_ffn_g), w_ff1.reshape(d, D_FF), w_ff2.reshape(D_FF, d), row(final_norm_g))
```

```python
import functools

import jax
import jax.numpy as jnp
import numpy as np
from jax import lax
from jax.experimental import pallas as pl
from jax.experimental.pallas import tpu as pltpu

D_MODEL = 1024
HEADS = 4
DK = 128
KW = HEADS * DK
PW = 2 * DK
CHUNK = 64
LEVELS = (32, 16, 8, 4, 2, 1)
POOL_WINDOWS = (2, 4, 8, 16)
POOL_HIST = 16
IN_WIDTH = 5 * KW
MEM_LEN = 256
XHEADS = 4
XDIM = 256
D_FF = 4096
EPS = 1e-6
LOG2E = 1.4426950408889634

TS_MIX = 512
SUB_TILE = 256
IN_BLOCK = 256
TM_TAIL = 512
FF_BLOCK = 1024
VMEM_LIMIT = 56 * 1024 * 1024
TAIL_VMEM_LIMIT = 60 * 1024 * 1024

F32 = jnp.float32
BF16 = jnp.bfloat16


def _rms(x, g):
    return x * lax.rsqrt(jnp.mean(x * x, axis=-1, keepdims=True) + EPS) * g


def _dot(a, b):
    return jnp.dot(a, b, preferred_element_type=F32)


def _dot_nt(a, b):
    return lax.dot_general(a, b, (((1,), (1,)), ((), ())), preferred_element_type=F32)


def _dot_tn(a, b):
    return lax.dot_general(a, b, (((0,), (0,)), ((), ())), preferred_element_type=F32)


def _sigmoid(x):
    return 1.0 / (1.0 + jnp.exp(-x))


def _block_diag(a, b):
    z = jnp.zeros_like(a)
    return jnp.concatenate([jnp.concatenate([a, z], axis=1), jnp.concatenate([z, b], axis=1)], axis=0)


def _cumsum_matrix():
    t = np.arange(CHUNK)[:, None]
    s = np.arange(CHUNK)[None, :]
    return (s <= t).astype(np.float32)


def _hgrn_tile(z_ref, ts, cm, lb_all, hg_all, st_ref, mixed_ref):
    n_chunks = ts // CHUNK
    qp = z_ref[:, 0:KW]
    fp = z_ref[:, KW:2 * KW]
    gp = z_ref[:, 3 * KW:4 * KW]

    f = lb_all + (1.0 - lb_all) * _sigmoid(fp)
    kk_all = 1.0 - f
    q_all = qp * _sigmoid(qp)
    gate_all = gp * _sigmoid(gp)

    lf2 = jnp.log(f) * LOG2E
    hi = lf2.astype(BF16)
    lo = (lf2 - hi.astype(F32)).astype(BF16)
    b_chunks = []
    for ci in range(n_chunks):
        r = slice(ci * CHUNK, (ci + 1) * CHUNK)
        b_chunks.append(_dot(cm, hi[r]) + _dot(cm, lo[r]))
    yield "prep"

    row = lax.broadcasted_iota(jnp.int32, (CHUNK, PW), 0)
    upper = {L: jnp.bitwise_and(row, 2 * L - 1) >= L for L in LEVELS}
    out_row = lax.broadcasted_iota(jnp.int32, (CHUNK, DK), 0)
    out_col = jnp.bitwise_and(lax.broadcasted_iota(jnp.int32, (CHUNK, DK), 1), CHUNK - 1)
    same_block = {L: jnp.bitwise_and(jnp.bitwise_xor(out_row, out_col), -2 * L) == 0 for L in LEVELS[1:]}
    diagonal = out_row == out_col

    def ref_rows(a, first, period, width):
        parts = [jnp.broadcast_to(a[r:r + 1, :], (width, a.shape[1])) for r in range(first, CHUNK, period)]
        return parts[0] if len(parts) == 1 else jnp.concatenate(parts, axis=0)

    def level_ref(b, L):
        if L >= 4:
            return ref_rows(b, L - 1, 2 * L, 2 * L)
        return jnp.where(jnp.bitwise_and(row, 7) < 4, ref_rows(b, 1, 8, 8), ref_rows(b, 5, 8, 8))

    def pair_dot(qe, ke):
        rhs = _block_diag(ke[:, :DK], ke[:, DK:]).T.astype(BF16)
        return _dot(qe.astype(BF16), rhs)

    def value_dot(entry):
        level_scores, v_bf = entry[5], entry[6]
        scores = jnp.where(diagonal, level_scores[-1], 0.0)
        for L, s_l in zip(LEVELS[1:], level_scores[1:-1]):
            scores = scores + jnp.where(same_block[L], s_l, 0.0)
        scores = scores + level_scores[0]
        return _dot(scores.astype(BF16), _block_diag(v_bf[:, :DK], v_bf[:, DK:]))

    work = []
    o_intra = []
    for ci in range(n_chunks):
        r = slice(ci * CHUNK, (ci + 1) * CHUNK)
        for hp in range(HEADS // 2):
            pc = slice(hp * PW, (hp + 1) * PW)
            q, kk, b = q_all[r, pc], kk_all[r, pc], b_chunks[ci][:, pc]
            b_last = b[CHUNK - 1:CHUNK, :]
            v_bf = z_ref[r, 2 * KW + hp * PW:2 * KW + (hp + 1) * PW].astype(BF16)

            qe_bf = (q * jnp.exp2(b)).astype(BF16)
            k_dec = (kk * jnp.exp2(b_last - b)).astype(BF16)
            upd = [_dot_tn(v_bf[:, i * DK:(i + 1) * DK], k_dec[:, i * DK:(i + 1) * DK]) for i in range(2)]

            level_scores = []
            for L in LEVELS:
                if L == 1:
                    x = jnp.where(upper[L], q * f[r, pc], kk)
                else:
                    x = jnp.where(upper[L], q, kk) * jnp.exp2(-jnp.abs(b - level_ref(b, L)))
                level_scores.append(pair_dot(jnp.where(upper[L], x, 0.0), jnp.where(upper[L], 0.0, x)))
            level_scores.append(pair_dot(q, kk))
            work.append((ci, hp, qe_bf, jnp.exp2(b_last), upd, level_scores, v_bf))
            if len(work) > 1:
                o_intra.append(value_dot(work[-2]))
            yield "levels"
    o_intra.append(value_dot(work[-1]))
    yield "levels"

    state = [st_ref[h] for h in range(HEADS)]
    o_inter = []
    for ci, hp, qe_bf, decay, upd, level_scores, v_bf in work:
        h0, h1 = 2 * hp, 2 * hp + 1
        o_inter.append(_dot_nt(qe_bf, _block_diag(state[h0].astype(BF16), state[h1].astype(BF16))))
        state[h0] = state[h0] * decay[:, :DK] + upd[0]
        state[h1] = state[h1] * decay[:, DK:] + upd[1]
    for h in range(HEADS):
        st_ref[h] = state[h]
    yield "state"

    for (ci, hp, *_), oa, ob in zip(work, o_inter, o_intra):
        r = slice(ci * CHUNK, (ci + 1) * CHUNK)
        o_pair = ob + oa
        for i in range(2):
            cols = slice((2 * hp + i) * DK, (2 * hp + i + 1) * DK)
            o = o_pair[:, i * DK:(i + 1) * DK]
            o = o * lax.rsqrt(jnp.mean(o * o, axis=-1, keepdims=True) + EPS) * hg_all[:, cols]
            mixed_ref[r, cols] = o * gate_all[r, cols]
        yield "norm"


def _mix_kernel(xn_ref, xp_ref, g_ref, w_in_f32, lbl_ref, hg_ref, w_pool_f32, ps_ref, w_out_f32, cm_ref,
                out_ref, za_ref, zb_ref, st_ref, pext_ref, mixed_ref, w_in_ref, w_pool_ref, w_out_ref,
                *, tiles_per_seq):
    g = pl.program_id(0)
    si = lax.rem(g + tiles_per_seq - 1, tiles_per_seq)

    @pl.when(g == 0)
    def _():
        zb_ref[...] = jnp.zeros(zb_ref.shape, F32)
        w_in_ref[...] = w_in_f32[...].astype(BF16)
        w_pool_ref[...] = w_pool_f32[...].astype(BF16)
        w_out_ref[...] = w_out_f32[...].astype(BF16)

    @pl.when(jnp.logical_or(si == 0, g == 0))
    def _():
        st_ref[...] = jnp.zeros_like(st_ref)
        pext_ref[0:POOL_HIST, :] = jnp.zeros((POOL_HIST, KW), F32)

    args = (xn_ref, xp_ref, g_ref, w_in_ref, lbl_ref, hg_ref, w_pool_ref, ps_ref, w_out_ref, cm_ref,
            out_ref, st_ref, pext_ref, mixed_ref, si)

    @pl.when(lax.rem(g, 2) == 0)
    def _():
        _mix_step(*args, zn_ref=za_ref, zc_ref=zb_ref)

    @pl.when(lax.rem(g, 2) == 1)
    def _():
        _mix_step(*args, zn_ref=zb_ref, zc_ref=za_ref)


def _mix_step(xn_ref, xp_ref, g_ref, w_in_ref, lbl_ref, hg_ref, w_pool_ref, ps_ref, w_out_ref, cm_ref,
              out_ref, st_ref, pext_ref, mixed_ref, si, *, zn_ref, zc_ref):
    ts = xn_ref.shape[1]
    d = xn_ref.shape[2]
    n_sub = ts // SUB_TILE
    hn_bf = _rms(xn_ref[0], g_ref[...]).astype(BF16)

    def in_proj_unit(blk):
        def run():
            cols = slice(blk * IN_BLOCK, (blk + 1) * IN_BLOCK)
            zn_ref[:, cols] = _dot(hn_bf, w_in_ref[:, cols])
        return run

    def out_proj_unit(sub, blk):
        def run():
            r = slice(sub * SUB_TILE, (sub + 1) * SUB_TILE)
            cols = slice(blk * IN_BLOCK, (blk + 1) * IN_BLOCK)
            out_ref[0, r, cols] = xp_ref[0, r, cols] + _dot(mixed_ref[r, :].astype(BF16), w_out_ref[:, cols])
        return run

    in_units = [in_proj_unit(blk) for blk in range(IN_WIDTH // IN_BLOCK)]

    l0 = lbl_ref[0:1, :]
    l1 = lbl_ref[1:2, :]
    lmax = jnp.maximum(l0, l1)
    e0 = jnp.exp(l0 - lmax)
    lb_all = e0 / (e0 + jnp.exp(l1 - lmax))

    def pool(sub):
        r = slice(sub * SUB_TILE, (sub + 1) * SUB_TILE)
        p = zc_ref[r, 4 * KW:5 * KW]
        pext_ref[POOL_HIST:, :] = p
        pos = (si * ts + sub * SUB_TILE + 1 + lax.broadcasted_iota(jnp.int32, (SUB_TILE, 1), 0)).astype(F32)
        for gi, w in enumerate(POOL_WINDOWS):
            cols = slice(gi * DK, (gi + 1) * DK)
            acc = pext_ref[:, cols]
            sh = 1
            while sh < w:
                acc = acc + pltpu.roll(acc, sh, 0)
                sh *= 2
            win = acc[POOL_HIST:, :]
            pooled = win / jnp.minimum(pos, float(w)) - p[:, cols]
            y = _dot(pooled.astype(BF16), w_pool_ref[gi]) * ps_ref[:, cols]
            mixed_ref[r, KW + gi * DK:KW + (gi + 1) * DK] = y
        pext_ref[0:POOL_HIST, :] = pext_ref[SUB_TILE:SUB_TILE + POOL_HIST, :]

    def hgrn(sub):
        r = pl.ds(sub * SUB_TILE, SUB_TILE)
        return _hgrn_tile(zc_ref.at[r], SUB_TILE, cm_ref[...], lb_all, hg_ref[...], st_ref, mixed_ref.at[r])

    def step(gen, label):
        got = next(gen)
        assert got == label, (got, label)

    def fill(n=1):
        for _ in range(n):
            if in_units:
                in_units.pop(0)()

    n_items = (SUB_TILE // CHUNK) * (HEADS // 2)
    gens = [hgrn(sub) for sub in range(n_sub)]
    out_units = []
    step(gens[0], "prep")
    fill(2)
    for i in range(n_items + 1):
        step(gens[0], "levels")
        if i % 2 == 1:
            fill()
    for sub in range(n_sub):
        cur = gens[sub]
        nxt = gens[sub + 1] if sub + 1 < n_sub else None
        if nxt is not None:
            step(nxt, "prep")
        step(cur, "state")
        for i in range(n_items):
            if nxt is not None:
                step(nxt, "levels")
            step(cur, "norm")
            if i % 2 == 1:
                (out_units.pop(0) if out_units else fill)()
        if nxt is not None:
            step(nxt, "levels")
        pool(sub)
        out_units = [out_proj_unit(sub, blk) for blk in range(d // IN_BLOCK)]
    fill(len(in_units))
    for unit in out_units:
        unit()


def _const_spec(shape):
    nd = len(shape)
    return pl.BlockSpec(shape, lambda *_: (0,) * nd, pipeline_mode=pl.Buffered(1))


def _mix_call(x, g, w_in, lbl, hg, w_pool, ps, w_out):
    bsz, seq, d = x.shape
    ts = TS_MIX
    ns = seq // ts
    n_tiles = bsz * ns
    cm = jnp.asarray(_cumsum_matrix(), BF16)

    def tile_next(i):
        t = jnp.minimum(i, n_tiles - 1)
        return (t // ns, t % ns, 0)

    def tile_prev(i):
        t = jnp.maximum(i - 1, 0)
        return (t // ns, t % ns, 0)

    return pl.pallas_call(
        functools.partial(_mix_kernel, tiles_per_seq=ns),
        grid=(n_tiles + 1,),
        in_specs=[
            pl.BlockSpec((1, ts, d), tile_next),
            pl.BlockSpec((1, ts, d), tile_prev),
            _const_spec((1, d)),
            _const_spec((d, IN_WIDTH)),
            _const_spec((2, KW)),
            _const_spec((1, KW)),
            _const_spec((len(POOL_WINDOWS), DK, DK)),
            _const_spec((1, KW)),
            _const_spec((d, d)),
            _const_spec((CHUNK, CHUNK)),
        ],
        out_specs=pl.BlockSpec((1, ts, d), tile_prev),
        out_shape=jax.ShapeDtypeStruct(x.shape, F32),
        scratch_shapes=[
            pltpu.VMEM((ts, IN_WIDTH), F32),
            pltpu.VMEM((ts, IN_WIDTH), F32),
            pltpu.VMEM((HEADS, DK, DK), F32),
            pltpu.VMEM((POOL_HIST + SUB_TILE, KW), F32),
            pltpu.VMEM((ts, d), F32),
            pltpu.VMEM((d, IN_WIDTH), BF16),
            pltpu.VMEM((len(POOL_WINDOWS), DK, DK), BF16),
            pltpu.VMEM((d, d), BF16),
        ],
        compiler_params=pltpu.CompilerParams(
            dimension_semantics=("arbitrary",), vmem_limit_bytes=VMEM_LIMIT),
        name="mix",
    )(x, x, g, w_in, lbl, hg, w_pool, ps, w_out, cm)


def _memkv_kernel(mem_ref, g_ref, wq_ref, wk_ref, wv_ref, kt_ref, v_ref, wq_out_ref, wk_bf, wv_bf):
    def dense(w_ref):
        return jnp.concatenate([w_ref[:, h, :] for h in range(XHEADS)], axis=1).astype(BF16)

    @pl.when(pl.program_id(0) == 0)
    def _():
        wq_out_ref[...] = dense(wq_ref)
        wk_bf[...] = dense(wk_ref)
        wv_bf[...] = dense(wv_ref)

    hm = _rms(mem_ref[0], g_ref[...]).astype(BF16)
    k = _dot(hm, wk_bf[...]) * (XDIM ** -0.5)
    kt_ref[0] = k.T.astype(BF16)
    v_ref[0] = _dot(hm, wv_bf[...]).astype(BF16)


def _memkv_call(mem, g, wq, wk, wv):
    bsz, m, d = mem.shape
    return pl.pallas_call(
        _memkv_kernel,
        grid=(bsz,),
        in_specs=[
            pl.BlockSpec((1, m, d), lambda b: (b, 0, 0)),
            _const_spec((1, d)),
            _const_spec((d, XHEADS, XDIM)),
            _const_spec((d, XHEADS, XDIM)),
            _const_spec((d, XHEADS, XDIM)),
        ],
        out_specs=[
            pl.BlockSpec((1, d, m), lambda b: (b, 0, 0)),
            pl.BlockSpec((1, m, d), lambda b: (b, 0, 0)),
            pl.BlockSpec((d, d), lambda b: (0, 0)),
        ],
        out_shape=[
            jax.ShapeDtypeStruct((bsz, d, m), BF16),
            jax.ShapeDtypeStruct((bsz, m, d), BF16),
            jax.ShapeDtypeStruct((d, d), BF16),
        ],
        scratch_shapes=[pltpu.VMEM((d, d), BF16), pltpu.VMEM((d, d), BF16)],
        compiler_params=pltpu.CompilerParams(
            dimension_semantics=("arbitrary",), vmem_limit_bytes=VMEM_LIMIT),
        name="memkv",
    )(mem, g, wq, wk, wv)


def _tail_kernel(x_ref, gx_ref, wq_ref, kt_ref, v_ref, wo_ref, gf_ref, w1_ref, w2_ref, gl_ref, out_ref):
    hm = x_ref.shape[1] // 2
    head_cols = [slice(h * XDIM, (h + 1) * XDIM) for h in range(XHEADS)]
    rows = [slice(i * hm, (i + 1) * hm) for i in range(2)]
    xs = [x_ref[0, r, :] for r in rows]

    def q_proj(x):
        return _dot(_rms(x, gx_ref[...]).astype(BF16), wq_ref[...]).astype(BF16)

    def scores(xq):
        return [_dot(xq[:, c], kt_ref[0, c, :]) for c in head_cols]

    def attend(ss):
        outs = []
        for s, c in zip(ss, head_cols):
            e = jnp.exp(s - jnp.max(s, axis=-1, keepdims=True))
            p = e * (1.0 / jnp.sum(e, axis=-1, keepdims=True))
            outs.append(_dot(p.astype(BF16), v_ref[0, :, c]).astype(BF16))
        return jnp.concatenate(outs, axis=1)

    def o_proj(x, att):
        return x + _dot(att, wo_ref[...].astype(BF16))

    def mlp(i, x):
        hf = _rms(x, gf_ref[...]).astype(BF16)
        acc = x
        for j in range(D_FF // FF_BLOCK):
            cols = slice(j * FF_BLOCK, (j + 1) * FF_BLOCK)
            u = jnp.maximum(_dot(hf, w1_ref[:, cols].astype(BF16)), 0.0)
            acc = acc + _dot((u * u).astype(BF16), w2_ref[cols, :].astype(BF16))
        out_ref[0, rows[i], :] = _rms(acc, gl_ref[...])

    s_a = scores(q_proj(xs[0]))
    xq_b = q_proj(xs[1])
    att_a = attend(s_a)
    s_b = scores(xq_b)
    x2_a = o_proj(xs[0], att_a)
    x2_b = o_proj(xs[1], attend(s_b))
    mlp(0, x2_a)
    mlp(1, x2_b)


def _tail_call(x, gx, wq, kt, v, wo, gf, w1, w2, gl):
    bsz, seq, d = x.shape
    tm = TM_TAIL
    return pl.pallas_call(
        _tail_kernel,
        grid=(bsz, seq // tm),
        in_specs=[
            pl.BlockSpec((1, tm, d), lambda b, s: (b, s, 0)),
            _const_spec((1, d)),
            _const_spec((d, d)),
            pl.BlockSpec((1, d, MEM_LEN), lambda b, s: (b, 0, 0)),
            pl.BlockSpec((1, MEM_LEN, d), lambda b, s: (b, 0, 0)),
            _const_spec((d, d)),
            _const_spec((1, d)),
            _const_spec((d, D_FF)),
            _const_spec((D_FF, d)),
            _const_spec((1, d)),
        ],
        out_specs=pl.BlockSpec((1, tm, d), lambda b, s: (b, s, 0)),
        out_shape=jax.ShapeDtypeStruct(x.shape, F32),
        compiler_params=pltpu.CompilerParams(
            dimension_semantics=("arbitrary", "arbitrary"), vmem_limit_bytes=TAIL_VMEM_LIMIT),
        name="tail",
    )(x, gx, wq, kt, v, wo, gf, w1, w2, gl)


def kernel(x, mem, norm_mix_g, w_in, lb_logits, hgrn_norm_g, w_pool, pool_scale, w_out,
           norm_x_g, norm_mem_g, w_xq, w_xk, w_xv, w_xo, norm_ffn_g, w_ff1, w_ff2, final_norm_g):
    bsz, seq, d = x.shape
    depth = norm_mix_g.shape[0]
    assert depth == 1 and lb_logits.shape[0] == 2
    row = lambda a: a.reshape(1, -1).astype(F32)
    x = _mix_call(x, row(norm_mix_g), w_in.reshape(d, IN_WIDTH), lb_logits.astype(F32),
                  row(hgrn_norm_g), w_pool.reshape(len(POOL_WINDOWS), DK, DK), row(pool_scale),
                  w_out.reshape(d, d))
    split = (d, XHEADS, XDIM)
    kt, v, wq = _memkv_call(mem, row(norm_mem_g), w_xq.reshape(split), w_xk.reshape(split), w_xv.reshape(split))
    return _tail_call(x, row(norm_x_g), wq, kt, v, w_xo.reshape(d, d),
                      row(norm_ffn_g), w_ff1.reshape(d, D_FF), w_ff2.reshape(D_FF, d), row(final_norm_g))
```

```python
import functools

import jax
import jax.numpy as jnp
import numpy as np
from jax import lax
from jax.experimental import pallas as pl
from jax.experimental.pallas import tpu as pltpu

D_MODEL = 1024
HEADS = 4
DK = 128
KW = HEADS * DK
PW = 2 * DK
CHUNK = 64
LEVELS = (32, 16, 8, 4, 2, 1)
POOL_WINDOWS = (2, 4, 8, 16)
POOL_HIST = 16
IN_WIDTH = 5 * KW
MEM_LEN = 256
XHEADS = 4
XDIM = 256
D_FF = 4096
EPS = 1e-6
LOG2E = 1.4426950408889634

TS_MIX = 512
SUB_TILE = 256
IN_BLOCK = 256
TM_TAIL = 512
FF_BLOCK = 1024
VMEM_LIMIT = 56 * 1024 * 1024
TAIL_VMEM_LIMIT = 60 * 1024 * 1024

F32 = jnp.float32
BF16 = jnp.bfloat16


def _rms(x, g):
    return x * lax.rsqrt(jnp.mean(x * x, axis=-1, keepdims=True) + EPS) * g


def _dot(a, b):
    return jnp.dot(a, b, preferred_element_type=F32)


def _dot_nt(a, b):
    return lax.dot_general(a, b, (((1,), (1,)), ((), ())), preferred_element_type=F32)


def _dot_tn(a, b):
    return lax.dot_general(a, b, (((0,), (0,)), ((), ())), preferred_element_type=F32)


def _sigmoid(x):
    return 1.0 / (1.0 + jnp.exp(-x))


def _block_diag(a, b):
    z = jnp.zeros_like(a)
    return jnp.concatenate([jnp.concatenate([a, z], axis=1), jnp.concatenate([z, b], axis=1)], axis=0)


def _cumsum_matrix():
    t = np.arange(CHUNK)[:, None]
    s = np.arange(CHUNK)[None, :]
    return (s <= t).astype(np.float32)


def _hgrn_tile(z_ref, ts, cm, lb_all, hg_all, st_ref, mixed_ref):
    n_chunks = ts // CHUNK
    qp = z_ref[:, 0:KW]
    fp = z_ref[:, KW:2 * KW]
    gp = z_ref[:, 3 * KW:4 * KW]

    f = lb_all + (1.0 - lb_all) * _sigmoid(fp)
    kk_all = 1.0 - f
    q_all = qp * _sigmoid(qp)
    gate_all = gp * _sigmoid(gp)

    lf2 = jnp.log(f) * LOG2E
    hi = lf2.astype(BF16)
    lo = (lf2 - hi.astype(F32)).astype(BF16)
    b_chunks = []
    for ci in range(n_chunks):
        r = slice(ci * CHUNK, (ci + 1) * CHUNK)
        b_chunks.append(_dot(cm, hi[r]) + _dot(cm, lo[r]))
    yield "prep"

    row = lax.broadcasted_iota(jnp.int32, (CHUNK, PW), 0)
    upper = {L: jnp.bitwise_and(row, 2 * L - 1) >= L for L in LEVELS}
    out_row = lax.broadcasted_iota(jnp.int32, (CHUNK, DK), 0)
    out_col = jnp.bitwise_and(lax.broadcasted_iota(jnp.int32, (CHUNK, DK), 1), CHUNK - 1)
    same_block = {L: jnp.bitwise_and(jnp.bitwise_xor(out_row, out_col), -2 * L) == 0 for L in LEVELS[1:]}
    diagonal = out_row == out_col

    def ref_rows(a, first, period, width):
        parts = [jnp.broadcast_to(a[r:r + 1, :], (width, a.shape[1])) for r in range(first, CHUNK, period)]
        return parts[0] if len(parts) == 1 else jnp.concatenate(parts, axis=0)

    def level_ref(b, L):
        if L >= 4:
            return ref_rows(b, L - 1, 2 * L, 2 * L)
        return jnp.where(jnp.bitwise_and(row, 7) < 4, ref_rows(b, 1, 8, 8), ref_rows(b, 5, 8, 8))

    def pair_dot(qe, ke):
        rhs = _block_diag(ke[:, :DK], ke[:, DK:]).T.astype(BF16)
        return _dot(qe.astype(BF16), rhs)

    def value_dot(entry):
        level_scores, v_bf = entry[5], entry[6]
        scores = jnp.where(diagonal, level_scores[-1], 0.0)
        for L, s_l in zip(LEVELS[1:], level_scores[1:-1]):
            scores = scores + jnp.where(same_block[L], s_l, 0.0)
        scores = scores + level_scores[0]
        return _dot(scores.astype(BF16), _block_diag(v_bf[:, :DK], v_bf[:, DK:]))

    work = []
    o_intra = []
    for ci in range(n_chunks):
        r = slice(ci * CHUNK, (ci + 1) * CHUNK)
        for hp in range(HEADS // 2):
            pc = slice(hp * PW, (hp + 1) * PW)
            q, kk, b = q_all[r, pc], kk_all[r, pc], b_chunks[ci][:, pc]
            b_last = b[CHUNK - 1:CHUNK, :]
            v_bf = z_ref[r, 2 * KW + hp * PW:2 * KW + (hp + 1) * PW].astype(BF16)

            qe_bf = (q * jnp.exp2(b)).astype(BF16)
            k_dec = (kk * jnp.exp2(b_last - b)).astype(BF16)
            upd = [_dot_tn(v_bf[:, i * DK:(i + 1) * DK], k_dec[:, i * DK:(i + 1) * DK]) for i in range(2)]

            level_scores = []
            for L in LEVELS:
                if L == 1:
                    x = jnp.where(upper[L], q * f[r, pc], kk)
                else:
                    x = jnp.where(upper[L], q, kk) * jnp.exp2(-jnp.abs(b - level_ref(b, L)))
                level_scores.append(pair_dot(jnp.where(upper[L], x, 0.0), jnp.where(upper[L], 0.0, x)))
            level_scores.append(pair_dot(q, kk))
            work.append((ci, hp, qe_bf, jnp.exp2(b_last), upd, level_scores, v_bf))
            if len(work) > 1:
                o_intra.append(value_dot(work[-2]))
            yield "levels"
    o_intra.append(value_dot(work[-1]))
    yield "levels"

    state = [st_ref[h] for h in range(HEADS)]
    o_inter = []
    for ci, hp, qe_bf, decay, upd, level_scores, v_bf in work:
        h0, h1 = 2 * hp, 2 * hp + 1
        o_inter.append(_dot_nt(qe_bf, _block_diag(state[h0].astype(BF16), state[h1].astype(BF16))))
        state[h0] = state[h0] * decay[:, :DK] + upd[0]
        state[h1] = state[h1] * decay[:, DK:] + upd[1]
    for h in range(HEADS):
        st_ref[h] = state[h]
    yield "state"

    for (ci, hp, *_), oa, ob in zip(work, o_inter, o_intra):
        r = slice(ci * CHUNK, (ci + 1) * CHUNK)
        o_pair = ob + oa
        for i in range(2):
            cols = slice((2 * hp + i) * DK, (2 * hp + i + 1) * DK)
            o = o_pair[:, i * DK:(i + 1) * DK]
            o = o * lax.rsqrt(jnp.mean(o * o, axis=-1, keepdims=True) + EPS) * hg_all[:, cols]
            mixed_ref[r, cols] = o * gate_all[r, cols]
        yield "norm"


def _mix_kernel(xn_ref, xp_ref, g_ref, w_in_f32, lbl_ref, hg_ref, w_pool_f32, ps_ref, w_out_f32, cm_ref,
                out_ref, za_ref, zb_ref, st_ref, pext_ref, mixed_ref, w_in_ref, w_pool_ref, w_out_ref,
                *, tiles_per_seq):
    g = pl.program_id(0)
    si = lax.rem(g + tiles_per_seq - 1, tiles_per_seq)

    @pl.when(g == 0)
    def _():
        w_in_ref[...] = w_in_f32[...].astype(BF16)
        w_pool_ref[...] = w_pool_f32[...].astype(BF16)
        w_out_ref[...] = w_out_f32[...].astype(BF16)
        za_ref[...] = _dot(_rms(xn_ref[0], g_ref[...]).astype(BF16), w_in_ref[...])

    @pl.when(si == 0)
    def _():
        st_ref[...] = jnp.zeros_like(st_ref)
        pext_ref[0:POOL_HIST, :] = jnp.zeros((POOL_HIST, KW), F32)

    args = (xn_ref, xp_ref, g_ref, w_in_ref, lbl_ref, hg_ref, w_pool_ref, ps_ref, w_out_ref, cm_ref,
            out_ref, st_ref, pext_ref, mixed_ref, si)

    @pl.when(jnp.logical_and(lax.rem(g, 2) == 0, g > 0))
    def _():
        _mix_step(*args, zn_ref=za_ref, zc_ref=zb_ref)

    @pl.when(lax.rem(g, 2) == 1)
    def _():
        _mix_step(*args, zn_ref=zb_ref, zc_ref=za_ref)


def _mix_step(xn_ref, xp_ref, g_ref, w_in_ref, lbl_ref, hg_ref, w_pool_ref, ps_ref, w_out_ref, cm_ref,
              out_ref, st_ref, pext_ref, mixed_ref, si, *, zn_ref, zc_ref):
    ts = xn_ref.shape[1]
    d = xn_ref.shape[2]
    n_sub = ts // SUB_TILE
    hn_bf = _rms(xn_ref[0], g_ref[...]).astype(BF16)

    def in_proj_unit(blk):
        def run():
            cols = slice(blk * IN_BLOCK, (blk + 1) * IN_BLOCK)
            zn_ref[:, cols] = _dot(hn_bf, w_in_ref[:, cols])
        return run

    def out_proj_unit(sub, blk):
        def run():
            r = slice(sub * SUB_TILE, (sub + 1) * SUB_TILE)
            cols = slice(blk * IN_BLOCK, (blk + 1) * IN_BLOCK)
            out_ref[0, r, cols] = xp_ref[0, r, cols] + _dot(mixed_ref[r, :].astype(BF16), w_out_ref[:, cols])
        return run

    in_units = [in_proj_unit(blk) for blk in range(IN_WIDTH // IN_BLOCK)]

    l0 = lbl_ref[0:1, :]
    l1 = lbl_ref[1:2, :]
    lmax = jnp.maximum(l0, l1)
    e0 = jnp.exp(l0 - lmax)
    lb_all = e0 / (e0 + jnp.exp(l1 - lmax))

    def pool(sub):
        r = slice(sub * SUB_TILE, (sub + 1) * SUB_TILE)
        p = zc_ref[r, 4 * KW:5 * KW]
        pext_ref[POOL_HIST:, :] = p
        pos = (si * ts + sub * SUB_TILE + 1 + lax.broadcasted_iota(jnp.int32, (SUB_TILE, 1), 0)).astype(F32)
        for gi, w in enumerate(POOL_WINDOWS):
            cols = slice(gi * DK, (gi + 1) * DK)
            acc = pext_ref[:, cols]
            sh = 1
            while sh < w:
                acc = acc + pltpu.roll(acc, sh, 0)
                sh *= 2
            win = acc[POOL_HIST:, :]
            pooled = win / jnp.minimum(pos, float(w)) - p[:, cols]
            y = _dot(pooled.astype(BF16), w_pool_ref[gi]) * ps_ref[:, cols]
            mixed_ref[r, KW + gi * DK:KW + (gi + 1) * DK] = y
        pext_ref[0:POOL_HIST, :] = pext_ref[SUB_TILE:SUB_TILE + POOL_HIST, :]

    def hgrn(sub):
        r = pl.ds(sub * SUB_TILE, SUB_TILE)
        return _hgrn_tile(zc_ref.at[r], SUB_TILE, cm_ref[...], lb_all, hg_ref[...], st_ref, mixed_ref.at[r])

    def step(gen, label):
        got = next(gen)
        assert got == label, (got, label)

    def fill(n=1):
        for _ in range(n):
            if in_units:
                in_units.pop(0)()

    n_items = (SUB_TILE // CHUNK) * (HEADS // 2)
    gens = [hgrn(sub) for sub in range(n_sub)]
    out_units = []
    step(gens[0], "prep")
    fill(2)
    for i in range(n_items + 1):
        step(gens[0], "levels")
        if i % 2 == 1:
            fill()
    for sub in range(n_sub):
        cur = gens[sub]
        nxt = gens[sub + 1] if sub + 1 < n_sub else None
        if nxt is not None:
            step(nxt, "prep")
        step(cur, "state")
        for i in range(n_items):
            if nxt is not None:
                step(nxt, "levels")
            step(cur, "norm")
            if i % 2 == 1:
                (out_units.pop(0) if out_units else fill)()
        if nxt is not None:
            step(nxt, "levels")
        pool(sub)
        out_units = [out_proj_unit(sub, blk) for blk in range(d // IN_BLOCK)]
    fill(len(in_units))
    for unit in out_units:
        unit()


def _const_spec(shape):
    nd = len(shape)
    return pl.BlockSpec(shape, lambda *_: (0,) * nd, pipeline_mode=pl.Buffered(1))


def _mix_call(x, g, w_in, lbl, hg, w_pool, ps, w_out):
    bsz, seq, d = x.shape
    ts = TS_MIX
    ns = seq // ts
    n_tiles = bsz * ns
    cm = jnp.asarray(_cumsum_matrix(), BF16)

    def tile_next(i):
        t = jnp.minimum(i, n_tiles - 1)
        return (t // ns, t % ns, 0)

    def tile_prev(i):
        t = jnp.maximum(i - 1, 0)
        return (t // ns, t % ns, 0)

    return pl.pallas_call(
        functools.partial(_mix_kernel, tiles_per_seq=ns),
        grid=(n_tiles + 1,),
        in_specs=[
            pl.BlockSpec((1, ts, d), tile_next),
            pl.BlockSpec((1, ts, d), tile_prev),
            _const_spec((1, d)),
            _const_spec((d, IN_WIDTH)),
            _const_spec((2, KW)),
            _const_spec((1, KW)),
            _const_spec((len(POOL_WINDOWS), DK, DK)),
            _const_spec((1, KW)),
            _const_spec((d, d)),
            _const_spec((CHUNK, CHUNK)),
        ],
        out_specs=pl.BlockSpec((1, ts, d), tile_prev),
        out_shape=jax.ShapeDtypeStruct(x.shape, F32),
        scratch_shapes=[
            pltpu.VMEM((ts, IN_WIDTH), F32),
            pltpu.VMEM((ts, IN_WIDTH), F32),
            pltpu.VMEM((HEADS, DK, DK), F32),
            pltpu.VMEM((POOL_HIST + SUB_TILE, KW), F32),
            pltpu.VMEM((ts, d), F32),
            pltpu.VMEM((d, IN_WIDTH), BF16),
            pltpu.VMEM((len(POOL_WINDOWS), DK, DK), BF16),
            pltpu.VMEM((d, d), BF16),
        ],
        compiler_params=pltpu.CompilerParams(
            dimension_semantics=("arbitrary",), vmem_limit_bytes=VMEM_LIMIT),
        name="mix",
    )(x, x, g, w_in, lbl, hg, w_pool, ps, w_out, cm)


def _memkv_kernel(mem_ref, g_ref, wq_ref, wk_ref, wv_ref, kt_ref, v_ref, wq_out_ref, wk_bf, wv_bf):
    def dense(w_ref):
        return jnp.concatenate([w_ref[:, h, :] for h in range(XHEADS)], axis=1).astype(BF16)

    @pl.when(pl.program_id(0) == 0)
    def _():
        wq_out_ref[...] = dense(wq_ref)
        wk_bf[...] = dense(wk_ref)
        wv_bf[...] = dense(wv_ref)

    hm = _rms(mem_ref[0], g_ref[...]).astype(BF16)
    k = _dot(hm, wk_bf[...]) * (XDIM ** -0.5)
    kt_ref[0] = k.T.astype(BF16)
    v_ref[0] = _dot(hm, wv_bf[...]).astype(BF16)


def _memkv_call(mem, g, wq, wk, wv):
    bsz, m, d = mem.shape
    return pl.pallas_call(
        _memkv_kernel,
        grid=(bsz,),
        in_specs=[
            pl.BlockSpec((1, m, d), lambda b: (b, 0, 0)),
            _const_spec((1, d)),
            _const_spec((d, XHEADS, XDIM)),
            _const_spec((d, XHEADS, XDIM)),
            _const_spec((d, XHEADS, XDIM)),
        ],
        out_specs=[
            pl.BlockSpec((1, d, m), lambda b: (b, 0, 0)),
            pl.BlockSpec((1, m, d), lambda b: (b, 0, 0)),
            pl.BlockSpec((d, d), lambda b: (0, 0)),
        ],
        out_shape=[
            jax.ShapeDtypeStruct((bsz, d, m), BF16),
            jax.ShapeDtypeStruct((bsz, m, d), BF16),
            jax.ShapeDtypeStruct((d, d), BF16),
        ],
        scratch_shapes=[pltpu.VMEM((d, d), BF16), pltpu.VMEM((d, d), BF16)],
        compiler_params=pltpu.CompilerParams(
            dimension_semantics=("arbitrary",), vmem_limit_bytes=VMEM_LIMIT),
        name="memkv",
    )(mem, g, wq, wk, wv)


def _tail_kernel(x_ref, gx_ref, wq_ref, kt_ref, v_ref, wo_ref, gf_ref, w1_ref, w2_ref, gl_ref, out_ref):
    hm = x_ref.shape[1] // 2
    head_cols = [slice(h * XDIM, (h + 1) * XDIM) for h in range(XHEADS)]
    rows = [slice(i * hm, (i + 1) * hm) for i in range(2)]
    xs = [x_ref[0, r, :] for r in rows]

    def q_proj(x):
        return _dot(_rms(x, gx_ref[...]).astype(BF16), wq_ref[...]).astype(BF16)

    def scores(xq):
        return [_dot(xq[:, c], kt_ref[0, c, :]) for c in head_cols]

    def attend(ss):
        outs = []
        for s, c in zip(ss, head_cols):
            e = jnp.exp(s - jnp.max(s, axis=-1, keepdims=True))
            p = e * (1.0 / jnp.sum(e, axis=-1, keepdims=True))
            outs.append(_dot(p.astype(BF16), v_ref[0, :, c]).astype(BF16))
        return jnp.concatenate(outs, axis=1)

    def o_proj(x, att):
        return x + _dot(att, wo_ref[...].astype(BF16))

    def mlp(i, x):
        hf = _rms(x, gf_ref[...]).astype(BF16)
        acc = x
        for j in range(D_FF // FF_BLOCK):
            cols = slice(j * FF_BLOCK, (j + 1) * FF_BLOCK)
            u = jnp.maximum(_dot(hf, w1_ref[:, cols].astype(BF16)), 0.0)
            acc = acc + _dot((u * u).astype(BF16), w2_ref[cols, :].astype(BF16))
        out_ref[0, rows[i], :] = _rms(acc, gl_ref[...])

    s_a = scores(q_proj(xs[0]))
    xq_b = q_proj(xs[1])
    att_a = attend(s_a)
    s_b = scores(xq_b)
    x2_a = o_proj(xs[0], att_a)
    x2_b = o_proj(xs[1], attend(s_b))
    mlp(0, x2_a)
    mlp(1, x2_b)


def _tail_call(x, gx, wq, kt, v, wo, gf, w1, w2, gl):
    bsz, seq, d = x.shape
    tm = TM_TAIL
    return pl.pallas_call(
        _tail_kernel,
        grid=(bsz, seq // tm),
        in_specs=[
            pl.BlockSpec((1, tm, d), lambda b, s: (b, s, 0)),
            _const_spec((1, d)),
            _const_spec((d, d)),
            pl.BlockSpec((1, d, MEM_LEN), lambda b, s: (b, 0, 0)),
            pl.BlockSpec((1, MEM_LEN, d), lambda b, s: (b, 0, 0)),
            _const_spec((d, d)),
            _const_spec((1, d)),
            _const_spec((d, D_FF)),
            _const_spec((D_FF, d)),
            _const_spec((1, d)),
        ],
        out_specs=pl.BlockSpec((1, tm, d), lambda b, s: (b, s, 0)),
        out_shape=jax.ShapeDtypeStruct(x.shape, F32),
        compiler_params=pltpu.CompilerParams(
            dimension_semantics=("arbitrary", "arbitrary"), vmem_limit_bytes=TAIL_VMEM_LIMIT),
        name="tail",
    )(x, gx, wq, kt, v, wo, gf, w1, w2, gl)


def kernel(x, mem, norm_mix_g, w_in, lb_logits, hgrn_norm_g, w_pool, pool_scale, w_out,
           norm_x_g, norm_mem_g, w_xq, w_xk, w_xv, w_xo, norm_ffn_g, w_ff1, w_ff2, final_norm_g):
    bsz, seq, d = x.shape
    depth = norm_mix_g.shape[0]
    assert depth == 1 and lb_logits.shape[0] == 2
    row = lambda a: a.reshape(1, -1).astype(F32)
    x = _mix_call(x, row(norm_mix_g), w_in.reshape(d, IN_WIDTH), lb_logits.astype(F32),
                  row(hgrn_norm_g), w_pool.reshape(len(POOL_WINDOWS), DK, DK), row(pool_scale),
                  w_out.reshape(d, d))
    split = (d, XHEADS, XDIM)
    kt, v, wq = _memkv_call(mem, row(norm_mem_g), w_xq.reshape(split), w_xk.reshape(split), w_xv.reshape(split))
    return _tail_call(x, row(norm_x_g), wq, kt, v, w_xo.reshape(d, d),
                      row(norm_ffn_g), w_ff1.reshape(d, D_FF), w_ff2.reshape(D_FF, d), row(final_norm_g))
```

```python
import functools

import jax
import jax.numpy as jnp
import numpy as np
from jax import lax
from jax.experimental import pallas as pl
from jax.experimental.pallas import tpu as pltpu

D_MODEL = 1024
HEADS = 4
DK = 128
KW = HEADS * DK
PW = 2 * DK
CHUNK = 64
LEVELS = (32, 16, 8, 4, 2, 1)
POOL_WINDOWS = (2, 4, 8, 16)
POOL_HIST = 16
IN_WIDTH = 5 * KW
MEM_LEN = 256
XHEADS = 4
XDIM = 256
D_FF = 4096
EPS = 1e-6
LOG2E = 1.4426950408889634

TS_MIX = 512
SUB_TILE = 256
IN_BLOCK = 256
TM_TAIL = 512
FF_BLOCK = 1024
VMEM_LIMIT = 56 * 1024 * 1024
TAIL_VMEM_LIMIT = 60 * 1024 * 1024

F32 = jnp.float32
BF16 = jnp.bfloat16


def _rms(x, g):
    return x * lax.rsqrt(jnp.mean(x * x, axis=-1, keepdims=True) + EPS) * g


def _dot(a, b):
    return jnp.dot(a, b, preferred_element_type=F32)


def _dot_nt(a, b):
    return lax.dot_general(a, b, (((1,), (1,)), ((), ())), preferred_element_type=F32)


def _dot_tn(a, b):
    return lax.dot_general(a, b, (((0,), (0,)), ((), ())), preferred_element_type=F32)


def _sigmoid(x):
    return 1.0 / (1.0 + jnp.exp(-x))


def _block_diag(a, b):
    z = jnp.zeros_like(a)
    return jnp.concatenate([jnp.concatenate([a, z], axis=1), jnp.concatenate([z, b], axis=1)], axis=0)


def _cumsum_matrix():
    t = np.arange(CHUNK)[:, None]
    s = np.arange(CHUNK)[None, :]
    return (s <= t).astype(np.float32)


def _hgrn_tile(z_ref, ts, cm, lb_all, hg_all, st_ref, mixed_ref):
    n_chunks = ts // CHUNK
    qp = z_ref[:, 0:KW]
    fp = z_ref[:, KW:2 * KW]
    gp = z_ref[:, 3 * KW:4 * KW]

    f = lb_all + (1.0 - lb_all) * _sigmoid(fp)
    kk_all = 1.0 - f
    q_all = qp * _sigmoid(qp)
    gate_all = gp * _sigmoid(gp)

    lf2 = jnp.log(f) * LOG2E
    hi = lf2.astype(BF16)
    lo = (lf2 - hi.astype(F32)).astype(BF16)
    b_chunks = []
    for ci in range(n_chunks):
        r = slice(ci * CHUNK, (ci + 1) * CHUNK)
        b_chunks.append(_dot(cm, hi[r]) + _dot(cm, lo[r]))
    yield "prep"

    row = lax.broadcasted_iota(jnp.int32, (CHUNK, PW), 0)
    upper = {L: jnp.bitwise_and(row, 2 * L - 1) >= L for L in LEVELS}
    out_row = lax.broadcasted_iota(jnp.int32, (CHUNK, DK), 0)
    out_col = jnp.bitwise_and(lax.broadcasted_iota(jnp.int32, (CHUNK, DK), 1), CHUNK - 1)
    same_block = {L: jnp.bitwise_and(jnp.bitwise_xor(out_row, out_col), -2 * L) == 0 for L in LEVELS[1:]}
    diagonal = out_row == out_col

    def ref_rows(a, first, period, width):
        parts = [jnp.broadcast_to(a[r:r + 1, :], (width, a.shape[1])) for r in range(first, CHUNK, period)]
        return parts[0] if len(parts) == 1 else jnp.concatenate(parts, axis=0)

    def level_ref(b, L):
        if L >= 4:
            return ref_rows(b, L - 1, 2 * L, 2 * L)
        return jnp.where(jnp.bitwise_and(row, 7) < 4, ref_rows(b, 1, 8, 8), ref_rows(b, 5, 8, 8))

    def pair_dot(qe, ke):
        rhs = _block_diag(ke[:, :DK], ke[:, DK:]).T.astype(BF16)
        return _dot(qe.astype(BF16), rhs)

    def value_dot(entry):
        level_scores, v_bf = entry[5], entry[6]
        scores = jnp.where(diagonal, level_scores[-1], 0.0)
        for L, s_l in zip(LEVELS[1:], level_scores[1:-1]):
            scores = scores + jnp.where(same_block[L], s_l, 0.0)
        scores = scores + level_scores[0]
        return _dot(scores.astype(BF16), _block_diag(v_bf[:, :DK], v_bf[:, DK:]))

    work = []
    o_intra = []
    for ci in range(n_chunks):
        r = slice(ci * CHUNK, (ci + 1) * CHUNK)
        for hp in range(HEADS // 2):
            pc = slice(hp * PW, (hp + 1) * PW)
            q, kk, b = q_all[r, pc], kk_all[r, pc], b_chunks[ci][:, pc]
            b_last = b[CHUNK - 1:CHUNK, :]
            v_bf = z_ref[r, 2 * KW + hp * PW:2 * KW + (hp + 1) * PW].astype(BF16)

            qe_bf = (q * jnp.exp2(b)).astype(BF16)
            k_dec = (kk * jnp.exp2(b_last - b)).astype(BF16)
            upd = [_dot_tn(v_bf[:, i * DK:(i + 1) * DK], k_dec[:, i * DK:(i + 1) * DK]) for i in range(2)]

            level_scores = []
            for L in LEVELS:
                if L == 1:
                    x = jnp.where(upper[L], q * f[r, pc], kk)
                else:
                    x = jnp.where(upper[L], q, kk) * jnp.exp2(-jnp.abs(b - level_ref(b, L)))
                level_scores.append(pair_dot(jnp.where(upper[L], x, 0.0), jnp.where(upper[L], 0.0, x)))
            level_scores.append(pair_dot(q, kk))
            work.append((ci, hp, qe_bf, jnp.exp2(b_last), upd, level_scores, v_bf))
            if len(work) > 1:
                o_intra.append(value_dot(work[-2]))
            yield "levels"
    o_intra.append(value_dot(work[-1]))
    yield "levels"

    state = [st_ref[h] for h in range(HEADS)]
    o_inter = []
    for ci, hp, qe_bf, decay, upd, level_scores, v_bf in work:
        h0, h1 = 2 * hp, 2 * hp + 1
        o_inter.append(_dot_nt(qe_bf, _block_diag(state[h0].astype(BF16), state[h1].astype(BF16))))
        state[h0] = state[h0] * decay[:, :DK] + upd[0]
        state[h1] = state[h1] * decay[:, DK:] + upd[1]
    for h in range(HEADS):
        st_ref[h] = state[h]
    yield "state"

    for (ci, hp, *_), oa, ob in zip(work, o_inter, o_intra):
        r = slice(ci * CHUNK, (ci + 1) * CHUNK)
        o_pair = ob + oa
        for i in range(2):
            cols = slice((2 * hp + i) * DK, (2 * hp + i + 1) * DK)
            o = o_pair[:, i * DK:(i + 1) * DK]
            o = o * lax.rsqrt(jnp.mean(o * o, axis=-1, keepdims=True) + EPS) * hg_all[:, cols]
            mixed_ref[r, cols] = o * gate_all[r, cols]
        yield "norm"


def _mix_kernel(xn_ref, xp_ref, g_ref, w_in_f32, lbl_ref, hg_ref, w_pool_f32, ps_ref, w_out_f32, cm_ref,
                out_ref, za_ref, zb_ref, st_ref, pext_ref, mixed_ref, w_in_ref, w_pool_ref, w_out_ref,
                *, tiles_per_seq):
    g = pl.program_id(0)
    si = lax.rem(g + tiles_per_seq - 1, tiles_per_seq)

    @pl.when(g == 0)
    def _():
        w_in_ref[...] = w_in_f32[...].astype(BF16)
        w_pool_ref[...] = w_pool_f32[...].astype(BF16)
        w_out_ref[...] = w_out_f32[...].astype(BF16)
        za_ref[...] = _dot(_rms(xn_ref[0], g_ref[...]).astype(BF16), w_in_ref[...])

    @pl.when(si == 0)
    def _():
        st_ref[...] = jnp.zeros_like(st_ref)
        pext_ref[0:POOL_HIST, :] = jnp.zeros((POOL_HIST, KW), F32)

    args = (xn_ref, xp_ref, g_ref, w_in_ref, lbl_ref, hg_ref, w_pool_ref, ps_ref, w_out_ref, cm_ref,
            out_ref, st_ref, pext_ref, mixed_ref, si)

    @pl.when(jnp.logical_and(lax.rem(g, 2) == 0, g > 0))
    def _():
        _mix_step(*args, zn_ref=za_ref, zc_ref=zb_ref)

    @pl.when(lax.rem(g, 2) == 1)
    def _():
        _mix_step(*args, zn_ref=zb_ref, zc_ref=za_ref)


def _mix_step(xn_ref, xp_ref, g_ref, w_in_ref, lbl_ref, hg_ref, w_pool_ref, ps_ref, w_out_ref, cm_ref,
              out_ref, st_ref, pext_ref, mixed_ref, si, *, zn_ref, zc_ref):
    ts = xn_ref.shape[1]
    d = xn_ref.shape[2]
    n_sub = ts // SUB_TILE
    hn_bf = _rms(xn_ref[0], g_ref[...]).astype(BF16)

    def in_proj_unit(blk):
        def run():
            cols = slice(blk * IN_BLOCK, (blk + 1) * IN_BLOCK)
            zn_ref[:, cols] = _dot(hn_bf, w_in_ref[:, cols])
        return run

    def out_proj_unit(sub, blk):
        def run():
            r = slice(sub * SUB_TILE, (sub + 1) * SUB_TILE)
            cols = slice(blk * IN_BLOCK, (blk + 1) * IN_BLOCK)
            out_ref[0, r, cols] = xp_ref[0, r, cols] + _dot(mixed_ref[r, :].astype(BF16), w_out_ref[:, cols])
        return run

    in_units = [in_proj_unit(blk) for blk in range(IN_WIDTH // IN_BLOCK)]

    l0 = lbl_ref[0:1, :]
    l1 = lbl_ref[1:2, :]
    lmax = jnp.maximum(l0, l1)
    e0 = jnp.exp(l0 - lmax)
    lb_all = e0 / (e0 + jnp.exp(l1 - lmax))

    def pool(sub):
        r = slice(sub * SUB_TILE, (sub + 1) * SUB_TILE)
        p = zc_ref[r, 4 * KW:5 * KW]
        pext_ref[POOL_HIST:, :] = p
        pos = (si * ts + sub * SUB_TILE + 1 + lax.broadcasted_iota(jnp.int32, (SUB_TILE, 1), 0)).astype(F32)
        for gi, w in enumerate(POOL_WINDOWS):
            cols = slice(gi * DK, (gi + 1) * DK)
            acc = pext_ref[:, cols]
            sh = 1
            while sh < w:
                acc = acc + pltpu.roll(acc, sh, 0)
                sh *= 2
            win = acc[POOL_HIST:, :]
            pooled = win / jnp.minimum(pos, float(w)) - p[:, cols]
            y = _dot(pooled.astype(BF16), w_pool_ref[gi]) * ps_ref[:, cols]
            mixed_ref[r, KW + gi * DK:KW + (gi + 1) * DK] = y
        pext_ref[0:POOL_HIST, :] = pext_ref[SUB_TILE:SUB_TILE + POOL_HIST, :]

    def hgrn(sub):
        r = pl.ds(sub * SUB_TILE, SUB_TILE)
        return _hgrn_tile(zc_ref.at[r], SUB_TILE, cm_ref[...], lb_all, hg_ref[...], st_ref, mixed_ref.at[r])

    def step(gen, label):
        got = next(gen)
        assert got == label, (got, label)

    def fill(n=1):
        for _ in range(n):
            if in_units:
                in_units.pop(0)()

    n_items = (SUB_TILE // CHUNK) * (HEADS // 2)
    gens = [hgrn(sub) for sub in range(n_sub)]
    out_units = []
    step(gens[0], "prep")
    fill(2)
    for i in range(n_items + 1):
        step(gens[0], "levels")
        if i % 2 == 1:
            fill()
    for sub in range(n_sub):
        cur = gens[sub]
        nxt = gens[sub + 1] if sub + 1 < n_sub else None
        if nxt is not None:
            step(nxt, "prep")
        step(cur, "state")
        for i in range(n_items):
            if nxt is not None:
                step(nxt, "levels")
            step(cur, "norm")
            if i % 2 == 1:
                (out_units.pop(0) if out_units else fill)()
        if nxt is not None:
            step(nxt, "levels")
        pool(sub)
        out_units = [out_proj_unit(sub, blk) for blk in range(d // IN_BLOCK)]
    fill(len(in_units))
    for unit in out_units:
        unit()


def _const_spec(shape):
    nd = len(shape)
    return pl.BlockSpec(shape, lambda *_: (0,) * nd, pipeline_mode=pl.Buffered(1))


def _mix_call(x, g, w_in, lbl, hg, w_pool, ps, w_out):
    bsz, seq, d = x.shape
    ts = TS_MIX
    ns = seq // ts
    n_tiles = bsz * ns
    cm = jnp.asarray(_cumsum_matrix(), BF16)

    def tile_next(i):
        t = jnp.minimum(i, n_tiles - 1)
        return (t // ns, t % ns, 0)

    def tile_prev(i):
        t = jnp.maximum(i - 1, 0)
        return (t // ns, t % ns, 0)

    return pl.pallas_call(
        functools.partial(_mix_kernel, tiles_per_seq=ns),
        grid=(n_tiles + 1,),
        in_specs=[
            pl.BlockSpec((1, ts, d), tile_next),
            pl.BlockSpec((1, ts, d), tile_prev),
            _const_spec((1, d)),
            _const_spec((d, IN_WIDTH)),
            _const_spec((2, KW)),
            _const_spec((1, KW)),
            _const_spec((len(POOL_WINDOWS), DK, DK)),
            _const_spec((1, KW)),
            _const_spec((d, d)),
            _const_spec((CHUNK, CHUNK)),
        ],
        out_specs=pl.BlockSpec((1, ts, d), tile_prev),
        out_shape=jax.ShapeDtypeStruct(x.shape, F32),
        scratch_shapes=[
            pltpu.VMEM((ts, IN_WIDTH), F32),
            pltpu.VMEM((ts, IN_WIDTH), F32),
            pltpu.VMEM((HEADS, DK, DK), F32),
            pltpu.VMEM((POOL_HIST + SUB_TILE, KW), F32),
            pltpu.VMEM((ts, d), F32),
            pltpu.VMEM((d, IN_WIDTH), BF16),
            pltpu.VMEM((len(POOL_WINDOWS), DK, DK), BF16),
            pltpu.VMEM((d, d), BF16),
        ],
        compiler_params=pltpu.CompilerParams(
            dimension_semantics=("arbitrary",), vmem_limit_bytes=VMEM_LIMIT),
        name="mix",
    )(x, x, g, w_in, lbl, hg, w_pool, ps, w_out, cm)


def _memkv_kernel(mem_ref, g_ref, wq_ref, wk_ref, wv_ref, kt_ref, v_ref, wq_out_ref, wk_bf, wv_bf):
    def dense(w_ref):
        return jnp.concatenate([w_ref[:, h, :] for h in range(XHEADS)], axis=1).astype(BF16)

    @pl.when(pl.program_id(0) == 0)
    def _():
        wq_out_ref[...] = dense(wq_ref)
        wk_bf[...] = dense(wk_ref)
        wv_bf[...] = dense(wv_ref)

    hm = _rms(mem_ref[0], g_ref[...]).astype(BF16)
    k = _dot(hm, wk_bf[...]) * (XDIM ** -0.5)
    kt_ref[0] = k.T.astype(BF16)
    v_ref[0] = _dot(hm, wv_bf[...]).astype(BF16)


def _memkv_call(mem, g, wq, wk, wv):
    bsz, m, d = mem.shape
    return pl.pallas_call(
        _memkv_kernel,
        grid=(bsz,),
        in_specs=[
            pl.BlockSpec((1, m, d), lambda b: (b, 0, 0)),
            _const_spec((1, d)),
            _const_spec((d, XHEADS, XDIM)),
            _const_spec((d, XHEADS, XDIM)),
            _const_spec((d, XHEADS, XDIM)),
        ],
        out_specs=[
            pl.BlockSpec((1, d, m), lambda b: (b, 0, 0)),
            pl.BlockSpec((1, m, d), lambda b: (b, 0, 0)),
            pl.BlockSpec((d, d), lambda b: (0, 0)),
        ],
        out_shape=[
            jax.ShapeDtypeStruct((bsz, d, m), BF16),
            jax.ShapeDtypeStruct((bsz, m, d), BF16),
            jax.ShapeDtypeStruct((d, d), BF16),
        ],
        scratch_shapes=[pltpu.VMEM((d, d), BF16), pltpu.VMEM((d, d), BF16)],
        compiler_params=pltpu.CompilerParams(
            dimension_semantics=("arbitrary",), vmem_limit_bytes=VMEM_LIMIT),
        name="memkv",
    )(mem, g, wq, wk, wv)


def _tail_kernel(x_ref, gx_ref, wq_ref, kt_ref, v_ref, wo_ref, gf_ref, w1_ref, w2_ref, gl_ref, out_ref):
    hm = x_ref.shape[1] // 2
    head_cols = [slice(h * XDIM, (h + 1) * XDIM) for h in range(XHEADS)]
    rows = [slice(i * hm, (i + 1) * hm) for i in range(2)]
    xs = [x_ref[0, r, :] for r in rows]

    def q_proj(x):
        return _dot(_rms(x, gx_ref[...]).astype(BF16), wq_ref[...]).astype(BF16)

    def scores(xq):
        return [_dot(xq[:, c], kt_ref[0, c, :]) for c in head_cols]

    def attend(ss):
        outs = []
        for s, c in zip(ss, head_cols):
            e = jnp.exp(s - jnp.max(s, axis=-1, keepdims=True))
            p = e * (1.0 / jnp.sum(e, axis=-1, keepdims=True))
            outs.append(_dot(p.astype(BF16), v_ref[0, :, c]).astype(BF16))
        return jnp.concatenate(outs, axis=1)

    def o_proj(x, att):
        return x + _dot(att, wo_ref[...].astype(BF16))

    def mlp(x2s):
        hfs = [_rms(x, gf_ref[...]).astype(BF16) for x in x2s]
        accs = list(x2s)
        for j in range(D_FF // FF_BLOCK):
            cols = slice(j * FF_BLOCK, (j + 1) * FF_BLOCK)
            w1_j = w1_ref[:, cols].astype(BF16)
            w2_j = w2_ref[cols, :].astype(BF16)
            for i in range(2):
                u = jnp.maximum(_dot(hfs[i], w1_j), 0.0)
                accs[i] = accs[i] + _dot((u * u).astype(BF16), w2_j)
        for i in range(2):
            out_ref[0, rows[i], :] = _rms(accs[i], gl_ref[...])

    s_a = scores(q_proj(xs[0]))
    xq_b = q_proj(xs[1])
    att_a = attend(s_a)
    s_b = scores(xq_b)
    x2_a = o_proj(xs[0], att_a)
    x2_b = o_proj(xs[1], attend(s_b))
    mlp([x2_a, x2_b])


def _tail_call(x, gx, wq, kt, v, wo, gf, w1, w2, gl):
    bsz, seq, d = x.shape
    tm = TM_TAIL
    return pl.pallas_call(
        _tail_kernel,
        grid=(bsz, seq // tm),
        in_specs=[
            pl.BlockSpec((1, tm, d), lambda b, s: (b, s, 0)),
            _const_spec((1, d)),
            _const_spec((d, d)),
            pl.BlockSpec((1, d, MEM_LEN), lambda b, s: (b, 0, 0)),
            pl.BlockSpec((1, MEM_LEN, d), lambda b, s: (b, 0, 0)),
            _const_spec((d, d)),
            _const_spec((1, d)),
            _const_spec((d, D_FF)),
            _const_spec((D_FF, d)),
            _const_spec((1, d)),
        ],
        out_specs=pl.BlockSpec((1, tm, d), lambda b, s: (b, s, 0)),
        out_shape=jax.ShapeDtypeStruct(x.shape, F32),
        compiler_params=pltpu.CompilerParams(
            dimension_semantics=("arbitrary", "arbitrary"), vmem_limit_bytes=TAIL_VMEM_LIMIT),
        name="tail",
    )(x, gx, wq, kt, v, wo, gf, w1, w2, gl)


def kernel(x, mem, norm_mix_g, w_in, lb_logits, hgrn_norm_g, w_pool, pool_scale, w_out,
           norm_x_g, norm_mem_g, w_xq, w_xk, w_xv, w_xo, norm_ffn_g, w_ff1, w_ff2, final_norm_g):
    bsz, seq, d = x.shape
    depth = norm_mix_g.shape[0]
    assert depth == 1 and lb_logits.shape[0] == 2
    row = lambda a: a.reshape(1, -1).astype(F32)
    x = _mix_call(x, row(norm_mix_g), w_in.reshape(d, IN_WIDTH), lb_logits.astype(F32),
                  row(hgrn_norm_g), w_pool.reshape(len(POOL_WINDOWS), DK, DK), row(pool_scale),
                  w_out.reshape(d, d))
    split = (d, XHEADS, XDIM)
    kt, v, wq = _memkv_call(mem, row(norm_mem_g), w_xq.reshape(split), w_xk.reshape(split), w_xv.reshape(split))
    return _tail_call(x, row(norm_x_g), wq, kt, v, w_xo.reshape(d, d),
                      row(norm_ffn_g), w_ff1.reshape(d, D_FF), w_ff2.reshape(D_FF, d), row(final_norm_g))
```

```python
import functools

import jax
import jax.numpy as jnp
import numpy as np
from jax import lax
from jax.experimental import pallas as pl
from jax.experimental.pallas import tpu as pltpu

HEADS = 4
DK = 128
KW = HEADS * DK
PW = 2 * DK
CHUNK = 64
LEVELS = (32, 16, 8, 4, 2, 1)
POOL_WINDOWS = (2, 4, 8, 16)
POOL_HIST = 16
IN_WIDTH = 5 * KW
MEM_LEN = 256
XHEADS = 4
XDIM = 256
D_FF = 4096
EPS = 1e-6
LOG2E = 1.4426950408889634

TS_MIX = 512
SUB_TILE = 256
IN_BLOCK = 256
TM_TAIL = 512
FF_BLOCK = 1024
VMEM_LIMIT = 56 * 1024 * 1024
TAIL_VMEM_LIMIT = 60 * 1024 * 1024

F32 = jnp.float32
BF16 = jnp.bfloat16


def _rms(x, g):
    return x * lax.rsqrt(jnp.mean(x * x, axis=-1, keepdims=True) + EPS) * g


def _dot(a, b):
    return jnp.dot(a, b, preferred_element_type=F32)


def _dot_nt(a, b):
    return lax.dot_general(a, b, (((1,), (1,)), ((), ())), preferred_element_type=F32)


def _dot_tn(a, b):
    return lax.dot_general(a, b, (((0,), (0,)), ((), ())), preferred_element_type=F32)


def _sigmoid(x):
    return 1.0 / (1.0 + jnp.exp(-x))


def _block_diag(a, b):
    z = jnp.zeros_like(a)
    return jnp.concatenate([jnp.concatenate([a, z], axis=1), jnp.concatenate([z, b], axis=1)], axis=0)


def _cumsum_matrix():
    t = np.arange(CHUNK)[:, None]
    s = np.arange(CHUNK)[None, :]
    return (s <= t).astype(np.float32)


def _hgrn_tile(z_ref, ts, cm, lb_all, hg_all, st_ref, mixed_ref):
    n_chunks = ts // CHUNK
    qp = z_ref[:, 0:KW]
    fp = z_ref[:, KW:2 * KW]
    gp = z_ref[:, 3 * KW:4 * KW]

    f = lb_all + (1.0 - lb_all) * _sigmoid(fp)
    kk_all = 1.0 - f
    q_all = qp * _sigmoid(qp)
    gate_all = gp * _sigmoid(gp)

    lf2 = jnp.log(f) * LOG2E
    hi = lf2.astype(BF16)
    lo = (lf2 - hi.astype(F32)).astype(BF16)
    cm2 = jnp.concatenate([cm, cm], axis=1)
    b_chunks = []
    for ci in range(n_chunks):
        r = slice(ci * CHUNK, (ci + 1) * CHUNK)
        b_chunks.append(_dot(cm2, jnp.concatenate([hi[r], lo[r]], axis=0)))
    yield "prep"

    row = lax.broadcasted_iota(jnp.int32, (CHUNK, PW), 0)
    upper = {L: jnp.bitwise_and(row, 2 * L - 1) >= L for L in LEVELS}
    sign = {L: jnp.where(upper[L], 1.0, -1.0) for L in LEVELS[:-1]}
    out_row = lax.broadcasted_iota(jnp.int32, (CHUNK, DK), 0)
    out_col = jnp.bitwise_and(lax.broadcasted_iota(jnp.int32, (CHUNK, DK), 1), CHUNK - 1)
    same_block = {L: jnp.bitwise_and(jnp.bitwise_xor(out_row, out_col), -2 * L) == 0 for L in LEVELS[1:]}
    diagonal = out_row == out_col

    def ref_rows(a, first, period, width):
        parts = [jnp.broadcast_to(a[r:r + 1, :], (width, a.shape[1])) for r in range(first, CHUNK, period)]
        return parts[0] if len(parts) == 1 else jnp.concatenate(parts, axis=0)

    def level_ref(b, L):
        if L >= 4:
            return ref_rows(b, L - 1, 2 * L, 2 * L)
        return jnp.where(jnp.bitwise_and(row, 7) < 4, ref_rows(b, 1, 8, 8), ref_rows(b, 5, 8, 8))

    def pair_dot(qe, ke):
        rhs = _block_diag(ke[:, :DK], ke[:, DK:]).T.astype(BF16)
        return _dot(qe.astype(BF16), rhs)

    def value_dot(entry):
        level_scores, v_bf = entry[5], entry[6]
        scores = jnp.where(diagonal, level_scores[-1], 0.0)
        for L, s_l in zip(LEVELS[1:], level_scores[1:-1]):
            scores = scores + jnp.where(same_block[L], s_l, 0.0)
        scores = scores + level_scores[0]
        return _dot(scores.astype(BF16), _block_diag(v_bf[:, :DK], v_bf[:, DK:]))

    work = []
    o_intra = []
    for ci in range(n_chunks):
        r = slice(ci * CHUNK, (ci + 1) * CHUNK)
        for hp in range(HEADS // 2):
            pc = slice(hp * PW, (hp + 1) * PW)
            q, kk, b = q_all[r, pc], kk_all[r, pc], b_chunks[ci][:, pc]
            b_last = b[CHUNK - 1:CHUNK, :]
            v_bf = z_ref[r, 2 * KW + hp * PW:2 * KW + (hp + 1) * PW].astype(BF16)

            qe_bf = (q * jnp.exp2(b)).astype(BF16)
            k_dec = (kk * jnp.exp2(b_last - b)).astype(BF16)
            upd = [_dot_tn(v_bf[:, i * DK:(i + 1) * DK], k_dec[:, i * DK:(i + 1) * DK]) for i in range(2)]

            level_scores = []
            for L in LEVELS:
                if L == 1:
                    x = jnp.where(upper[L], q * f[r, pc], kk)
                else:
                    x = jnp.where(upper[L], q, kk) * jnp.exp2((b - level_ref(b, L)) * sign[L])
                level_scores.append(pair_dot(jnp.where(upper[L], x, 0.0), jnp.where(upper[L], 0.0, x)))
            level_scores.append(pair_dot(q, kk))
            work.append((ci, hp, qe_bf, jnp.exp2(b_last), upd, level_scores, v_bf))
            if len(work) > 1:
                o_intra.append(value_dot(work[-2]))
            yield "levels"
    o_intra.append(value_dot(work[-1]))
    yield "levels"

    state = [st_ref[h] for h in range(HEADS)]
    o_inter = []
    for ci, hp, qe_bf, decay, upd, level_scores, v_bf in work:
        h0, h1 = 2 * hp, 2 * hp + 1
        o_inter.append(_dot_nt(qe_bf, _block_diag(state[h0].astype(BF16), state[h1].astype(BF16))))
        state[h0] = state[h0] * decay[:, :DK] + upd[0]
        state[h1] = state[h1] * decay[:, DK:] + upd[1]
    for h in range(HEADS):
        st_ref[h] = state[h]
    yield "state"

    for (ci, hp, *_), oa, ob in zip(work, o_inter, o_intra):
        r = slice(ci * CHUNK, (ci + 1) * CHUNK)
        o_pair = ob + oa
        for i in range(2):
            cols = slice((2 * hp + i) * DK, (2 * hp + i + 1) * DK)
            o = o_pair[:, i * DK:(i + 1) * DK]
            o = o * lax.rsqrt(jnp.mean(o * o, axis=-1, keepdims=True) + EPS) * hg_all[:, cols]
            mixed_ref[r, cols] = o * gate_all[r, cols]
        yield "norm"


def _mix_kernel(xn_ref, xp_ref, g_ref, w_in_f32, lbl_ref, hg_ref, w_pool_f32, ps_ref, w_out_f32, cm_ref,
                out_ref, za_ref, zb_ref, st_ref, pext_ref, mixed_ref, w_in_ref, w_pool_ref, w_out_ref,
                *, tiles_per_seq):
    g = pl.program_id(0)
    si = lax.rem(g + tiles_per_seq - 1, tiles_per_seq)

    @pl.when(g == 0)
    def _():
        w_in_ref[...] = w_in_f32[...].astype(BF16)
        w_pool_ref[...] = w_pool_f32[...].astype(BF16)
        w_out_ref[...] = w_out_f32[...].astype(BF16)
        za_ref[...] = _dot(_rms(xn_ref[0], g_ref[...]).astype(BF16), w_in_ref[...])

    @pl.when(si == 0)
    def _():
        st_ref[...] = jnp.zeros_like(st_ref)
        pext_ref[0:POOL_HIST, :] = jnp.zeros((POOL_HIST, KW), F32)

    args = (xn_ref, xp_ref, g_ref, w_in_ref, lbl_ref, hg_ref, w_pool_ref, ps_ref, w_out_ref, cm_ref,
            out_ref, st_ref, pext_ref, mixed_ref, si)

    @pl.when(jnp.logical_and(lax.rem(g, 2) == 0, g > 0))
    def _():
        _mix_step(*args, zn_ref=za_ref, zc_ref=zb_ref)

    @pl.when(lax.rem(g, 2) == 1)
    def _():
        _mix_step(*args, zn_ref=zb_ref, zc_ref=za_ref)


def _mix_step(xn_ref, xp_ref, g_ref, w_in_ref, lbl_ref, hg_ref, w_pool_ref, ps_ref, w_out_ref, cm_ref,
              out_ref, st_ref, pext_ref, mixed_ref, si, *, zn_ref, zc_ref):
    ts = xn_ref.shape[1]
    d = xn_ref.shape[2]
    n_sub = ts // SUB_TILE
    hn_bf = _rms(xn_ref[0], g_ref[...]).astype(BF16)

    def in_proj_unit(blk):
        def run():
            cols = slice(blk * IN_BLOCK, (blk + 1) * IN_BLOCK)
            zn_ref[:, cols] = _dot(hn_bf, w_in_ref[:, cols])
        return run

    def out_proj_unit(sub, blk):
        def run():
            r = slice(sub * SUB_TILE, (sub + 1) * SUB_TILE)
            cols = slice(blk * IN_BLOCK, (blk + 1) * IN_BLOCK)
            out_ref[0, r, cols] = xp_ref[0, r, cols] + _dot(mixed_ref[r, :].astype(BF16), w_out_ref[:, cols])
        return run

    in_units = [in_proj_unit(blk) for blk in range(IN_WIDTH // IN_BLOCK)]

    l0 = lbl_ref[0:1, :]
    l1 = lbl_ref[1:2, :]
    lmax = jnp.maximum(l0, l1)
    e0 = jnp.exp(l0 - lmax)
    lb_all = e0 / (e0 + jnp.exp(l1 - lmax))

    def pool(sub):
        r = slice(sub * SUB_TILE, (sub + 1) * SUB_TILE)
        p = zc_ref[r, 4 * KW:5 * KW]
        pext_ref[POOL_HIST:, :] = p
        pos = (si * ts + sub * SUB_TILE + 1 + lax.broadcasted_iota(jnp.int32, (SUB_TILE, 1), 0)).astype(F32)
        for gi, w in enumerate(POOL_WINDOWS):
            cols = slice(gi * DK, (gi + 1) * DK)
            acc = pext_ref[:, cols]
            sh = 1
            while sh < w:
                acc = acc + pltpu.roll(acc, sh, 0)
                sh *= 2
            win = acc[POOL_HIST:, :]
            pooled = win / jnp.minimum(pos, float(w)) - p[:, cols]
            y = _dot(pooled.astype(BF16), w_pool_ref[gi]) * ps_ref[:, cols]
            mixed_ref[r, KW + gi * DK:KW + (gi + 1) * DK] = y
        pext_ref[0:POOL_HIST, :] = pext_ref[SUB_TILE:SUB_TILE + POOL_HIST, :]

    def hgrn(sub):
        r = pl.ds(sub * SUB_TILE, SUB_TILE)
        return _hgrn_tile(zc_ref.at[r], SUB_TILE, cm_ref[...], lb_all, hg_ref[...], st_ref, mixed_ref.at[r])

    def step(gen, label):
        got = next(gen)
        assert got == label, (got, label)

    def fill(n=1):
        for _ in range(n):
            if in_units:
                in_units.pop(0)()

    n_items = (SUB_TILE // CHUNK) * (HEADS // 2)
    gens = [hgrn(sub) for sub in range(n_sub)]
    slot_every = max(1, n_items // 4)
    out_units = []
    step(gens[0], "prep")
    fill(2)
    for i in range(n_items + 1):
        step(gens[0], "levels")
        if i % slot_every == slot_every - 1:
            fill()
    for sub in range(n_sub):
        cur = gens[sub]
        nxt = gens[sub + 1] if sub + 1 < n_sub else None
        if nxt is not None:
            step(nxt, "prep")
        step(cur, "state")
        for i in range(n_items):
            if nxt is not None:
                step(nxt, "levels")
            step(cur, "norm")
            if i % slot_every == slot_every - 1:
                (out_units.pop(0) if out_units else fill)()
        if nxt is not None:
            step(nxt, "levels")
        pool(sub)
        out_units += [out_proj_unit(sub, blk) for blk in range(d // IN_BLOCK)]
    fill(len(in_units))
    for unit in out_units:
        unit()


def _const_spec(shape):
    nd = len(shape)
    return pl.BlockSpec(shape, lambda *_: (0,) * nd, pipeline_mode=pl.Buffered(1))


def _mix_call(x, g, w_in, lbl, hg, w_pool, ps, w_out):
    bsz, seq, d = x.shape
    ts = TS_MIX
    ns = seq // ts
    n_tiles = bsz * ns
    cm = jnp.asarray(_cumsum_matrix(), BF16)

    def tile_next(i):
        t = jnp.minimum(i, n_tiles - 1)
        return (t // ns, t % ns, 0)

    def tile_prev(i):
        t = jnp.maximum(i - 1, 0)
        return (t // ns, t % ns, 0)

    return pl.pallas_call(
        functools.partial(_mix_kernel, tiles_per_seq=ns),
        grid=(n_tiles + 1,),
        in_specs=[
            pl.BlockSpec((1, ts, d), tile_next),
            pl.BlockSpec((1, ts, d), tile_prev),
            _const_spec((1, d)),
            _const_spec((d, IN_WIDTH)),
            _const_spec((2, KW)),
            _const_spec((1, KW)),
            _const_spec((len(POOL_WINDOWS), DK, DK)),
            _const_spec((1, KW)),
            _const_spec((d, d)),
            _const_spec((CHUNK, CHUNK)),
        ],
        out_specs=pl.BlockSpec((1, ts, d), tile_prev),
        out_shape=jax.ShapeDtypeStruct(x.shape, F32),
        scratch_shapes=[
            pltpu.VMEM((ts, IN_WIDTH), F32),
            pltpu.VMEM((ts, IN_WIDTH), F32),
            pltpu.VMEM((HEADS, DK, DK), F32),
            pltpu.VMEM((POOL_HIST + SUB_TILE, KW), F32),
            pltpu.VMEM((ts, d), F32),
            pltpu.VMEM((d, IN_WIDTH), BF16),
            pltpu.VMEM((len(POOL_WINDOWS), DK, DK), BF16),
            pltpu.VMEM((d, d), BF16),
        ],
        compiler_params=pltpu.CompilerParams(
            dimension_semantics=("arbitrary",), vmem_limit_bytes=VMEM_LIMIT),
        name="mix",
    )(x, x, g, w_in, lbl, hg, w_pool, ps, w_out, cm)


def _memkv_kernel(mem_ref, g_ref, wq_ref, wk_ref, wv_ref, kt_ref, v_ref, wq_out_ref, wk_bf, wv_bf):
    def dense(w_ref):
        return jnp.concatenate([w_ref[:, h, :] for h in range(XHEADS)], axis=1).astype(BF16)

    @pl.when(pl.program_id(0) == 0)
    def _():
        wq_out_ref[...] = dense(wq_ref)
        wk_bf[...] = dense(wk_ref)
        wv_bf[...] = dense(wv_ref)

    hm = _rms(mem_ref[0], g_ref[...]).astype(BF16)
    k = _dot(hm, wk_bf[...]) * (XDIM ** -0.5)
    kt_ref[0] = k.T.astype(BF16)
    v_ref[0] = _dot(hm, wv_bf[...]).astype(BF16)


def _memkv_call(mem, g, wq, wk, wv):
    bsz, m, d = mem.shape
    return pl.pallas_call(
        _memkv_kernel,
        grid=(bsz,),
        in_specs=[
            pl.BlockSpec((1, m, d), lambda b: (b, 0, 0)),
            _const_spec((1, d)),
            _const_spec((d, XHEADS, XDIM)),
            _const_spec((d, XHEADS, XDIM)),
            _const_spec((d, XHEADS, XDIM)),
        ],
        out_specs=[
            pl.BlockSpec((1, d, m), lambda b: (b, 0, 0)),
            pl.BlockSpec((1, m, d), lambda b: (b, 0, 0)),
            pl.BlockSpec((d, d), lambda b: (0, 0)),
        ],
        out_shape=[
            jax.ShapeDtypeStruct((bsz, d, m), BF16),
            jax.ShapeDtypeStruct((bsz, m, d), BF16),
            jax.ShapeDtypeStruct((d, d), BF16),
        ],
        scratch_shapes=[pltpu.VMEM((d, d), BF16), pltpu.VMEM((d, d), BF16)],
        compiler_params=pltpu.CompilerParams(
            dimension_semantics=("arbitrary",), vmem_limit_bytes=VMEM_LIMIT),
        name="memkv",
    )(mem, g, wq, wk, wv)


def _tail_kernel(x_ref, gx_ref, wq_ref, kt_ref, v_ref, wo_ref, gf_ref, w1_ref, w2_ref, gl_ref, out_ref):
    hm = x_ref.shape[1] // 2
    head_cols = [slice(h * XDIM, (h + 1) * XDIM) for h in range(XHEADS)]
    rows = [slice(i * hm, (i + 1) * hm) for i in range(2)]
    xs = [x_ref[0, r, :] for r in rows]

    def q_proj(x):
        return _dot(_rms(x, gx_ref[...]).astype(BF16), wq_ref[...]).astype(BF16)

    def scores(xq):
        return [_dot(xq[:, c], kt_ref[0, c, :]) for c in head_cols]

    def attend(ss):
        outs = []
        for s, c in zip(ss, head_cols):
            e = jnp.exp(s - jnp.max(s, axis=-1, keepdims=True))
            p = e * (1.0 / jnp.sum(e, axis=-1, keepdims=True))
            outs.append(_dot(p.astype(BF16), v_ref[0, :, c]).astype(BF16))
        return jnp.concatenate(outs, axis=1)

    def o_proj(x, att):
        return x + _dot(att, wo_ref[...].astype(BF16))

    def mlp(i, x):
        hf = _rms(x, gf_ref[...]).astype(BF16)
        acc = x
        for j in range(D_FF // FF_BLOCK):
            cols = slice(j * FF_BLOCK, (j + 1) * FF_BLOCK)
            u = jnp.maximum(_dot(hf, w1_ref[:, cols].astype(BF16)), 0.0)
            acc = acc + _dot((u * u).astype(BF16), w2_ref[cols, :].astype(BF16))
        out_ref[0, rows[i], :] = _rms(acc, gl_ref[...])

    s_a = scores(q_proj(xs[0]))
    xq_b = q_proj(xs[1])
    att_a = attend(s_a)
    s_b = scores(xq_b)
    x2_a = o_proj(xs[0], att_a)
    x2_b = o_proj(xs[1], attend(s_b))
    mlp(0, x2_a)
    mlp(1, x2_b)


def _tail_call(x, gx, wq, kt, v, wo, gf, w1, w2, gl):
    bsz, seq, d = x.shape
    tm = TM_TAIL
    return pl.pallas_call(
        _tail_kernel,
        grid=(bsz, seq // tm),
        in_specs=[
            pl.BlockSpec((1, tm, d), lambda b, s: (b, s, 0)),
            _const_spec((1, d)),
            _const_spec((d, d)),
            pl.BlockSpec((1, d, MEM_LEN), lambda b, s: (b, 0, 0)),
            pl.BlockSpec((1, MEM_LEN, d), lambda b, s: (b, 0, 0)),
            _const_spec((d, d)),
            _const_spec((1, d)),
            _const_spec((d, D_FF)),
            _const_spec((D_FF, d)),
            _const_spec((1, d)),
        ],
        out_specs=pl.BlockSpec((1, tm, d), lambda b, s: (b, s, 0)),
        out_shape=jax.ShapeDtypeStruct(x.shape, F32),
        compiler_params=pltpu.CompilerParams(
            dimension_semantics=("arbitrary", "arbitrary"), vmem_limit_bytes=TAIL_VMEM_LIMIT),
        name="tail",
    )(x, gx, wq, kt, v, wo, gf, w1, w2, gl)


def kernel(x, mem, norm_mix_g, w_in, lb_logits, hgrn_norm_g, w_pool, pool_scale, w_out,
           norm_x_g, norm_mem_g, w_xq, w_xk, w_xv, w_xo, norm_ffn_g, w_ff1, w_ff2, final_norm_g):
    bsz, seq, d = x.shape
    depth = norm_mix_g.shape[0]
    assert depth == 1 and lb_logits.shape[0] == 2
    row = lambda a: a.reshape(1, -1).astype(F32)
    x = _mix_call(x, row(norm_mix_g), w_in.reshape(d, IN_WIDTH), lb_logits.astype(F32),
                  row(hgrn_norm_g), w_pool.reshape(len(POOL_WINDOWS), DK, DK), row(pool_scale),
                  w_out.reshape(d, d))
    split = (d, XHEADS, XDIM)
    kt, v, wq = _memkv_call(mem, row(norm_mem_g), w_xq.reshape(split), w_xk.reshape(split), w_xv.reshape(split))
    return _tail_call(x, row(norm_x_g), wq, kt, v, w_xo.reshape(d, d),
                      row(norm_ffn_g), w_ff1.reshape(d, D_FF), w_ff2.reshape(D_FF, d), row(final_norm_g))
```

```python
import functools

import jax
import jax.numpy as jnp
import numpy as np
from jax import lax
from jax.experimental import pallas as pl
from jax.experimental.pallas import tpu as pltpu

HEADS = 4
DK = 128
KW = HEADS * DK
PW = 2 * DK
CHUNK = 64
LEVELS = (32, 16, 8, 4, 2, 1)
POOL_WINDOWS = (2, 4, 8, 16)
POOL_HIST = 16
IN_WIDTH = 5 * KW
MEM_LEN = 256
XHEADS = 4
XDIM = 256
D_FF = 4096
EPS = 1e-6
LOG2E = 1.4426950408889634

TS_MIX = 512
SUB_TILE = 256
IN_BLOCK = 256
TM_TAIL = 512
FF_BLOCK = 1024
VMEM_LIMIT = 56 * 1024 * 1024
TAIL_VMEM_LIMIT = 60 * 1024 * 1024

F32 = jnp.float32
BF16 = jnp.bfloat16


def _rms(x, g):
    return x * lax.rsqrt(jnp.mean(x * x, axis=-1, keepdims=True) + EPS) * g


def _dot(a, b):
    return jnp.dot(a, b, preferred_element_type=F32)


def _dot_nt(a, b):
    return lax.dot_general(a, b, (((1,), (1,)), ((), ())), preferred_element_type=F32)


def _dot_tn(a, b):
    return lax.dot_general(a, b, (((0,), (0,)), ((), ())), preferred_element_type=F32)


def _sigmoid(x):
    return 1.0 / (1.0 + jnp.exp(-x))


def _block_diag(a, b):
    z = jnp.zeros_like(a)
    return jnp.concatenate([jnp.concatenate([a, z], axis=1), jnp.concatenate([z, b], axis=1)], axis=0)


def _cumsum_matrix():
    t = np.arange(CHUNK)[:, None]
    s = np.arange(CHUNK)[None, :]
    return (s <= t).astype(np.float32)


def _hgrn_tile(z_ref, ts, cm, lb_all, hg_all, st_ref, mixed_ref):
    n_chunks = ts // CHUNK
    qp = z_ref[:, 0:KW]
    fp = z_ref[:, KW:2 * KW]
    gp = z_ref[:, 3 * KW:4 * KW]

    f = lb_all + (1.0 - lb_all) * _sigmoid(fp)
    kk_all = 1.0 - f
    q_all = qp * _sigmoid(qp)
    gate_all = gp * _sigmoid(gp)

    lf2 = jnp.log(f) * LOG2E
    hi = lf2.astype(BF16)
    lo = (lf2 - hi.astype(F32)).astype(BF16)
    cm2 = jnp.concatenate([cm, cm], axis=1)
    b_chunks = []
    for ci in range(n_chunks):
        r = slice(ci * CHUNK, (ci + 1) * CHUNK)
        b_chunks.append(_dot(cm2, jnp.concatenate([hi[r], lo[r]], axis=0)))
    yield "prep"

    row = lax.broadcasted_iota(jnp.int32, (CHUNK, PW), 0)
    upper = {L: jnp.bitwise_and(row, 2 * L - 1) >= L for L in LEVELS}
    sign = {L: jnp.where(upper[L], 1.0, -1.0) for L in LEVELS if 1 < L < 8}
    up_f = {L: jnp.where(upper[L], 1.0, 0.0) for L in LEVELS if L < 8}
    lo_f = {L: jnp.where(upper[L], 0.0, 1.0) for L in LEVELS if L < 8}
    out_row = lax.broadcasted_iota(jnp.int32, (CHUNK, DK), 0)
    out_col = jnp.bitwise_and(lax.broadcasted_iota(jnp.int32, (CHUNK, DK), 1), CHUNK - 1)
    same_block = {L: jnp.where(jnp.bitwise_and(jnp.bitwise_xor(out_row, out_col), -2 * L) == 0, 1.0, 0.0)
                  for L in LEVELS[1:]}
    diagonal = jnp.where(out_row == out_col, 1.0, 0.0)

    def ref_rows(a, first, period, width):
        parts = [jnp.broadcast_to(a[r:r + 1, :], (width, a.shape[1])) for r in range(first, CHUNK, period)]
        return parts[0] if len(parts) == 1 else jnp.concatenate(parts, axis=0)

    def level_ref(b, L):
        if L >= 4:
            return ref_rows(b, L - 1, 2 * L, 2 * L)
        return jnp.where(jnp.bitwise_and(row, 7) < 4, ref_rows(b, 1, 8, 8), ref_rows(b, 5, 8, 8))

    def pair_dot(qe, ke):
        rhs = _block_diag(ke[:, :DK], ke[:, DK:]).T.astype(BF16)
        return _dot(qe.astype(BF16), rhs)

    def value_dot(entry):
        level_scores, v_bf = entry[5], entry[6]
        scores = level_scores[-1] * diagonal
        for L, s_l in zip(LEVELS[1:], level_scores[1:-1]):
            scores = scores + s_l * same_block[L]
        scores = scores + level_scores[0]
        return _dot(scores.astype(BF16), _block_diag(v_bf[:, :DK], v_bf[:, DK:]))

    work = []
    o_intra = []
    for ci in range(n_chunks):
        r = slice(ci * CHUNK, (ci + 1) * CHUNK)
        for hp in range(HEADS // 2):
            pc = slice(hp * PW, (hp + 1) * PW)
            q, kk, b = q_all[r, pc], kk_all[r, pc], b_chunks[ci][:, pc]
            b_last = b[CHUNK - 1:CHUNK, :]
            v_bf = z_ref[r, 2 * KW + hp * PW:2 * KW + (hp + 1) * PW].astype(BF16)

            qe_bf = (q * jnp.exp2(b)).astype(BF16)
            k_dec = (kk * jnp.exp2(b_last - b)).astype(BF16)
            upd = [_dot_tn(v_bf[:, i * DK:(i + 1) * DK], k_dec[:, i * DK:(i + 1) * DK]) for i in range(2)]

            level_scores = []
            for L in LEVELS:
                if L >= 8:
                    ref = level_ref(b, L)
                    zeros = jnp.zeros((L, PW), F32)
                    q_runs, k_runs = [], []
                    for r0 in range(0, CHUNK, L):
                        run = slice(r0, r0 + L)
                        if (r0 // L) % 2:
                            q_runs.append(q[run] * jnp.exp2(b[run] - ref[run]))
                            k_runs.append(zeros)
                        else:
                            q_runs.append(zeros)
                            k_runs.append(kk[run] * jnp.exp2(ref[run] - b[run]))
                    level_scores.append(pair_dot(jnp.concatenate(q_runs, axis=0),
                                                 jnp.concatenate(k_runs, axis=0)))
                    continue
                if L == 1:
                    qe, ke = q * f[r, pc] * up_f[L], kk * lo_f[L]
                else:
                    e = jnp.exp2((b - level_ref(b, L)) * sign[L])
                    qe, ke = q * up_f[L] * e, kk * lo_f[L] * e
                level_scores.append(pair_dot(qe, ke))
            level_scores.append(pair_dot(q, kk))
            work.append((ci, hp, qe_bf, jnp.exp2(b_last), upd, level_scores, v_bf))
            if len(work) > 1:
                o_intra.append(value_dot(work[-2]))
            yield "levels"
    o_intra.append(value_dot(work[-1]))
    yield "levels"

    state = [st_ref[h] for h in range(HEADS)]
    o_inter = []
    for ci, hp, qe_bf, decay, upd, level_scores, v_bf in work:
        h0, h1 = 2 * hp, 2 * hp + 1
        o_inter.append(_dot_nt(qe_bf, _block_diag(state[h0].astype(BF16), state[h1].astype(BF16))))
        state[h0] = state[h0] * decay[:, :DK] + upd[0]
        state[h1] = state[h1] * decay[:, DK:] + upd[1]
    for h in range(HEADS):
        st_ref[h] = state[h]
    yield "state"

    for (ci, hp, *_), oa, ob in zip(work, o_inter, o_intra):
        r = slice(ci * CHUNK, (ci + 1) * CHUNK)
        o_pair = ob + oa
        for i in range(2):
            cols = slice((2 * hp + i) * DK, (2 * hp + i + 1) * DK)
            o = o_pair[:, i * DK:(i + 1) * DK]
            o = o * lax.rsqrt(jnp.mean(o * o, axis=-1, keepdims=True) + EPS) * hg_all[:, cols]
            mixed_ref[r, cols] = o * gate_all[r, cols]
        yield "norm"


def _mix_kernel(xn_ref, xp_ref, g_ref, w_in_f32, lbl_ref, hg_ref, w_pool_f32, ps_ref, w_out_f32, cm_ref,
                out_ref, za_ref, zb_ref, st_ref, pext_ref, mixed_ref, w_in_ref, w_pool_ref, w_out_ref,
                *, tiles_per_seq):
    g = pl.program_id(0)
    si = lax.rem(g + tiles_per_seq - 1, tiles_per_seq)

    @pl.when(g == 0)
    def _():
        w_in_ref[...] = w_in_f32[...].astype(BF16)
        w_pool_ref[...] = w_pool_f32[...].astype(BF16)
        w_out_ref[...] = w_out_f32[...].astype(BF16)
        za_ref[...] = _dot(_rms(xn_ref[0], g_ref[...]).astype(BF16), w_in_ref[...])

    @pl.when(si == 0)
    def _():
        st_ref[...] = jnp.zeros_like(st_ref)
        pext_ref[0:POOL_HIST, :] = jnp.zeros((POOL_HIST, KW), F32)

    args = (xn_ref, xp_ref, g_ref, w_in_ref, lbl_ref, hg_ref, w_pool_ref, ps_ref, w_out_ref, cm_ref,
            out_ref, st_ref, pext_ref, mixed_ref, si)

    @pl.when(jnp.logical_and(lax.rem(g, 2) == 0, g > 0))
    def _():
        _mix_step(*args, zn_ref=za_ref, zc_ref=zb_ref)

    @pl.when(lax.rem(g, 2) == 1)
    def _():
        _mix_step(*args, zn_ref=zb_ref, zc_ref=za_ref)


def _mix_step(xn_ref, xp_ref, g_ref, w_in_ref, lbl_ref, hg_ref, w_pool_ref, ps_ref, w_out_ref, cm_ref,
              out_ref, st_ref, pext_ref, mixed_ref, si, *, zn_ref, zc_ref):
    ts = xn_ref.shape[1]
    d = xn_ref.shape[2]
    n_sub = ts // SUB_TILE
    hn_bf = _rms(xn_ref[0], g_ref[...]).astype(BF16)

    def in_proj_unit(blk):
        def run():
            cols = slice(blk * IN_BLOCK, (blk + 1) * IN_BLOCK)
            zn_ref[:, cols] = _dot(hn_bf, w_in_ref[:, cols])
        return run

    def out_proj_unit(sub, blk):
        def run():
            r = slice(sub * SUB_TILE, (sub + 1) * SUB_TILE)
            cols = slice(blk * IN_BLOCK, (blk + 1) * IN_BLOCK)
            out_ref[0, r, cols] = xp_ref[0, r, cols] + _dot(mixed_ref[r, :].astype(BF16), w_out_ref[:, cols])
        return run

    in_units = [in_proj_unit(blk) for blk in range(IN_WIDTH // IN_BLOCK)]

    l0 = lbl_ref[0:1, :]
    l1 = lbl_ref[1:2, :]
    lmax = jnp.maximum(l0, l1)
    e0 = jnp.exp(l0 - lmax)
    lb_all = e0 / (e0 + jnp.exp(l1 - lmax))

    def pool(sub):
        r = slice(sub * SUB_TILE, (sub + 1) * SUB_TILE)
        p = zc_ref[r, 4 * KW:5 * KW]
        pext_ref[POOL_HIST:, :] = p
        pos = (si * ts + sub * SUB_TILE + 1 + lax.broadcasted_iota(jnp.int32, (SUB_TILE, 1), 0)).astype(F32)
        for gi, w in enumerate(POOL_WINDOWS):
            cols = slice(gi * DK, (gi + 1) * DK)
            acc = pext_ref[:, cols]
            sh = 1
            while sh < w:
                acc = acc + pltpu.roll(acc, sh, 0)
                sh *= 2
            win = acc[POOL_HIST:, :]
            pooled = win / jnp.minimum(pos, float(w)) - p[:, cols]
            y = _dot(pooled.astype(BF16), w_pool_ref[gi]) * ps_ref[:, cols]
            mixed_ref[r, KW + gi * DK:KW + (gi + 1) * DK] = y
        pext_ref[0:POOL_HIST, :] = pext_ref[SUB_TILE:SUB_TILE + POOL_HIST, :]

    def hgrn(sub):
        r = pl.ds(sub * SUB_TILE, SUB_TILE)
        return _hgrn_tile(zc_ref.at[r], SUB_TILE, cm_ref[...], lb_all, hg_ref[...], st_ref, mixed_ref.at[r])

    def step(gen, label):
        got = next(gen)
        assert got == label, (got, label)

    def fill(n=1):
        for _ in range(n):
            if in_units:
                in_units.pop(0)()

    n_items = (SUB_TILE // CHUNK) * (HEADS // 2)
    gens = [hgrn(sub) for sub in range(n_sub)]
    slot_every = max(1, n_items // 4)
    out_units = []
    step(gens[0], "prep")
    fill(2)
    for i in range(n_items + 1):
        step(gens[0], "levels")
        if i % slot_every == slot_every - 1:
            fill()
    for sub in range(n_sub):
        cur = gens[sub]
        nxt = gens[sub + 1] if sub + 1 < n_sub else None
        if nxt is not None:
            step(nxt, "prep")
        step(cur, "state")
        for i in range(n_items):
            if nxt is not None:
                step(nxt, "levels")
            step(cur, "norm")
            if i % slot_every == slot_every - 1:
                (out_units.pop(0) if out_units else fill)()
        if nxt is not None:
            step(nxt, "levels")
        pool(sub)
        out_units += [out_proj_unit(sub, blk) for blk in range(d // IN_BLOCK)]
    fill(len(in_units))
    for unit in out_units:
        unit()


def _const_spec(shape):
    nd = len(shape)
    return pl.BlockSpec(shape, lambda *_: (0,) * nd, pipeline_mode=pl.Buffered(1))


def _mix_call(x, g, w_in, lbl, hg, w_pool, ps, w_out):
    bsz, seq, d = x.shape
    ts = TS_MIX
    ns = seq // ts
    n_tiles = bsz * ns
    cm = jnp.asarray(_cumsum_matrix(), BF16)

    def tile_next(i):
        t = jnp.minimum(i, n_tiles - 1)
        return (t // ns, t % ns, 0)

    def tile_prev(i):
        t = jnp.maximum(i - 1, 0)
        return (t // ns, t % ns, 0)

    return pl.pallas_call(
        functools.partial(_mix_kernel, tiles_per_seq=ns),
        grid=(n_tiles + 1,),
        in_specs=[
            pl.BlockSpec((1, ts, d), tile_next),
            pl.BlockSpec((1, ts, d), tile_prev),
            _const_spec((1, d)),
            _const_spec((d, IN_WIDTH)),
            _const_spec((2, KW)),
            _const_spec((1, KW)),
            _const_spec((len(POOL_WINDOWS), DK, DK)),
            _const_spec((1, KW)),
            _const_spec((d, d)),
            _const_spec((CHUNK, CHUNK)),
        ],
        out_specs=pl.BlockSpec((1, ts, d), tile_prev),
        out_shape=jax.ShapeDtypeStruct(x.shape, F32),
        scratch_shapes=[
            pltpu.VMEM((ts, IN_WIDTH), F32),
            pltpu.VMEM((ts, IN_WIDTH), F32),
            pltpu.VMEM((HEADS, DK, DK), F32),
            pltpu.VMEM((POOL_HIST + SUB_TILE, KW), F32),
            pltpu.VMEM((ts, d), F32),
            pltpu.VMEM((d, IN_WIDTH), BF16),
            pltpu.VMEM((len(POOL_WINDOWS), DK, DK), BF16),
            pltpu.VMEM((d, d), BF16),
        ],
        compiler_params=pltpu.CompilerParams(
            dimension_semantics=("arbitrary",), vmem_limit_bytes=VMEM_LIMIT),
        name="mix",
    )(x, x, g, w_in, lbl, hg, w_pool, ps, w_out, cm)


def _memkv_kernel(mem_ref, g_ref, wq_ref, wk_ref, wv_ref, kt_ref, v_ref, wq_out_ref, wk_bf, wv_bf):
    def dense(w_ref):
        return jnp.concatenate([w_ref[:, h, :] for h in range(XHEADS)], axis=1).astype(BF16)

    @pl.when(pl.program_id(0) == 0)
    def _():
        wq_out_ref[...] = dense(wq_ref)
        wk_bf[...] = dense(wk_ref)
        wv_bf[...] = dense(wv_ref)

    hm = _rms(mem_ref[0], g_ref[...]).astype(BF16)
    k = _dot(hm, wk_bf[...]) * (XDIM ** -0.5)
    kt_ref[0] = k.T.astype(BF16)
    v_ref[0] = _dot(hm, wv_bf[...]).astype(BF16)


def _memkv_call(mem, g, wq, wk, wv):
    bsz, m, d = mem.shape
    return pl.pallas_call(
        _memkv_kernel,
        grid=(bsz,),
        in_specs=[
            pl.BlockSpec((1, m, d), lambda b: (b, 0, 0)),
            _const_spec((1, d)),
            _const_spec((d, XHEADS, XDIM)),
            _const_spec((d, XHEADS, XDIM)),
            _const_spec((d, XHEADS, XDIM)),
        ],
        out_specs=[
            pl.BlockSpec((1, d, m), lambda b: (b, 0, 0)),
            pl.BlockSpec((1, m, d), lambda b: (b, 0, 0)),
            pl.BlockSpec((d, d), lambda b: (0, 0)),
        ],
        out_shape=[
            jax.ShapeDtypeStruct((bsz, d, m), BF16),
            jax.ShapeDtypeStruct((bsz, m, d), BF16),
            jax.ShapeDtypeStruct((d, d), BF16),
        ],
        scratch_shapes=[pltpu.VMEM((d, d), BF16), pltpu.VMEM((d, d), BF16)],
        compiler_params=pltpu.CompilerParams(
            dimension_semantics=("arbitrary",), vmem_limit_bytes=VMEM_LIMIT),
        name="memkv",
    )(mem, g, wq, wk, wv)


def _tail_kernel(x_ref, gx_ref, wq_ref, kt_ref, v_ref, wo_ref, gf_ref, w1_ref, w2_ref, gl_ref, out_ref):
    hm = x_ref.shape[1] // 2
    head_cols = [slice(h * XDIM, (h + 1) * XDIM) for h in range(XHEADS)]
    rows = [slice(i * hm, (i + 1) * hm) for i in range(2)]
    xs = [x_ref[0, r, :] for r in rows]

    def q_proj(x):
        return _dot(_rms(x, gx_ref[...]).astype(BF16), wq_ref[...]).astype(BF16)

    def scores(xq):
        return [_dot(xq[:, c], kt_ref[0, c, :]) for c in head_cols]

    def attend(ss):
        outs = []
        for s, c in zip(ss, head_cols):
            e = jnp.exp(s - jnp.max(s, axis=-1, keepdims=True))
            p = e * (1.0 / jnp.sum(e, axis=-1, keepdims=True))
            outs.append(_dot(p.astype(BF16), v_ref[0, :, c]).astype(BF16))
        return jnp.concatenate(outs, axis=1)

    def o_proj(x, att):
        return x + _dot(att, wo_ref[...].astype(BF16))

    def mlp(i, x):
        hf = _rms(x, gf_ref[...]).astype(BF16)
        acc = x
        for j in range(D_FF // FF_BLOCK):
            cols = slice(j * FF_BLOCK, (j + 1) * FF_BLOCK)
            u = jnp.maximum(_dot(hf, w1_ref[:, cols].astype(BF16)), 0.0)
            acc = acc + _dot((u * u).astype(BF16), w2_ref[cols, :].astype(BF16))
        out_ref[0, rows[i], :] = _rms(acc, gl_ref[...])

    s_a = scores(q_proj(xs[0]))
    xq_b = q_proj(xs[1])
    att_a = attend(s_a)
    s_b = scores(xq_b)
    x2_a = o_proj(xs[0], att_a)
    x2_b = o_proj(xs[1], attend(s_b))
    mlp(0, x2_a)
    mlp(1, x2_b)


def _tail_call(x, gx, wq, kt, v, wo, gf, w1, w2, gl):
    bsz, seq, d = x.shape
    tm = TM_TAIL
    return pl.pallas_call(
        _tail_kernel,
        grid=(bsz, seq // tm),
        in_specs=[
            pl.BlockSpec((1, tm, d), lambda b, s: (b, s, 0)),
            _const_spec((1, d)),
            _const_spec((d, d)),
            pl.BlockSpec((1, d, MEM_LEN), lambda b, s: (b, 0, 0)),
            pl.BlockSpec((1, MEM_LEN, d), lambda b, s: (b, 0, 0)),
            _const_spec((d, d)),
            _const_spec((1, d)),
            _const_spec((d, D_FF)),
            _const_spec((D_FF, d)),
            _const_spec((1, d)),
        ],
        out_specs=pl.BlockSpec((1, tm, d), lambda b, s: (b, s, 0)),
        out_shape=jax.ShapeDtypeStruct(x.shape, F32),
        compiler_params=pltpu.CompilerParams(
            dimension_semantics=("arbitrary", "arbitrary"), vmem_limit_bytes=TAIL_VMEM_LIMIT),
        name="tail",
    )(x, gx, wq, kt, v, wo, gf, w1, w2, gl)


def kernel(x, mem, norm_mix_g, w_in, lb_logits, hgrn_norm_g, w_pool, pool_scale, w_out,
           norm_x_g, norm_mem_g, w_xq, w_xk, w_xv, w_xo, norm_ffn_g, w_ff1, w_ff2, final_norm_g):
    bsz, seq, d = x.shape
    depth = norm_mix_g.shape[0]
    assert depth == 1 and lb_logits.shape[0] == 2
    row = lambda a: a.reshape(1, -1).astype(F32)
    x = _mix_call(x, row(norm_mix_g), w_in.reshape(d, IN_WIDTH), lb_logits.astype(F32),
                  row(hgrn_norm_g), w_pool.reshape(len(POOL_WINDOWS), DK, DK), row(pool_scale),
                  w_out.reshape(d, d))
    split = (d, XHEADS, XDIM)
    kt, v, wq = _memkv_call(mem, row(norm_mem_g), w_xq.reshape(split), w_xk.reshape(split), w_xv.reshape(split))
    return _tail_call(x, row(norm_x_g), wq, kt, v, w_xo.reshape(d, d),
                      row(norm_ffn_g), w_ff1.reshape(d, D_FF), w_ff2.reshape(D_FF, d), row(final_norm_g))
```

```python
import functools

import jax
import jax.numpy as jnp
import numpy as np
from jax import lax
from jax.experimental import pallas as pl
from jax.experimental.pallas import tpu as pltpu

HEADS = 4
DK = 128
KW = HEADS * DK
PW = 2 * DK
CHUNK = 64
LEVELS = (32, 16, 8, 4, 2, 1)
POOL_WINDOWS = (2, 4, 8, 16)
POOL_HIST = 16
IN_WIDTH = 5 * KW
MEM_LEN = 256
XHEADS = 4
XDIM = 256
D_FF = 4096
EPS = 1e-6
LOG2E = 1.4426950408889634

TS_MIX = 512
SUB_TILE = 256
IN_BLOCK = 256
TM_TAIL = 512
FF_BLOCK = 1024
VMEM_LIMIT = 56 * 1024 * 1024
TAIL_VMEM_LIMIT = 60 * 1024 * 1024

F32 = jnp.float32
BF16 = jnp.bfloat16


def _rms(x, g):
    return x * lax.rsqrt(jnp.mean(x * x, axis=-1, keepdims=True) + EPS) * g


def _dot(a, b):
    return jnp.dot(a, b, preferred_element_type=F32)


def _dot_nt(a, b):
    return lax.dot_general(a, b, (((1,), (1,)), ((), ())), preferred_element_type=F32)


def _dot_tn(a, b):
    return lax.dot_general(a, b, (((0,), (0,)), ((), ())), preferred_element_type=F32)


def _sigmoid(x):
    return 1.0 / (1.0 + jnp.exp(-x))


def _block_diag(a, b):
    z = jnp.zeros_like(a)
    return jnp.concatenate([jnp.concatenate([a, z], axis=1), jnp.concatenate([z, b], axis=1)], axis=0)


def _cumsum_matrix():
    t = np.arange(CHUNK)[:, None]
    s = np.arange(CHUNK)[None, :]
    return (s <= t).astype(np.float32)


def _hgrn_tile(z_ref, ts, cm, lb_all, hg_all, st_ref, mixed_ref):
    n_chunks = ts // CHUNK
    qp = z_ref[:, 0:KW]
    fp = z_ref[:, KW:2 * KW]
    gp = z_ref[:, 3 * KW:4 * KW]

    f = lb_all + (1.0 - lb_all) * _sigmoid(fp)
    kk_all = 1.0 - f
    q_all = qp * _sigmoid(qp)
    gate_all = gp * _sigmoid(gp)

    lf2 = jnp.log(f) * LOG2E
    hi = lf2.astype(BF16)
    lo = (lf2 - hi.astype(F32)).astype(BF16)
    cm2 = jnp.concatenate([cm, cm], axis=1)
    b_chunks = []
    for ci in range(n_chunks):
        r = slice(ci * CHUNK, (ci + 1) * CHUNK)
        b_chunks.append(_dot(cm2, jnp.concatenate([hi[r], lo[r]], axis=0)))
    yield "prep"

    row = lax.broadcasted_iota(jnp.int32, (CHUNK, PW), 0)
    upper = {L: jnp.bitwise_and(row, 2 * L - 1) >= L for L in LEVELS}
    sign = {L: jnp.where(upper[L], 1.0, -1.0) for L in LEVELS if 1 < L < 8}
    up_f = {L: jnp.where(upper[L], 1.0, 0.0) for L in LEVELS if L < 8}
    lo_f = {L: jnp.where(upper[L], 0.0, 1.0) for L in LEVELS if L < 8}
    out_row = lax.broadcasted_iota(jnp.int32, (CHUNK, DK), 0)
    out_col = jnp.bitwise_and(lax.broadcasted_iota(jnp.int32, (CHUNK, DK), 1), CHUNK - 1)
    same_block = {L: jnp.where(jnp.bitwise_and(jnp.bitwise_xor(out_row, out_col), -2 * L) == 0, 1.0, 0.0)
                  for L in LEVELS[1:]}
    diagonal = jnp.where(out_row == out_col, 1.0, 0.0)

    def ref_rows(a, first, period, width):
        parts = [jnp.broadcast_to(a[r:r + 1, :], (width, a.shape[1])) for r in range(first, CHUNK, period)]
        return parts[0] if len(parts) == 1 else jnp.concatenate(parts, axis=0)

    def level_ref(b, L):
        if L >= 4:
            return ref_rows(b, L - 1, 2 * L, 2 * L)
        return jnp.where(jnp.bitwise_and(row, 7) < 4, ref_rows(b, 1, 8, 8), ref_rows(b, 5, 8, 8))

    def pair_dot(qe, ke):
        ke = ke.astype(BF16)
        rhs = _block_diag(ke[:, :DK], ke[:, DK:]).T
        return _dot(qe.astype(BF16), rhs)

    def value_dot(entry):
        level_scores, v_bf = entry[5], entry[6]
        scores = level_scores[-1] * diagonal
        for L, s_l in zip(LEVELS[1:], level_scores[1:-1]):
            scores = scores + s_l * same_block[L]
        scores = scores + level_scores[0]
        return _dot(scores.astype(BF16), _block_diag(v_bf[:, :DK], v_bf[:, DK:]))

    work = []
    o_intra = []
    for ci in range(n_chunks):
        r = slice(ci * CHUNK, (ci + 1) * CHUNK)
        for hp in range(HEADS // 2):
            pc = slice(hp * PW, (hp + 1) * PW)
            q, kk, b = q_all[r, pc], kk_all[r, pc], b_chunks[ci][:, pc]
            b_last = b[CHUNK - 1:CHUNK, :]
            v_bf = z_ref[r, 2 * KW + hp * PW:2 * KW + (hp + 1) * PW].astype(BF16)

            qe_bf = (q * jnp.exp2(b)).astype(BF16)
            k_dec = (kk * jnp.exp2(b_last - b)).astype(BF16)
            upd = [_dot_tn(v_bf[:, i * DK:(i + 1) * DK], k_dec[:, i * DK:(i + 1) * DK]) for i in range(2)]

            level_scores = []
            for L in LEVELS:
                if L >= 8:
                    ref = level_ref(b, L)
                    zeros = jnp.zeros((L, PW), F32)
                    q_runs, k_runs = [], []
                    for r0 in range(0, CHUNK, L):
                        run = slice(r0, r0 + L)
                        if (r0 // L) % 2:
                            q_runs.append(q[run] * jnp.exp2(b[run] - ref[run]))
                            k_runs.append(zeros)
                        else:
                            q_runs.append(zeros)
                            k_runs.append(kk[run] * jnp.exp2(ref[run] - b[run]))
                    level_scores.append(pair_dot(jnp.concatenate(q_runs, axis=0),
                                                 jnp.concatenate(k_runs, axis=0)))
                    continue
                if L == 1:
                    qe, ke = q * f[r, pc] * up_f[L], kk * lo_f[L]
                else:
                    e = jnp.exp2((b - level_ref(b, L)) * sign[L])
                    qe, ke = q * up_f[L] * e, kk * lo_f[L] * e
                level_scores.append(pair_dot(qe, ke))
            level_scores.append(pair_dot(q, kk))
            work.append((ci, hp, qe_bf, jnp.exp2(b_last), upd, level_scores, v_bf))
            if len(work) > 1:
                o_intra.append(value_dot(work[-2]))
            yield "levels"
    o_intra.append(value_dot(work[-1]))
    yield "levels"

    state = [st_ref[h] for h in range(HEADS)]
    o_inter = []
    for ci, hp, qe_bf, decay, upd, level_scores, v_bf in work:
        h0, h1 = 2 * hp, 2 * hp + 1
        o_inter.append(_dot_nt(qe_bf, _block_diag(state[h0].astype(BF16), state[h1].astype(BF16))))
        state[h0] = state[h0] * decay[:, :DK] + upd[0]
        state[h1] = state[h1] * decay[:, DK:] + upd[1]
    for h in range(HEADS):
        st_ref[h] = state[h]
    yield "state"

    for (ci, hp, *_), oa, ob in zip(work, o_inter, o_intra):
        r = slice(ci * CHUNK, (ci + 1) * CHUNK)
        o_pair = ob + oa
        for i in range(2):
            cols = slice((2 * hp + i) * DK, (2 * hp + i + 1) * DK)
            o = o_pair[:, i * DK:(i + 1) * DK]
            o = o * lax.rsqrt(jnp.mean(o * o, axis=-1, keepdims=True) + EPS) * hg_all[:, cols]
            mixed_ref[r, cols] = o * gate_all[r, cols]
        yield "norm"


def _mix_kernel(xn_ref, xp_ref, g_ref, w_in_f32, lbl_ref, hg_ref, w_pool_f32, ps_ref, w_out_f32, cm_ref,
                out_ref, za_ref, zb_ref, st_ref, pext_ref, mixed_ref, w_in_ref, w_pool_ref, w_out_ref,
                *, tiles_per_seq):
    g = pl.program_id(0)
    si = lax.rem(g + tiles_per_seq - 1, tiles_per_seq)

    @pl.when(g == 0)
    def _():
        w_in_ref[...] = w_in_f32[...].astype(BF16)
        w_pool_ref[...] = w_pool_f32[...].astype(BF16)
        w_out_ref[...] = w_out_f32[...].astype(BF16)
        za_ref[...] = _dot(_rms(xn_ref[0], g_ref[...]).astype(BF16), w_in_ref[...])

    @pl.when(si == 0)
    def _():
        st_ref[...] = jnp.zeros_like(st_ref)
        pext_ref[0:POOL_HIST, :] = jnp.zeros((POOL_HIST, KW), F32)

    args = (xn_ref, xp_ref, g_ref, w_in_ref, lbl_ref, hg_ref, w_pool_ref, ps_ref, w_out_ref, cm_ref,
            out_ref, st_ref, pext_ref, mixed_ref, si)

    @pl.when(jnp.logical_and(lax.rem(g, 2) == 0, g > 0))
    def _():
        _mix_step(*args, zn_ref=za_ref, zc_ref=zb_ref)

    @pl.when(lax.rem(g, 2) == 1)
    def _():
        _mix_step(*args, zn_ref=zb_ref, zc_ref=za_ref)


def _mix_step(xn_ref, xp_ref, g_ref, w_in_ref, lbl_ref, hg_ref, w_pool_ref, ps_ref, w_out_ref, cm_ref,
              out_ref, st_ref, pext_ref, mixed_ref, si, *, zn_ref, zc_ref):
    ts = xn_ref.shape[1]
    d = xn_ref.shape[2]
    n_sub = ts // SUB_TILE
    hn_bf = _rms(xn_ref[0], g_ref[...]).astype(BF16)

    def in_proj_unit(blk):
        def run():
            cols = slice(blk * IN_BLOCK, (blk + 1) * IN_BLOCK)
            zn_ref[:, cols] = _dot(hn_bf, w_in_ref[:, cols])
        return run

    def out_proj_unit(sub, blk):
        def run():
            r = slice(sub * SUB_TILE, (sub + 1) * SUB_TILE)
            cols = slice(blk * IN_BLOCK, (blk + 1) * IN_BLOCK)
            out_ref[0, r, cols] = xp_ref[0, r, cols] + _dot(mixed_ref[r, :].astype(BF16), w_out_ref[:, cols])
        return run

    in_units = [in_proj_unit(blk) for blk in range(IN_WIDTH // IN_BLOCK)]

    l0 = lbl_ref[0:1, :]
    l1 = lbl_ref[1:2, :]
    lmax = jnp.maximum(l0, l1)
    e0 = jnp.exp(l0 - lmax)
    lb_all = e0 / (e0 + jnp.exp(l1 - lmax))

    def pool(sub):
        r = slice(sub * SUB_TILE, (sub + 1) * SUB_TILE)
        p = zc_ref[r, 4 * KW:5 * KW]
        pext_ref[POOL_HIST:, :] = p
        pos = (si * ts + sub * SUB_TILE + 1 + lax.broadcasted_iota(jnp.int32, (SUB_TILE, 1), 0)).astype(F32)
        for gi, w in enumerate(POOL_WINDOWS):
            cols = slice(gi * DK, (gi + 1) * DK)
            acc = pext_ref[:, cols]
            sh = 1
            while sh < w:
                acc = acc + pltpu.roll(acc, sh, 0)
                sh *= 2
            win = acc[POOL_HIST:, :]
            pooled = win / jnp.minimum(pos, float(w)) - p[:, cols]
            y = _dot(pooled.astype(BF16), w_pool_ref[gi]) * ps_ref[:, cols]
            mixed_ref[r, KW + gi * DK:KW + (gi + 1) * DK] = y
        pext_ref[0:POOL_HIST, :] = pext_ref[SUB_TILE:SUB_TILE + POOL_HIST, :]

    def hgrn(sub):
        r = pl.ds(sub * SUB_TILE, SUB_TILE)
        return _hgrn_tile(zc_ref.at[r], SUB_TILE, cm_ref[...], lb_all, hg_ref[...], st_ref, mixed_ref.at[r])

    def step(gen, label):
        got = next(gen)
        assert got == label, (got, label)

    def fill(n=1):
        for _ in range(n):
            if in_units:
                in_units.pop(0)()

    n_items = (SUB_TILE // CHUNK) * (HEADS // 2)
    gens = [hgrn(sub) for sub in range(n_sub)]
    slot_every = max(1, n_items // 4)
    out_units = []
    step(gens[0], "prep")
    fill(2)
    for i in range(n_items + 1):
        step(gens[0], "levels")
        if i % slot_every == slot_every - 1:
            fill()
    for sub in range(n_sub):
        cur = gens[sub]
        nxt = gens[sub + 1] if sub + 1 < n_sub else None
        if nxt is not None:
            step(nxt, "prep")
        step(cur, "state")
        for i in range(n_items):
            if nxt is not None:
                step(nxt, "levels")
            step(cur, "norm")
            if i % slot_every == slot_every - 1:
                (out_units.pop(0) if out_units else fill)()
        if nxt is not None:
            step(nxt, "levels")
        pool(sub)
        out_units += [out_proj_unit(sub, blk) for blk in range(d // IN_BLOCK)]
    fill(len(in_units))
    for unit in out_units:
        unit()


def _const_spec(shape):
    nd = len(shape)
    return pl.BlockSpec(shape, lambda *_: (0,) * nd, pipeline_mode=pl.Buffered(1))


def _mix_call(x, g, w_in, lbl, hg, w_pool, ps, w_out):
    bsz, seq, d = x.shape
    ts = TS_MIX
    ns = seq // ts
    n_tiles = bsz * ns
    cm = jnp.asarray(_cumsum_matrix(), BF16)

    def tile_next(i):
        t = jnp.minimum(i, n_tiles - 1)
        return (t // ns, t % ns, 0)

    def tile_prev(i):
        t = jnp.maximum(i - 1, 0)
        return (t // ns, t % ns, 0)

    return pl.pallas_call(
        functools.partial(_mix_kernel, tiles_per_seq=ns),
        grid=(n_tiles + 1,),
        in_specs=[
            pl.BlockSpec((1, ts, d), tile_next),
            pl.BlockSpec((1, ts, d), tile_prev),
            _const_spec((1, d)),
            _const_spec((d, IN_WIDTH)),
            _const_spec((2, KW)),
            _const_spec((1, KW)),
            _const_spec((len(POOL_WINDOWS), DK, DK)),
            _const_spec((1, KW)),
            _const_spec((d, d)),
            _const_spec((CHUNK, CHUNK)),
        ],
        out_specs=pl.BlockSpec((1, ts, d), tile_prev),
        out_shape=jax.ShapeDtypeStruct(x.shape, F32),
        scratch_shapes=[
            pltpu.VMEM((ts, IN_WIDTH), F32),
            pltpu.VMEM((ts, IN_WIDTH), F32),
            pltpu.VMEM((HEADS, DK, DK), F32),
            pltpu.VMEM((POOL_HIST + SUB_TILE, KW), F32),
            pltpu.VMEM((ts, d), F32),
            pltpu.VMEM((d, IN_WIDTH), BF16),
            pltpu.VMEM((len(POOL_WINDOWS), DK, DK), BF16),
            pltpu.VMEM((d, d), BF16),
        ],
        compiler_params=pltpu.CompilerParams(
            dimension_semantics=("arbitrary",), vmem_limit_bytes=VMEM_LIMIT),
        name="mix",
    )(x, x, g, w_in, lbl, hg, w_pool, ps, w_out, cm)


def _memkv_kernel(mem_ref, g_ref, wq_ref, wk_ref, wv_ref, kt_ref, v_ref, wq_out_ref, wk_bf, wv_bf):
    def dense(w_ref):
        return jnp.concatenate([w_ref[:, h, :] for h in range(XHEADS)], axis=1).astype(BF16)

    @pl.when(pl.program_id(0) == 0)
    def _():
        wq_out_ref[...] = dense(wq_ref)
        wk_bf[...] = dense(wk_ref)
        wv_bf[...] = dense(wv_ref)

    hm = _rms(mem_ref[0], g_ref[...]).astype(BF16)
    k = _dot(hm, wk_bf[...]) * (XDIM ** -0.5)
    kt_ref[0] = k.T.astype(BF16)
    v_ref[0] = _dot(hm, wv_bf[...]).astype(BF16)


def _memkv_call(mem, g, wq, wk, wv):
    bsz, m, d = mem.shape
    return pl.pallas_call(
        _memkv_kernel,
        grid=(bsz,),
        in_specs=[
            pl.BlockSpec((1, m, d), lambda b: (b, 0, 0)),
            _const_spec((1, d)),
            _const_spec((d, XHEADS, XDIM)),
            _const_spec((d, XHEADS, XDIM)),
            _const_spec((d, XHEADS, XDIM)),
        ],
        out_specs=[
            pl.BlockSpec((1, d, m), lambda b: (b, 0, 0)),
            pl.BlockSpec((1, m, d), lambda b: (b, 0, 0)),
            pl.BlockSpec((d, d), lambda b: (0, 0)),
        ],
        out_shape=[
            jax.ShapeDtypeStruct((bsz, d, m), BF16),
            jax.ShapeDtypeStruct((bsz, m, d), BF16),
            jax.ShapeDtypeStruct((d, d), BF16),
        ],
        scratch_shapes=[pltpu.VMEM((d, d), BF16), pltpu.VMEM((d, d), BF16)],
        compiler_params=pltpu.CompilerParams(
            dimension_semantics=("arbitrary",), vmem_limit_bytes=VMEM_LIMIT),
        name="memkv",
    )(mem, g, wq, wk, wv)


def _tail_kernel(x_ref, gx_ref, wq_ref, kt_ref, v_ref, wo_ref, gf_ref, w1_ref, w2_ref, gl_ref, out_ref):
    hm = x_ref.shape[1] // 2
    head_cols = [slice(h * XDIM, (h + 1) * XDIM) for h in range(XHEADS)]
    rows = [slice(i * hm, (i + 1) * hm) for i in range(2)]
    xs = [x_ref[0, r, :] for r in rows]

    def q_proj(x):
        return _dot(_rms(x, gx_ref[...]).astype(BF16), wq_ref[...]).astype(BF16)

    def scores(xq):
        return [_dot(xq[:, c], kt_ref[0, c, :]) for c in head_cols]

    def attend(ss):
        outs = []
        for s, c in zip(ss, head_cols):
            e = jnp.exp(s - jnp.max(s, axis=-1, keepdims=True))
            p = e * (1.0 / jnp.sum(e, axis=-1, keepdims=True))
            outs.append(_dot(p.astype(BF16), v_ref[0, :, c]).astype(BF16))
        return jnp.concatenate(outs, axis=1)

    def o_proj(x, att):
        return x + _dot(att, wo_ref[...].astype(BF16))

    def mlp(i, x):
        hf = _rms(x, gf_ref[...]).astype(BF16)
        acc = x
        for j in range(D_FF // FF_BLOCK):
            cols = slice(j * FF_BLOCK, (j + 1) * FF_BLOCK)
            u = jnp.maximum(_dot(hf, w1_ref[:, cols].astype(BF16)), 0.0)
            acc = acc + _dot((u * u).astype(BF16), w2_ref[cols, :].astype(BF16))
        out_ref[0, rows[i], :] = _rms(acc, gl_ref[...])

    s_a = scores(q_proj(xs[0]))
    xq_b = q_proj(xs[1])
    att_a = attend(s_a)
    s_b = scores(xq_b)
    x2_a = o_proj(xs[0], att_a)
    x2_b = o_proj(xs[1], attend(s_b))
    mlp(0, x2_a)
    mlp(1, x2_b)


def _tail_call(x, gx, wq, kt, v, wo, gf, w1, w2, gl):
    bsz, seq, d = x.shape
    tm = TM_TAIL
    return pl.pallas_call(
        _tail_kernel,
        grid=(bsz, seq // tm),
        in_specs=[
            pl.BlockSpec((1, tm, d), lambda b, s: (b, s, 0)),
            _const_spec((1, d)),
            _const_spec((d, d)),
            pl.BlockSpec((1, d, MEM_LEN), lambda b, s: (b, 0, 0)),
            pl.BlockSpec((1, MEM_LEN, d), lambda b, s: (b, 0, 0)),
            _const_spec((d, d)),
            _const_spec((1, d)),
            _const_spec((d, D_FF)),
            _const_spec((D_FF, d)),
            _const_spec((1, d)),
        ],
        out_specs=pl.BlockSpec((1, tm, d), lambda b, s: (b, s, 0)),
        out_shape=jax.ShapeDtypeStruct(x.shape, F32),
        compiler_params=pltpu.CompilerParams(
            dimension_semantics=("arbitrary", "arbitrary"), vmem_limit_bytes=TAIL_VMEM_LIMIT),
        name="tail",
    )(x, gx, wq, kt, v, wo, gf, w1, w2, gl)


def kernel(x, mem, norm_mix_g, w_in, lb_logits, hgrn_norm_g, w_pool, pool_scale, w_out,
           norm_x_g, norm_mem_g, w_xq, w_xk, w_xv, w_xo, norm_ffn_g, w_ff1, w_ff2, final_norm_g):
    bsz, seq, d = x.shape
    depth = norm_mix_g.shape[0]
    assert depth == 1 and lb_logits.shape[0] == 2
    row = lambda a: a.reshape(1, -1).astype(F32)
    x = _mix_call(x, row(norm_mix_g), w_in.reshape(d, IN_WIDTH), lb_logits.astype(F32),
                  row(hgrn_norm_g), w_pool.reshape(len(POOL_WINDOWS), DK, DK), row(pool_scale),
                  w_out.reshape(d, d))
    split = (d, XHEADS, XDIM)
    kt, v, wq = _memkv_call(mem, row(norm_mem_g), w_xq.reshape(split), w_xk.reshape(split), w_xv.reshape(split))
    return _tail_call(x, row(norm_x_g), wq, kt, v, w_xo.reshape(d, d),
                      row(norm_ffn_g), w_ff1.reshape(d, D_FF), w_ff2.reshape(D_FF, d), row(final_norm_g))
```

```python
import functools

import jax
import jax.numpy as jnp
import numpy as np
from jax import lax
from jax.experimental import pallas as pl
from jax.experimental.pallas import tpu as pltpu

HEADS = 4
DK = 128
KW = HEADS * DK
PW = 2 * DK
CHUNK = 64
LEVELS = (32, 16, 8, 4, 2, 1)
POOL_WINDOWS = (2, 4, 8, 16)
POOL_HIST = 16
IN_WIDTH = 5 * KW
MEM_LEN = 256
XHEADS = 4
XDIM = 256
D_FF = 4096
EPS = 1e-6
LOG2E = 1.4426950408889634

TS_MIX = 512
SUB_TILE = 256
IN_BLOCK = 256
TM_TAIL = 512
FF_BLOCK = 1024
VMEM_LIMIT = 56 * 1024 * 1024
TAIL_VMEM_LIMIT = 60 * 1024 * 1024

F32 = jnp.float32
BF16 = jnp.bfloat16


def _rms(x, g):
    return x * lax.rsqrt(jnp.mean(x * x, axis=-1, keepdims=True) + EPS) * g


def _dot(a, b):
    return jnp.dot(a, b, preferred_element_type=F32)


def _dot_nt(a, b):
    return lax.dot_general(a, b, (((1,), (1,)), ((), ())), preferred_element_type=F32)


def _dot_tn(a, b):
    return lax.dot_general(a, b, (((0,), (0,)), ((), ())), preferred_element_type=F32)


def _sigmoid(x):
    return 1.0 / (1.0 + jnp.exp(-x))


def _block_diag(a, b):
    z = jnp.zeros_like(a)
    return jnp.concatenate([jnp.concatenate([a, z], axis=1), jnp.concatenate([z, b], axis=1)], axis=0)


def _cumsum_matrix():
    t = np.arange(CHUNK)[:, None]
    s = np.arange(CHUNK)[None, :]
    return (s <= t).astype(np.float32)


def _hgrn_tile(z_ref, ts, cm, lb_all, hg_all, st_ref, mixed_ref):
    n_chunks = ts // CHUNK
    qp = z_ref[:, 0:KW]
    fp = z_ref[:, KW:2 * KW]
    gp = z_ref[:, 3 * KW:4 * KW]

    f = lb_all + (1.0 - lb_all) * _sigmoid(fp)
    kk_all = 1.0 - f
    q_all = qp * _sigmoid(qp)
    gate_all = gp * _sigmoid(gp)

    lf2 = jnp.log(f) * LOG2E
    hi = lf2.astype(BF16)
    lo = (lf2 - hi.astype(F32)).astype(BF16)
    cm2 = jnp.concatenate([cm, cm], axis=1)
    b_chunks = []
    for ci in range(n_chunks):
        r = slice(ci * CHUNK, (ci + 1) * CHUNK)
        b_chunks.append(_dot(cm2, jnp.concatenate([hi[r], lo[r]], axis=0)))
    yield "prep"

    row = lax.broadcasted_iota(jnp.int32, (CHUNK, PW), 0)
    upper = {L: jnp.bitwise_and(row, 2 * L - 1) >= L for L in LEVELS}
    sign = {L: jnp.where(upper[L], 1.0, -1.0) for L in LEVELS if 1 < L < 8}
    up_f = {L: jnp.where(upper[L], 1.0, 0.0) for L in LEVELS if L < 8}
    lo_f = {L: jnp.where(upper[L], 0.0, 1.0) for L in LEVELS if L < 8}
    out_row = lax.broadcasted_iota(jnp.int32, (CHUNK, DK), 0)
    out_col = jnp.bitwise_and(lax.broadcasted_iota(jnp.int32, (CHUNK, DK), 1), CHUNK - 1)
    same_block = {L: jnp.where(jnp.bitwise_and(jnp.bitwise_xor(out_row, out_col), -2 * L) == 0, 1.0, 0.0)
                  for L in LEVELS[1:]}
    diagonal = jnp.where(out_row == out_col, 1.0, 0.0)

    def ref_rows(a, first, period, width):
        parts = [jnp.broadcast_to(a[r:r + 1, :], (width, a.shape[1])) for r in range(first, CHUNK, period)]
        return parts[0] if len(parts) == 1 else jnp.concatenate(parts, axis=0)

    def level_ref(b, L):
        if L >= 4:
            return ref_rows(b, L - 1, 2 * L, 2 * L)
        return jnp.where(jnp.bitwise_and(row, 7) < 4, ref_rows(b, 1, 8, 8), ref_rows(b, 5, 8, 8))

    def pair_dot(qe, ke):
        rhs = _block_diag(ke[:, :DK], ke[:, DK:]).T.astype(BF16)
        return _dot(qe.astype(BF16), rhs)

    def value_dot(entry):
        level_scores, v_bf = entry[5], entry[6]
        scores = level_scores[-1] * diagonal
        for L, s_l in zip(LEVELS[1:], level_scores[1:-1]):
            scores = scores + s_l * same_block[L]
        scores = scores + level_scores[0]
        return _dot(scores.astype(BF16), _block_diag(v_bf[:, :DK], v_bf[:, DK:]))

    work = []
    o_intra = []
    for ci in range(n_chunks):
        r = slice(ci * CHUNK, (ci + 1) * CHUNK)
        for hp in range(HEADS // 2):
            pc = slice(hp * PW, (hp + 1) * PW)
            q, kk, b = q_all[r, pc], kk_all[r, pc], b_chunks[ci][:, pc]
            b_last = b[CHUNK - 1:CHUNK, :]
            v_bf = z_ref[r, 2 * KW + hp * PW:2 * KW + (hp + 1) * PW].astype(BF16)

            qe_bf = (q * jnp.exp2(b)).astype(BF16)
            k_dec = (kk * jnp.exp2(b_last - b)).astype(BF16)
            upd = [_dot_tn(v_bf[:, i * DK:(i + 1) * DK], k_dec[:, i * DK:(i + 1) * DK]) for i in range(2)]

            level_scores = []
            for L in LEVELS:
                if L >= 8:
                    ref = level_ref(b, L)
                    zeros = jnp.zeros((L, PW), F32)
                    q_runs, k_runs = [], []
                    for r0 in range(0, CHUNK, L):
                        run = slice(r0, r0 + L)
                        if (r0 // L) % 2:
                            q_runs.append(q[run] * jnp.exp2(b[run] - ref[run]))
                            k_runs.append(zeros)
                        else:
                            q_runs.append(zeros)
                            k_runs.append(kk[run] * jnp.exp2(ref[run] - b[run]))
                    level_scores.append(pair_dot(jnp.concatenate(q_runs, axis=0),
                                                 jnp.concatenate(k_runs, axis=0)))
                    continue
                if L == 1:
                    qe, ke = q * f[r, pc] * up_f[L], kk * lo_f[L]
                else:
                    e = jnp.exp2((b - level_ref(b, L)) * sign[L])
                    qe, ke = q * up_f[L] * e, kk * lo_f[L] * e
                level_scores.append(pair_dot(qe, ke))
            level_scores.append(pair_dot(q, kk))
            work.append((ci, hp, qe_bf, jnp.exp2(b_last), upd, level_scores, v_bf))
            if len(work) > 1:
                o_intra.append(value_dot(work[-2]))
            yield "levels"
    o_intra.append(value_dot(work[-1]))
    yield "levels"

    state = [st_ref[h] for h in range(HEADS)]
    o_inter = []
    for ci, hp, qe_bf, decay, upd, level_scores, v_bf in work:
        h0, h1 = 2 * hp, 2 * hp + 1
        o_inter.append(_dot_nt(qe_bf, _block_diag(state[h0].astype(BF16), state[h1].astype(BF16))))
        state[h0] = state[h0] * decay[:, :DK] + upd[0]
        state[h1] = state[h1] * decay[:, DK:] + upd[1]
    for h in range(HEADS):
        st_ref[h] = state[h]
    yield "state"

    for (ci, hp, *_), oa, ob in zip(work, o_inter, o_intra):
        r = slice(ci * CHUNK, (ci + 1) * CHUNK)
        o_pair = ob + oa
        for i in range(2):
            cols = slice((2 * hp + i) * DK, (2 * hp + i + 1) * DK)
            o = o_pair[:, i * DK:(i + 1) * DK]
            o = o * lax.rsqrt(jnp.mean(o * o, axis=-1, keepdims=True) + EPS) * hg_all[:, cols]
            mixed_ref[r, cols] = o * gate_all[r, cols]
        yield "norm"


def _mix_kernel(xn_ref, xp_ref, g_ref, w_in_f32, lbl_ref, hg_ref, w_pool_f32, ps_ref, w_out_f32, cm_ref,
                out_ref, za_ref, zb_ref, st_ref, pext_ref, mixed_ref, w_in_ref, w_pool_ref, w_out_ref,
                *, tiles_per_seq):
    g = pl.program_id(0)
    si = lax.rem(g + tiles_per_seq - 1, tiles_per_seq)

    @pl.when(g == 0)
    def _():
        w_in_ref[...] = w_in_f32[...].astype(BF16)
        w_pool_ref[...] = w_pool_f32[...].astype(BF16)
        w_out_ref[...] = w_out_f32[...].astype(BF16)
        za_ref[...] = _dot(_rms(xn_ref[0], g_ref[...]).astype(BF16), w_in_ref[...])

    @pl.when(si == 0)
    def _():
        st_ref[...] = jnp.zeros_like(st_ref)
        pext_ref[0:POOL_HIST, :] = jnp.zeros((POOL_HIST, KW), F32)

    args = (xn_ref, xp_ref, g_ref, w_in_ref, lbl_ref, hg_ref, w_pool_ref, ps_ref, w_out_ref, cm_ref,
            out_ref, st_ref, pext_ref, mixed_ref, si)

    @pl.when(jnp.logical_and(lax.rem(g, 2) == 0, g > 0))
    def _():
        _mix_step(*args, zn_ref=za_ref, zc_ref=zb_ref)

    @pl.when(lax.rem(g, 2) == 1)
    def _():
        _mix_step(*args, zn_ref=zb_ref, zc_ref=za_ref)


def _mix_step(xn_ref, xp_ref, g_ref, w_in_ref, lbl_ref, hg_ref, w_pool_ref, ps_ref, w_out_ref, cm_ref,
              out_ref, st_ref, pext_ref, mixed_ref, si, *, zn_ref, zc_ref):
    ts = xn_ref.shape[1]
    d = xn_ref.shape[2]
    n_sub = ts // SUB_TILE
    hn_bf = _rms(xn_ref[0], g_ref[...]).astype(BF16)

    def in_proj_unit(blk):
        def run():
            cols = slice(blk * IN_BLOCK, (blk + 1) * IN_BLOCK)
            zn_ref[:, cols] = _dot(hn_bf, w_in_ref[:, cols])
        return run

    def out_proj_unit(sub, blk):
        def run():
            r = slice(sub * SUB_TILE, (sub + 1) * SUB_TILE)
            cols = slice(blk * IN_BLOCK, (blk + 1) * IN_BLOCK)
            out_ref[0, r, cols] = xp_ref[0, r, cols] + _dot(mixed_ref[r, :].astype(BF16), w_out_ref[:, cols])
        return run

    in_units = [in_proj_unit(blk) for blk in range(IN_WIDTH // IN_BLOCK)]

    l0 = lbl_ref[0:1, :]
    l1 = lbl_ref[1:2, :]
    lmax = jnp.maximum(l0, l1)
    e0 = jnp.exp(l0 - lmax)
    lb_all = e0 / (e0 + jnp.exp(l1 - lmax))

    def pool(sub):
        r = slice(sub * SUB_TILE, (sub + 1) * SUB_TILE)
        p = zc_ref[r, 4 * KW:5 * KW]
        pext_ref[POOL_HIST:, :] = p
        pos = (si * ts + sub * SUB_TILE + 1 + lax.broadcasted_iota(jnp.int32, (SUB_TILE, 1), 0)).astype(F32)
        for gi, w in enumerate(POOL_WINDOWS):
            cols = slice(gi * DK, (gi + 1) * DK)
            acc = pext_ref[:, cols]
            sh = 1
            while sh < w:
                acc = acc + pltpu.roll(acc, sh, 0)
                sh *= 2
            win = acc[POOL_HIST:, :]
            pooled = win / jnp.minimum(pos, float(w)) - p[:, cols]
            y = _dot(pooled.astype(BF16), w_pool_ref[gi]) * ps_ref[:, cols]
            mixed_ref[r, KW + gi * DK:KW + (gi + 1) * DK] = y
        pext_ref[0:POOL_HIST, :] = pext_ref[SUB_TILE:SUB_TILE + POOL_HIST, :]

    def hgrn(sub):
        r = pl.ds(sub * SUB_TILE, SUB_TILE)
        return _hgrn_tile(zc_ref.at[r], SUB_TILE, cm_ref[...], lb_all, hg_ref[...], st_ref, mixed_ref.at[r])

    def step(gen, label):
        got = next(gen)
        assert got == label, (got, label)

    def fill(n=1):
        for _ in range(n):
            if in_units:
                in_units.pop(0)()

    n_items = (SUB_TILE // CHUNK) * (HEADS // 2)
    gens = [hgrn(sub) for sub in range(n_sub)]
    slot_every = max(1, n_items // 4)
    out_units = []
    step(gens[0], "prep")
    fill(3)
    for i in range(n_items + 1):
        step(gens[0], "levels")
        if i % slot_every == slot_every - 1:
            fill()
    for sub in range(n_sub):
        cur = gens[sub]
        nxt = gens[sub + 1] if sub + 1 < n_sub else None
        if nxt is not None:
            step(nxt, "prep")
        step(cur, "state")
        for i in range(n_items):
            if nxt is not None:
                step(nxt, "levels")
            step(cur, "norm")
            if i % slot_every == slot_every - 1:
                (out_units.pop(0) if out_units else fill)()
        if nxt is not None:
            step(nxt, "levels")
        pool(sub)
        out_units += [out_proj_unit(sub, blk) for blk in range(d // IN_BLOCK)]
    fill(len(in_units))
    for unit in out_units:
        unit()


def _const_spec(shape):
    nd = len(shape)
    return pl.BlockSpec(shape, lambda *_: (0,) * nd, pipeline_mode=pl.Buffered(1))


def _mix_call(x, g, w_in, lbl, hg, w_pool, ps, w_out):
    bsz, seq, d = x.shape
    ts = TS_MIX
    ns = seq // ts
    n_tiles = bsz * ns
    cm = jnp.asarray(_cumsum_matrix(), BF16)

    def tile_next(i):
        t = jnp.minimum(i, n_tiles - 1)
        return (t // ns, t % ns, 0)

    def tile_prev(i):
        t = jnp.maximum(i - 1, 0)
        return (t // ns, t % ns, 0)

    return pl.pallas_call(
        functools.partial(_mix_kernel, tiles_per_seq=ns),
        grid=(n_tiles + 1,),
        in_specs=[
            pl.BlockSpec((1, ts, d), tile_next),
            pl.BlockSpec((1, ts, d), tile_prev),
            _const_spec((1, d)),
            _const_spec((d, IN_WIDTH)),
            _const_spec((2, KW)),
            _const_spec((1, KW)),
            _const_spec((len(POOL_WINDOWS), DK, DK)),
            _const_spec((1, KW)),
            _const_spec((d, d)),
            _const_spec((CHUNK, CHUNK)),
        ],
        out_specs=pl.BlockSpec((1, ts, d), tile_prev),
        out_shape=jax.ShapeDtypeStruct(x.shape, F32),
        scratch_shapes=[
            pltpu.VMEM((ts, IN_WIDTH), F32),
            pltpu.VMEM((ts, IN_WIDTH), F32),
            pltpu.VMEM((HEADS, DK, DK), F32),
            pltpu.VMEM((POOL_HIST + SUB_TILE, KW), F32),
            pltpu.VMEM((ts, d), F32),
            pltpu.VMEM((d, IN_WIDTH), BF16),
            pltpu.VMEM((len(POOL_WINDOWS), DK, DK), BF16),
            pltpu.VMEM((d, d), BF16),
        ],
        compiler_params=pltpu.CompilerParams(
            dimension_semantics=("arbitrary",), vmem_limit_bytes=VMEM_LIMIT),
        name="mix",
    )(x, x, g, w_in, lbl, hg, w_pool, ps, w_out, cm)


def _memkv_kernel(mem_ref, g_ref, wq_ref, wk_ref, wv_ref, kt_ref, v_ref, wq_out_ref, wk_bf, wv_bf):
    def dense(w_ref):
        return jnp.concatenate([w_ref[:, h, :] for h in range(XHEADS)], axis=1).astype(BF16)

    @pl.when(pl.program_id(0) == 0)
    def _():
        wq_out_ref[...] = dense(wq_ref)
        wk_bf[...] = dense(wk_ref)
        wv_bf[...] = dense(wv_ref)

    hm = _rms(mem_ref[0], g_ref[...]).astype(BF16)
    k = _dot(hm, wk_bf[...]) * (XDIM ** -0.5)
    kt_ref[0] = k.T.astype(BF16)
    v_ref[0] = _dot(hm, wv_bf[...]).astype(BF16)


def _memkv_call(mem, g, wq, wk, wv):
    bsz, m, d = mem.shape
    return pl.pallas_call(
        _memkv_kernel,
        grid=(bsz,),
        in_specs=[
            pl.BlockSpec((1, m, d), lambda b: (b, 0, 0)),
            _const_spec((1, d)),
            _const_spec((d, XHEADS, XDIM)),
            _const_spec((d, XHEADS, XDIM)),
            _const_spec((d, XHEADS, XDIM)),
        ],
        out_specs=[
            pl.BlockSpec((1, d, m), lambda b: (b, 0, 0)),
            pl.BlockSpec((1, m, d), lambda b: (b, 0, 0)),
            pl.BlockSpec((d, d), lambda b: (0, 0)),
        ],
        out_shape=[
            jax.ShapeDtypeStruct((bsz, d, m), BF16),
            jax.ShapeDtypeStruct((bsz, m, d), BF16),
            jax.ShapeDtypeStruct((d, d), BF16),
        ],
        scratch_shapes=[pltpu.VMEM((d, d), BF16), pltpu.VMEM((d, d), BF16)],
        compiler_params=pltpu.CompilerParams(
            dimension_semantics=("arbitrary",), vmem_limit_bytes=VMEM_LIMIT),
        name="memkv",
    )(mem, g, wq, wk, wv)


def _tail_kernel(x_ref, gx_ref, wq_ref, kt_ref, v_ref, wo_ref, gf_ref, w1_ref, w2_ref, gl_ref, out_ref):
    hm = x_ref.shape[1] // 2
    head_cols = [slice(h * XDIM, (h + 1) * XDIM) for h in range(XHEADS)]
    rows = [slice(i * hm, (i + 1) * hm) for i in range(2)]
    xs = [x_ref[0, r, :] for r in rows]

    def q_proj(x):
        return _dot(_rms(x, gx_ref[...]).astype(BF16), wq_ref[...]).astype(BF16)

    def scores(xq):
        return [_dot(xq[:, c], kt_ref[0, c, :]) for c in head_cols]

    def attend(ss):
        outs = []
        for s, c in zip(ss, head_cols):
            e = jnp.exp(s - jnp.max(s, axis=-1, keepdims=True))
            p = e * (1.0 / jnp.sum(e, axis=-1, keepdims=True))
            outs.append(_dot(p.astype(BF16), v_ref[0, :, c]).astype(BF16))
        return jnp.concatenate(outs, axis=1)

    def o_proj(x, att):
        return x + _dot(att, wo_ref[...].astype(BF16))

    def mlp(i, x):
        hf = _rms(x, gf_ref[...]).astype(BF16)
        acc = x
        for j in range(D_FF // FF_BLOCK):
            cols = slice(j * FF_BLOCK, (j + 1) * FF_BLOCK)
            u = jnp.maximum(_dot(hf, w1_ref[:, cols].astype(BF16)), 0.0)
            acc = acc + _dot((u * u).astype(BF16), w2_ref[cols, :].astype(BF16))
        out_ref[0, rows[i], :] = _rms(acc, gl_ref[...])

    s_a = scores(q_proj(xs[0]))
    xq_b = q_proj(xs[1])
    att_a = attend(s_a)
    s_b = scores(xq_b)
    x2_a = o_proj(xs[0], att_a)
    x2_b = o_proj(xs[1], attend(s_b))
    mlp(0, x2_a)
    mlp(1, x2_b)


def _tail_call(x, gx, wq, kt, v, wo, gf, w1, w2, gl):
    bsz, seq, d = x.shape
    tm = TM_TAIL
    return pl.pallas_call(
        _tail_kernel,
        grid=(bsz, seq // tm),
        in_specs=[
            pl.BlockSpec((1, tm, d), lambda b, s: (b, s, 0)),
            _const_spec((1, d)),
            _const_spec((d, d)),
            pl.BlockSpec((1, d, MEM_LEN), lambda b, s: (b, 0, 0)),
            pl.BlockSpec((1, MEM_LEN, d), lambda b, s: (b, 0, 0)),
            _const_spec((d, d)),
            _const_spec((1, d)),
            _const_spec((d, D_FF)),
            _const_spec((D_FF, d)),
            _const_spec((1, d)),
        ],
        out_specs=pl.BlockSpec((1, tm, d), lambda b, s: (b, s, 0)),
        out_shape=jax.ShapeDtypeStruct(x.shape, F32),
        compiler_params=pltpu.CompilerParams(
            dimension_semantics=("arbitrary", "arbitrary"), vmem_limit_bytes=TAIL_VMEM_LIMIT),
        name="tail",
    )(x, gx, wq, kt, v, wo, gf, w1, w2, gl)


def kernel(x, mem, norm_mix_g, w_in, lb_logits, hgrn_norm_g, w_pool, pool_scale, w_out,
           norm_x_g, norm_mem_g, w_xq, w_xk, w_xv, w_xo, norm_ffn_g, w_ff1, w_ff2, final_norm_g):
    bsz, seq, d = x.shape
    depth = norm_mix_g.shape[0]
    assert depth == 1 and lb_logits.shape[0] == 2
    row = lambda a: a.reshape(1, -1).astype(F32)
    x = _mix_call(x, row(norm_mix_g), w_in.reshape(d, IN_WIDTH), lb_logits.astype(F32),
                  row(hgrn_norm_g), w_pool.reshape(len(POOL_WINDOWS), DK, DK), row(pool_scale),
                  w_out.reshape(d, d))
    split = (d, XHEADS, XDIM)
    kt, v, wq = _memkv_call(mem, row(norm_mem_g), w_xq.reshape(split), w_xk.reshape(split), w_xv.reshape(split))
    return _tail_call(x, row(norm_x_g), wq, kt, v, w_xo.reshape(d, d),
                      row(norm_ffn_g), w_ff1.reshape(d, D_FF), w_ff2.reshape(D_FF, d), row(final_norm_g))
```

```python
import functools

import jax
import jax.numpy as jnp
import numpy as np
from jax import lax
from jax.experimental import pallas as pl
from jax.experimental.pallas import tpu as pltpu

HEADS = 4
DK = 128
KW = HEADS * DK
PW = 2 * DK
CHUNK = 64
LEVELS = (32, 16, 8, 4, 2, 1)
POOL_WINDOWS = (2, 4, 8, 16)
POOL_HIST = 16
IN_WIDTH = 5 * KW
MEM_LEN = 256
XHEADS = 4
XDIM = 256
D_FF = 4096
EPS = 1e-6
LOG2E = 1.4426950408889634

TS_MIX = 512
SUB_TILE = 256
IN_BLOCK = 256
RHS_SLOTS = 2 * (len(LEVELS) + 1)
MIX_VMEM_LIMIT = 60 * 1024 * 1024
TM_TAIL = 512
FF_BLOCK = 1024
VMEM_LIMIT = 56 * 1024 * 1024
TAIL_VMEM_LIMIT = 60 * 1024 * 1024

F32 = jnp.float32
BF16 = jnp.bfloat16


def _rms(x, g):
    return x * lax.rsqrt(jnp.mean(x * x, axis=-1, keepdims=True) + EPS) * g


def _dot(a, b):
    return jnp.dot(a, b, preferred_element_type=F32)


def _dot_nt(a, b):
    return lax.dot_general(a, b, (((1,), (1,)), ((), ())), preferred_element_type=F32)


def _dot_tn(a, b):
    return lax.dot_general(a, b, (((0,), (0,)), ((), ())), preferred_element_type=F32)


def _sigmoid(x):
    return 1.0 / (1.0 + jnp.exp(-x))


def _block_diag(a, b):
    z = jnp.zeros_like(a)
    return jnp.concatenate([jnp.concatenate([a, z], axis=1), jnp.concatenate([z, b], axis=1)], axis=0)


def _cumsum_matrix():
    t = np.arange(CHUNK)[:, None]
    s = np.arange(CHUNK)[None, :]
    return (s <= t).astype(np.float32)


def _hgrn_tile(z_ref, ts, cm, lb_all, hg_all, st_ref, mixed_ref, rhs_ref):
    n_chunks = ts // CHUNK
    qp = z_ref[:, 0:KW]
    fp = z_ref[:, KW:2 * KW]
    gp = z_ref[:, 3 * KW:4 * KW]

    f = lb_all + (1.0 - lb_all) * _sigmoid(fp)
    kk_all = 1.0 - f
    q_all = qp * _sigmoid(qp)
    gate_all = gp * _sigmoid(gp)

    lf2 = jnp.log(f) * LOG2E
    hi = lf2.astype(BF16)
    lo = (lf2 - hi.astype(F32)).astype(BF16)
    cm2 = jnp.concatenate([cm, cm], axis=1)
    b_chunks = []
    for ci in range(n_chunks):
        r = slice(ci * CHUNK, (ci + 1) * CHUNK)
        b_chunks.append(_dot(cm2, jnp.concatenate([hi[r], lo[r]], axis=0)))
    yield "prep"

    row = lax.broadcasted_iota(jnp.int32, (CHUNK, PW), 0)
    upper = {L: jnp.bitwise_and(row, 2 * L - 1) >= L for L in LEVELS}
    sign = {L: jnp.where(upper[L], 1.0, -1.0) for L in LEVELS if 1 < L < 8}
    up_f = {L: jnp.where(upper[L], 1.0, 0.0) for L in LEVELS if L < 8}
    lo_f = {L: jnp.where(upper[L], 0.0, 1.0) for L in LEVELS if L < 8}
    out_row = lax.broadcasted_iota(jnp.int32, (CHUNK, DK), 0)
    out_col = jnp.bitwise_and(lax.broadcasted_iota(jnp.int32, (CHUNK, DK), 1), CHUNK - 1)
    same_block = {L: jnp.where(jnp.bitwise_and(jnp.bitwise_xor(out_row, out_col), -2 * L) == 0, 1.0, 0.0)
                  for L in LEVELS[1:]}
    diagonal = jnp.where(out_row == out_col, 1.0, 0.0)

    def ref_rows(a, first, period, width):
        parts = [jnp.broadcast_to(a[r:r + 1, :], (width, a.shape[1])) for r in range(first, CHUNK, period)]
        return parts[0] if len(parts) == 1 else jnp.concatenate(parts, axis=0)

    def level_ref(b, L):
        if L >= 4:
            return ref_rows(b, L - 1, 2 * L, 2 * L)
        return jnp.where(jnp.bitwise_and(row, 7) < 4, ref_rows(b, 1, 8, 8), ref_rows(b, 5, 8, 8))

    rhs_uses = [0]

    def pair_dot(qe, ke):
        slot = rhs_uses[0] % rhs_ref.shape[0]
        rhs_uses[0] += 1
        ke = ke.astype(BF16)
        rhs_ref[slot] = _block_diag(ke[:, :DK], ke[:, DK:]).T
        return _dot(qe.astype(BF16), rhs_ref[slot])

    def value_dot(entry):
        level_scores, v_bf = entry[5], entry[6]
        scores = level_scores[-1] * diagonal
        for L, s_l in zip(LEVELS[1:], level_scores[1:-1]):
            scores = scores + s_l * same_block[L]
        scores = scores + level_scores[0]
        return _dot(scores.astype(BF16), _block_diag(v_bf[:, :DK], v_bf[:, DK:]))

    work = []
    o_intra = []
    for ci in range(n_chunks):
        r = slice(ci * CHUNK, (ci + 1) * CHUNK)
        for hp in range(HEADS // 2):
            pc = slice(hp * PW, (hp + 1) * PW)
            q, kk, b = q_all[r, pc], kk_all[r, pc], b_chunks[ci][:, pc]
            b_last = b[CHUNK - 1:CHUNK, :]
            v_bf = z_ref[r, 2 * KW + hp * PW:2 * KW + (hp + 1) * PW].astype(BF16)

            qe_bf = (q * jnp.exp2(b)).astype(BF16)
            k_dec = (kk * jnp.exp2(b_last - b)).astype(BF16)
            upd = [_dot_tn(v_bf[:, i * DK:(i + 1) * DK], k_dec[:, i * DK:(i + 1) * DK]) for i in range(2)]

            level_scores = []
            for L in LEVELS:
                if L >= 8:
                    ref = level_ref(b, L)
                    zeros = jnp.zeros((L, PW), F32)
                    q_runs, k_runs = [], []
                    for r0 in range(0, CHUNK, L):
                        run = slice(r0, r0 + L)
                        if (r0 // L) % 2:
                            q_runs.append(q[run] * jnp.exp2(b[run] - ref[run]))
                            k_runs.append(zeros)
                        else:
                            q_runs.append(zeros)
                            k_runs.append(kk[run] * jnp.exp2(ref[run] - b[run]))
                    level_scores.append(pair_dot(jnp.concatenate(q_runs, axis=0),
                                                 jnp.concatenate(k_runs, axis=0)))
                    continue
                if L == 1:
                    qe, ke = q * f[r, pc] * up_f[L], kk * lo_f[L]
                else:
                    e = jnp.exp2((b - level_ref(b, L)) * sign[L])
                    qe, ke = q * up_f[L] * e, kk * lo_f[L] * e
                level_scores.append(pair_dot(qe, ke))
            level_scores.append(pair_dot(q, kk))
            work.append((ci, hp, qe_bf, jnp.exp2(b_last), upd, level_scores, v_bf))
            if len(work) > 1:
                o_intra.append(value_dot(work[-2]))
            yield "levels"
    o_intra.append(value_dot(work[-1]))
    yield "levels"

    state = [st_ref[h] for h in range(HEADS)]
    o_inter = []
    for ci, hp, qe_bf, decay, upd, level_scores, v_bf in work:
        h0, h1 = 2 * hp, 2 * hp + 1
        o_inter.append(_dot_nt(qe_bf, _block_diag(state[h0].astype(BF16), state[h1].astype(BF16))))
        state[h0] = state[h0] * decay[:, :DK] + upd[0]
        state[h1] = state[h1] * decay[:, DK:] + upd[1]
    for h in range(HEADS):
        st_ref[h] = state[h]
    yield "state"

    for (ci, hp, *_), oa, ob in zip(work, o_inter, o_intra):
        r = slice(ci * CHUNK, (ci + 1) * CHUNK)
        o_pair = ob + oa
        for i in range(2):
            cols = slice((2 * hp + i) * DK, (2 * hp + i + 1) * DK)
            o = o_pair[:, i * DK:(i + 1) * DK]
            o = o * lax.rsqrt(jnp.mean(o * o, axis=-1, keepdims=True) + EPS) * hg_all[:, cols]
            mixed_ref[r, cols] = o * gate_all[r, cols]
        yield "norm"


def _mix_kernel(xn_ref, xp_ref, g_ref, w_in_f32, lbl_ref, hg_ref, w_pool_f32, ps_ref, w_out_f32, cm_ref,
                out_ref, za_ref, zb_ref, st_ref, pext_ref, mixed_ref, w_in_ref, w_pool_ref, w_out_ref, rhs_ref,
                *, tiles_per_seq):
    g = pl.program_id(0)
    si = lax.rem(g + tiles_per_seq - 1, tiles_per_seq)

    @pl.when(g == 0)
    def _():
        w_in_ref[...] = w_in_f32[...].astype(BF16)
        w_pool_ref[...] = w_pool_f32[...].astype(BF16)
        w_out_ref[...] = w_out_f32[...].astype(BF16)
        za_ref[...] = _dot(_rms(xn_ref[0], g_ref[...]).astype(BF16), w_in_ref[...])

    @pl.when(si == 0)
    def _():
        st_ref[...] = jnp.zeros_like(st_ref)
        pext_ref[0:POOL_HIST, :] = jnp.zeros((POOL_HIST, KW), F32)

    args = (xn_ref, xp_ref, g_ref, w_in_ref, lbl_ref, hg_ref, w_pool_ref, ps_ref, w_out_ref, cm_ref,
            out_ref, st_ref, pext_ref, mixed_ref, rhs_ref, si)

    @pl.when(jnp.logical_and(lax.rem(g, 2) == 0, g > 0))
    def _():
        _mix_step(*args, zn_ref=za_ref, zc_ref=zb_ref)

    @pl.when(lax.rem(g, 2) == 1)
    def _():
        _mix_step(*args, zn_ref=zb_ref, zc_ref=za_ref)


def _mix_step(xn_ref, xp_ref, g_ref, w_in_ref, lbl_ref, hg_ref, w_pool_ref, ps_ref, w_out_ref, cm_ref,
              out_ref, st_ref, pext_ref, mixed_ref, rhs_ref, si, *, zn_ref, zc_ref):
    ts = xn_ref.shape[1]
    d = xn_ref.shape[2]
    n_sub = ts // SUB_TILE
    hn_bf = _rms(xn_ref[0], g_ref[...]).astype(BF16)

    def in_proj_unit(blk):
        def run():
            cols = slice(blk * IN_BLOCK, (blk + 1) * IN_BLOCK)
            zn_ref[:, cols] = _dot(hn_bf, w_in_ref[:, cols])
        return run

    def out_proj_unit(sub, blk):
        def run():
            r = slice(sub * SUB_TILE, (sub + 1) * SUB_TILE)
            cols = slice(blk * IN_BLOCK, (blk + 1) * IN_BLOCK)
            out_ref[0, r, cols] = xp_ref[0, r, cols] + _dot(mixed_ref[r, :].astype(BF16), w_out_ref[:, cols])
        return run

    in_units = [in_proj_unit(blk) for blk in range(IN_WIDTH // IN_BLOCK)]

    l0 = lbl_ref[0:1, :]
    l1 = lbl_ref[1:2, :]
    lmax = jnp.maximum(l0, l1)
    e0 = jnp.exp(l0 - lmax)
    lb_all = e0 / (e0 + jnp.exp(l1 - lmax))

    def pool(sub):
        r = slice(sub * SUB_TILE, (sub + 1) * SUB_TILE)
        p = zc_ref[r, 4 * KW:5 * KW]
        pext_ref[POOL_HIST:, :] = p
        pos = (si * ts + sub * SUB_TILE + 1 + lax.broadcasted_iota(jnp.int32, (SUB_TILE, 1), 0)).astype(F32)
        for gi, w in enumerate(POOL_WINDOWS):
            cols = slice(gi * DK, (gi + 1) * DK)
            acc = pext_ref[:, cols]
            sh = 1
            while sh < w:
                acc = acc + pltpu.roll(acc, sh, 0)
                sh *= 2
            win = acc[POOL_HIST:, :]
            pooled = win / jnp.minimum(pos, float(w)) - p[:, cols]
            y = _dot(pooled.astype(BF16), w_pool_ref[gi]) * ps_ref[:, cols]
            mixed_ref[r, KW + gi * DK:KW + (gi + 1) * DK] = y
        pext_ref[0:POOL_HIST, :] = pext_ref[SUB_TILE:SUB_TILE + POOL_HIST, :]

    def hgrn(sub):
        r = pl.ds(sub * SUB_TILE, SUB_TILE)
        slots = pl.ds(sub * RHS_SLOTS, RHS_SLOTS)
        return _hgrn_tile(zc_ref.at[r], SUB_TILE, cm_ref[...], lb_all, hg_ref[...], st_ref, mixed_ref.at[r],
                          rhs_ref.at[slots])

    def step(gen, label):
        got = next(gen)
        assert got == label, (got, label)

    def fill(n=1):
        for _ in range(n):
            if in_units:
                in_units.pop(0)()

    n_items = (SUB_TILE // CHUNK) * (HEADS // 2)
    gens = [hgrn(sub) for sub in range(n_sub)]
    slot_every = max(1, n_items // 4)
    out_units = []
    step(gens[0], "prep")
    fill(3)
    for i in range(n_items + 1):
        step(gens[0], "levels")
        if i % slot_every == slot_every - 1:
            fill()
    for sub in range(n_sub):
        cur = gens[sub]
        nxt = gens[sub + 1] if sub + 1 < n_sub else None
        if nxt is not None:
            step(nxt, "prep")
        step(cur, "state")
        for i in range(n_items):
            if nxt is not None:
                step(nxt, "levels")
            step(cur, "norm")
            if i % slot_every == slot_every - 1:
                (out_units.pop(0) if out_units else fill)()
        if nxt is not None:
            step(nxt, "levels")
        pool(sub)
        out_units += [out_proj_unit(sub, blk) for blk in range(d // IN_BLOCK)]
    fill(len(in_units))
    for unit in out_units:
        unit()


def _const_spec(shape):
    nd = len(shape)
    return pl.BlockSpec(shape, lambda *_: (0,) * nd, pipeline_mode=pl.Buffered(1))


def _mix_call(x, g, w_in, lbl, hg, w_pool, ps, w_out):
    bsz, seq, d = x.shape
    ts = TS_MIX
    ns = seq // ts
    n_tiles = bsz * ns
    cm = jnp.asarray(_cumsum_matrix(), BF16)

    def tile_next(i):
        t = jnp.minimum(i, n_tiles - 1)
        return (t // ns, t % ns, 0)

    def tile_prev(i):
        t = jnp.maximum(i - 1, 0)
        return (t // ns, t % ns, 0)

    return pl.pallas_call(
        functools.partial(_mix_kernel, tiles_per_seq=ns),
        grid=(n_tiles + 1,),
        in_specs=[
            pl.BlockSpec((1, ts, d), tile_next),
            pl.BlockSpec((1, ts, d), tile_prev),
            _const_spec((1, d)),
            _const_spec((d, IN_WIDTH)),
            _const_spec((2, KW)),
            _const_spec((1, KW)),
            _const_spec((len(POOL_WINDOWS), DK, DK)),
            _const_spec((1, KW)),
            _const_spec((d, d)),
            _const_spec((CHUNK, CHUNK)),
        ],
        out_specs=pl.BlockSpec((1, ts, d), tile_prev),
        out_shape=jax.ShapeDtypeStruct(x.shape, F32),
        scratch_shapes=[
            pltpu.VMEM((ts, IN_WIDTH), F32),
            pltpu.VMEM((ts, IN_WIDTH), F32),
            pltpu.VMEM((HEADS, DK, DK), F32),
            pltpu.VMEM((POOL_HIST + SUB_TILE, KW), F32),
            pltpu.VMEM((ts, d), F32),
            pltpu.VMEM((d, IN_WIDTH), BF16),
            pltpu.VMEM((len(POOL_WINDOWS), DK, DK), BF16),
            pltpu.VMEM((d, d), BF16),
            pltpu.VMEM(((ts // SUB_TILE) * RHS_SLOTS, PW, 2 * CHUNK), BF16),
        ],
        compiler_params=pltpu.CompilerParams(
            dimension_semantics=("arbitrary",), vmem_limit_bytes=MIX_VMEM_LIMIT),
        name="mix",
    )(x, x, g, w_in, lbl, hg, w_pool, ps, w_out, cm)


def _memkv_kernel(mem_ref, g_ref, wq_ref, wk_ref, wv_ref, kt_ref, v_ref, wq_out_ref, wk_bf, wv_bf):
    def dense(w_ref):
        return jnp.concatenate([w_ref[:, h, :] for h in range(XHEADS)], axis=1).astype(BF16)

    @pl.when(pl.program_id(0) == 0)
    def _():
        wq_out_ref[...] = dense(wq_ref)
        wk_bf[...] = dense(wk_ref)
        wv_bf[...] = dense(wv_ref)

    hm = _rms(mem_ref[0], g_ref[...]).astype(BF16)
    k = _dot(hm, wk_bf[...]) * (XDIM ** -0.5)
    kt_ref[0] = k.T.astype(BF16)
    v_ref[0] = _dot(hm, wv_bf[...]).astype(BF16)


def _memkv_call(mem, g, wq, wk, wv):
    bsz, m, d = mem.shape
    return pl.pallas_call(
        _memkv_kernel,
        grid=(bsz,),
        in_specs=[
            pl.BlockSpec((1, m, d), lambda b: (b, 0, 0)),
            _const_spec((1, d)),
            _const_spec((d, XHEADS, XDIM)),
            _const_spec((d, XHEADS, XDIM)),
            _const_spec((d, XHEADS, XDIM)),
        ],
        out_specs=[
            pl.BlockSpec((1, d, m), lambda b: (b, 0, 0)),
            pl.BlockSpec((1, m, d), lambda b: (b, 0, 0)),
            pl.BlockSpec((d, d), lambda b: (0, 0)),
        ],
        out_shape=[
            jax.ShapeDtypeStruct((bsz, d, m), BF16),
            jax.ShapeDtypeStruct((bsz, m, d), BF16),
            jax.ShapeDtypeStruct((d, d), BF16),
        ],
        scratch_shapes=[pltpu.VMEM((d, d), BF16), pltpu.VMEM((d, d), BF16)],
        compiler_params=pltpu.CompilerParams(
            dimension_semantics=("arbitrary",), vmem_limit_bytes=VMEM_LIMIT),
        name="memkv",
    )(mem, g, wq, wk, wv)


def _tail_kernel(x_ref, gx_ref, wq_ref, kt_ref, v_ref, wo_ref, gf_ref, w1_ref, w2_ref, gl_ref, out_ref):
    hm = x_ref.shape[1] // 2
    head_cols = [slice(h * XDIM, (h + 1) * XDIM) for h in range(XHEADS)]
    rows = [slice(i * hm, (i + 1) * hm) for i in range(2)]
    xs = [x_ref[0, r, :] for r in rows]

    def q_proj(x):
        return _dot(_rms(x, gx_ref[...]).astype(BF16), wq_ref[...]).astype(BF16)

    def scores(xq):
        return [_dot(xq[:, c], kt_ref[0, c, :]) for c in head_cols]

    def attend(ss):
        outs = []
        for s, c in zip(ss, head_cols):
            e = jnp.exp(s - jnp.max(s, axis=-1, keepdims=True))
            p = e * (1.0 / jnp.sum(e, axis=-1, keepdims=True))
            outs.append(_dot(p.astype(BF16), v_ref[0, :, c]).astype(BF16))
        return jnp.concatenate(outs, axis=1)

    def o_proj(x, att):
        return x + _dot(att, wo_ref[...].astype(BF16))

    def mlp(i, x):
        hf = _rms(x, gf_ref[...]).astype(BF16)
        acc = x
        for j in range(D_FF // FF_BLOCK):
            cols = slice(j * FF_BLOCK, (j + 1) * FF_BLOCK)
            u = jnp.maximum(_dot(hf, w1_ref[:, cols].astype(BF16)), 0.0)
            acc = acc + _dot((u * u).astype(BF16), w2_ref[cols, :].astype(BF16))
        out_ref[0, rows[i], :] = _rms(acc, gl_ref[...])

    s_a = scores(q_proj(xs[0]))
    xq_b = q_proj(xs[1])
    att_a = attend(s_a)
    s_b = scores(xq_b)
    x2_a = o_proj(xs[0], att_a)
    x2_b = o_proj(xs[1], attend(s_b))
    mlp(0, x2_a)
    mlp(1, x2_b)


def _tail_call(x, gx, wq, kt, v, wo, gf, w1, w2, gl):
    bsz, seq, d = x.shape
    tm = TM_TAIL
    return pl.pallas_call(
        _tail_kernel,
        grid=(bsz, seq // tm),
        in_specs=[
            pl.BlockSpec((1, tm, d), lambda b, s: (b, s, 0)),
            _const_spec((1, d)),
            _const_spec((d, d)),
            pl.BlockSpec((1, d, MEM_LEN), lambda b, s: (b, 0, 0)),
            pl.BlockSpec((1, MEM_LEN, d), lambda b, s: (b, 0, 0)),
            _const_spec((d, d)),
            _const_spec((1, d)),
            _const_spec((d, D_FF)),
            _const_spec((D_FF, d)),
            _const_spec((1, d)),
        ],
        out_specs=pl.BlockSpec((1, tm, d), lambda b, s: (b, s, 0)),
        out_shape=jax.ShapeDtypeStruct(x.shape, F32),
        compiler_params=pltpu.CompilerParams(
            dimension_semantics=("arbitrary", "arbitrary"), vmem_limit_bytes=TAIL_VMEM_LIMIT),
        name="tail",
    )(x, gx, wq, kt, v, wo, gf, w1, w2, gl)


def kernel(x, mem, norm_mix_g, w_in, lb_logits, hgrn_norm_g, w_pool, pool_scale, w_out,
           norm_x_g, norm_mem_g, w_xq, w_xk, w_xv, w_xo, norm_ffn_g, w_ff1, w_ff2, final_norm_g):
    bsz, seq, d = x.shape
    depth = norm_mix_g.shape[0]
    assert depth == 1 and lb_logits.shape[0] == 2
    row = lambda a: a.reshape(1, -1).astype(F32)
    x = _mix_call(x, row(norm_mix_g), w_in.reshape(d, IN_WIDTH), lb_logits.astype(F32),
                  row(hgrn_norm_g), w_pool.reshape(len(POOL_WINDOWS), DK, DK), row(pool_scale),
                  w_out.reshape(d, d))
    split = (d, XHEADS, XDIM)
    kt, v, wq = _memkv_call(mem, row(norm_mem_g), w_xq.reshape(split), w_xk.reshape(split), w_xv.reshape(split))
    return _tail_call(x, row(norm_x_g), wq, kt, v, w_xo.reshape(d, d),
                      row(norm_ffn_g), w_ff1.reshape(d, D_FF), w_ff2.reshape(D_FF, d), row(final_norm_g))
```

```python
import functools

import jax
import jax.numpy as jnp
import numpy as np
from jax import lax
from jax.experimental import pallas as pl
from jax.experimental.pallas import tpu as pltpu

HEADS = 4
DK = 128
KW = HEADS * DK
PW = 2 * DK
CHUNK = 64
LEVELS = (32, 16, 8, 4, 2, 1)
POOL_WINDOWS = (2, 4, 8, 16)
POOL_HIST = 16
IN_WIDTH = 5 * KW
MEM_LEN = 256
XHEADS = 4
XDIM = 256
D_FF = 4096
EPS = 1e-6
LOG2E = 1.4426950408889634

TS_MIX = 512
SUB_TILE = 256
IN_BLOCK = 256
TM_TAIL = 512
FF_BLOCK = 1024
VMEM_LIMIT = 56 * 1024 * 1024
TAIL_VMEM_LIMIT = 60 * 1024 * 1024

F32 = jnp.float32
BF16 = jnp.bfloat16


def _rms(x, g):
    return x * lax.rsqrt(jnp.mean(x * x, axis=-1, keepdims=True) + EPS) * g


def _dot(a, b):
    return jnp.dot(a, b, preferred_element_type=F32)


def _dot_nt(a, b):
    return lax.dot_general(a, b, (((1,), (1,)), ((), ())), preferred_element_type=F32)


def _dot_tn(a, b):
    return lax.dot_general(a, b, (((0,), (0,)), ((), ())), preferred_element_type=F32)


def _sigmoid(x):
    return 1.0 / (1.0 + jnp.exp(-x))


def _block_diag(a, b):
    z = jnp.zeros_like(a)
    return jnp.concatenate([jnp.concatenate([a, z], axis=1), jnp.concatenate([z, b], axis=1)], axis=0)


def _cumsum_matrix():
    t = np.arange(CHUNK)[:, None]
    s = np.arange(CHUNK)[None, :]
    return (s <= t).astype(np.float32)


def _hgrn_tile(z_ref, ts, cm, lb_all, hg_all, st_ref, mixed_ref):
    n_chunks = ts // CHUNK
    qp = z_ref[:, 0:KW]
    fp = z_ref[:, KW:2 * KW]
    gp = z_ref[:, 3 * KW:4 * KW]

    f = lb_all + (1.0 - lb_all) * _sigmoid(fp)
    kk_all = 1.0 - f
    q_all = qp * _sigmoid(qp)
    gate_all = gp * _sigmoid(gp)

    lf2 = jnp.log(f) * LOG2E
    hi = lf2.astype(BF16)
    lo = (lf2 - hi.astype(F32)).astype(BF16)
    cm2 = jnp.concatenate([cm, cm], axis=1)
    b_chunks = []
    for ci in range(n_chunks):
        r = slice(ci * CHUNK, (ci + 1) * CHUNK)
        b_chunks.append(_dot(cm2, jnp.concatenate([hi[r], lo[r]], axis=0)))
    yield "prep"

    row = lax.broadcasted_iota(jnp.int32, (CHUNK, PW), 0)
    upper = {L: jnp.bitwise_and(row, 2 * L - 1) >= L for L in LEVELS}
    sign = {L: jnp.where(upper[L], 1.0, -1.0) for L in LEVELS if 1 < L < 8}
    up_f = {L: jnp.where(upper[L], 1.0, 0.0) for L in LEVELS if L < 8}
    lo_f = {L: jnp.where(upper[L], 0.0, 1.0) for L in LEVELS if L < 8}
    out_row = lax.broadcasted_iota(jnp.int32, (CHUNK, DK), 0)
    out_col = jnp.bitwise_and(lax.broadcasted_iota(jnp.int32, (CHUNK, DK), 1), CHUNK - 1)
    same_block = {L: jnp.where(jnp.bitwise_and(jnp.bitwise_xor(out_row, out_col), -2 * L) == 0, 1.0, 0.0)
                  for L in LEVELS[1:]}
    diagonal = jnp.where(out_row == out_col, 1.0, 0.0)

    def ref_rows(a, first, period, width):
        parts = [jnp.broadcast_to(a[r:r + 1, :], (width, a.shape[1])) for r in range(first, CHUNK, period)]
        return parts[0] if len(parts) == 1 else jnp.concatenate(parts, axis=0)

    def level_ref(b, L):
        if L >= 4:
            return ref_rows(b, L - 1, 2 * L, 2 * L)
        return jnp.where(jnp.bitwise_and(row, 7) < 4, ref_rows(b, 1, 8, 8), ref_rows(b, 5, 8, 8))

    def pair_dot(qe, ke):
        rhs = _block_diag(ke[:, :DK], ke[:, DK:]).T.astype(BF16)
        return _dot(qe.astype(BF16), rhs)

    def value_dot(entry):
        level_scores, v_bf = entry[5], entry[6]
        scores = level_scores[-1] * diagonal
        for L, s_l in zip(LEVELS[1:], level_scores[1:-1]):
            scores = scores + s_l * same_block[L]
        scores = scores + level_scores[0]
        return _dot(scores.astype(BF16), _block_diag(v_bf[:, :DK], v_bf[:, DK:]))

    work = []
    o_intra = []
    for ci in range(n_chunks):
        r = slice(ci * CHUNK, (ci + 1) * CHUNK)
        for hp in range(HEADS // 2):
            pc = slice(hp * PW, (hp + 1) * PW)
            q, kk, b = q_all[r, pc], kk_all[r, pc], b_chunks[ci][:, pc]
            b_last = b[CHUNK - 1:CHUNK, :]
            v_bf = z_ref[r, 2 * KW + hp * PW:2 * KW + (hp + 1) * PW].astype(BF16)

            qe_bf = (q * jnp.exp2(b)).astype(BF16)
            k_dec = (kk * jnp.exp2(b_last - b)).astype(BF16)
            upd = [_dot_tn(v_bf[:, i * DK:(i + 1) * DK], k_dec[:, i * DK:(i + 1) * DK]) for i in range(2)]

            level_scores = []
            for L in LEVELS:
                if L >= 8:
                    ref = level_ref(b, L)
                    zeros = jnp.zeros((L, PW), F32)
                    q_runs, k_runs = [], []
                    for r0 in range(0, CHUNK, L):
                        run = slice(r0, r0 + L)
                        if (r0 // L) % 2:
                            q_runs.append(q[run] * jnp.exp2(b[run] - ref[run]))
                            k_runs.append(zeros)
                        else:
                            q_runs.append(zeros)
                            k_runs.append(kk[run] * jnp.exp2(ref[run] - b[run]))
                    level_scores.append(pair_dot(jnp.concatenate(q_runs, axis=0),
                                                 jnp.concatenate(k_runs, axis=0)))
                    continue
                if L == 1:
                    qe, ke = q * f[r, pc] * up_f[L], kk * lo_f[L]
                else:
                    e = jnp.exp2((b - level_ref(b, L)) * sign[L])
                    qe, ke = q * up_f[L] * e, kk * lo_f[L] * e
                level_scores.append(pair_dot(qe, ke))
            level_scores.append(pair_dot(q, kk))
            work.append((ci, hp, qe_bf, jnp.exp2(b_last), upd, level_scores, v_bf))
            if len(work) > 1:
                o_intra.append(value_dot(work[-2]))
            yield "levels"
    o_intra.append(value_dot(work[-1]))
    yield "levels"

    state = [st_ref[h] for h in range(HEADS)]
    o_inter = []
    for ci, hp, qe_bf, decay, upd, level_scores, v_bf in work:
        h0, h1 = 2 * hp, 2 * hp + 1
        o_inter.append(_dot_nt(qe_bf, _block_diag(state[h0].astype(BF16), state[h1].astype(BF16))))
        state[h0] = state[h0] * decay[:, :DK] + upd[0]
        state[h1] = state[h1] * decay[:, DK:] + upd[1]
    for h in range(HEADS):
        st_ref[h] = state[h]
    yield "state"

    for (ci, hp, *_), oa, ob in zip(work, o_inter, o_intra):
        r = slice(ci * CHUNK, (ci + 1) * CHUNK)
        o_pair = ob + oa
        for i in range(2):
            cols = slice((2 * hp + i) * DK, (2 * hp + i + 1) * DK)
            o = o_pair[:, i * DK:(i + 1) * DK]
            o = o * lax.rsqrt(jnp.mean(o * o, axis=-1, keepdims=True) + EPS) * hg_all[:, cols]
            mixed_ref[r, cols] = o * gate_all[r, cols]
        yield "norm"


def _mix_kernel(xn_ref, xp_ref, g_ref, w_in_f32, lbl_ref, hg_ref, w_pool_f32, ps_ref, w_out_f32, cm_ref,
                out_ref, za_ref, zb_ref, st_ref, pext_ref, mixed_ref, w_in_ref, w_pool_ref, w_out_ref,
                *, tiles_per_seq):
    g = pl.program_id(0)
    si = lax.rem(g + tiles_per_seq - 1, tiles_per_seq)

    @pl.when(g == 0)
    def _():
        w_in_ref[...] = w_in_f32[...].astype(BF16)
        w_pool_ref[...] = w_pool_f32[...].astype(BF16)
        w_out_ref[...] = w_out_f32[...].astype(BF16)
        za_ref[...] = _dot(_rms(xn_ref[0], g_ref[...]).astype(BF16), w_in_ref[...])

    @pl.when(si == 0)
    def _():
        st_ref[...] = jnp.zeros_like(st_ref)
        pext_ref[0:POOL_HIST, :] = jnp.zeros((POOL_HIST, KW), F32)

    args = (xn_ref, xp_ref, g_ref, w_in_ref, lbl_ref, hg_ref, w_pool_ref, ps_ref, w_out_ref, cm_ref,
            out_ref, st_ref, pext_ref, mixed_ref, si)

    @pl.when(jnp.logical_and(lax.rem(g, 2) == 0, g > 0))
    def _():
        _mix_step(*args, zn_ref=za_ref, zc_ref=zb_ref)

    @pl.when(lax.rem(g, 2) == 1)
    def _():
        _mix_step(*args, zn_ref=zb_ref, zc_ref=za_ref)


def _mix_step(xn_ref, xp_ref, g_ref, w_in_ref, lbl_ref, hg_ref, w_pool_ref, ps_ref, w_out_ref, cm_ref,
              out_ref, st_ref, pext_ref, mixed_ref, si, *, zn_ref, zc_ref):
    ts = xn_ref.shape[1]
    d = xn_ref.shape[2]
    n_sub = ts // SUB_TILE
    hn_bf = _rms(xn_ref[0], g_ref[...]).astype(BF16)

    def in_proj_unit(blk):
        def run():
            cols = slice(blk * IN_BLOCK, (blk + 1) * IN_BLOCK)
            zn_ref[:, cols] = _dot(hn_bf, w_in_ref[:, cols])
        return run

    def out_proj_unit(sub, blk):
        def run():
            r = slice(sub * SUB_TILE, (sub + 1) * SUB_TILE)
            cols = slice(blk * IN_BLOCK, (blk + 1) * IN_BLOCK)
            out_ref[0, r, cols] = xp_ref[0, r, cols] + _dot(mixed_ref[r, :].astype(BF16), w_out_ref[:, cols])
        return run

    in_units = [in_proj_unit(blk) for blk in range(IN_WIDTH // IN_BLOCK)]

    l0 = lbl_ref[0:1, :]
    l1 = lbl_ref[1:2, :]
    lmax = jnp.maximum(l0, l1)
    e0 = jnp.exp(l0 - lmax)
    lb_all = e0 / (e0 + jnp.exp(l1 - lmax))

    def pool(sub):
        r = slice(sub * SUB_TILE, (sub + 1) * SUB_TILE)
        p = zc_ref[r, 4 * KW:5 * KW]
        pext_ref[POOL_HIST:, :] = p
        pos = (si * ts + sub * SUB_TILE + 1 + lax.broadcasted_iota(jnp.int32, (SUB_TILE, 1), 0)).astype(F32)
        for gi, w in enumerate(POOL_WINDOWS):
            cols = slice(gi * DK, (gi + 1) * DK)
            acc = pext_ref[:, cols]
            sh = 1
            while sh < w:
                acc = acc + pltpu.roll(acc, sh, 0)
                sh *= 2
            win = acc[POOL_HIST:, :]
            pooled = win / jnp.minimum(pos, float(w)) - p[:, cols]
            y = _dot(pooled.astype(BF16), w_pool_ref[gi]) * ps_ref[:, cols]
            mixed_ref[r, KW + gi * DK:KW + (gi + 1) * DK] = y
        pext_ref[0:POOL_HIST, :] = pext_ref[SUB_TILE:SUB_TILE + POOL_HIST, :]

    def hgrn(sub):
        r = pl.ds(sub * SUB_TILE, SUB_TILE)
        return _hgrn_tile(zc_ref.at[r], SUB_TILE, cm_ref[...], lb_all, hg_ref[...], st_ref, mixed_ref.at[r])

    def step(gen, label):
        got = next(gen)
        assert got == label, (got, label)

    def fill(n=1):
        for _ in range(n):
            if in_units:
                in_units.pop(0)()

    n_items = (SUB_TILE // CHUNK) * (HEADS // 2)
    gens = [hgrn(sub) for sub in range(n_sub)]
    slot_every = max(1, n_items // 4)
    out_units = []
    step(gens[0], "prep")
    fill(3)
    for i in range(n_items + 1):
        step(gens[0], "levels")
        if i % slot_every == slot_every - 1:
            fill()
    for sub in range(n_sub):
        cur = gens[sub]
        nxt = gens[sub + 1] if sub + 1 < n_sub else None
        if nxt is not None:
            step(nxt, "prep")
        step(cur, "state")
        for i in range(n_items):
            if nxt is not None:
                step(nxt, "levels")
            step(cur, "norm")
            if i % slot_every == slot_every - 1:
                (out_units.pop(0) if out_units else fill)()
        if nxt is not None:
            step(nxt, "levels")
        pool(sub)
        out_units += [out_proj_unit(sub, blk) for blk in range(d // IN_BLOCK)]
    fill(len(in_units))
    for unit in out_units:
        unit()


def _const_spec(shape):
    nd = len(shape)
    return pl.BlockSpec(shape, lambda *_: (0,) * nd, pipeline_mode=pl.Buffered(1))


def _mix_call(x, g, w_in, lbl, hg, w_pool, ps, w_out):
    bsz, seq, d = x.shape
    ts = TS_MIX
    ns = seq // ts
    n_tiles = bsz * ns
    cm = jnp.asarray(_cumsum_matrix(), BF16)

    def tile_next(i):
        t = jnp.minimum(i, n_tiles - 1)
        return (t // ns, t % ns, 0)

    def tile_prev(i):
        t = jnp.maximum(i - 1, 0)
        return (t // ns, t % ns, 0)

    return pl.pallas_call(
        functools.partial(_mix_kernel, tiles_per_seq=ns),
        grid=(n_tiles + 1,),
        in_specs=[
            pl.BlockSpec((1, ts, d), tile_next),
            pl.BlockSpec((1, ts, d), tile_prev),
            _const_spec((1, d)),
            _const_spec((d, IN_WIDTH)),
            _const_spec((2, KW)),
            _const_spec((1, KW)),
            _const_spec((len(POOL_WINDOWS), DK, DK)),
            _const_spec((1, KW)),
            _const_spec((d, d)),
            _const_spec((CHUNK, CHUNK)),
        ],
        out_specs=pl.BlockSpec((1, ts, d), tile_prev),
        out_shape=jax.ShapeDtypeStruct(x.shape, F32),
        scratch_shapes=[
            pltpu.VMEM((ts, IN_WIDTH), F32),
            pltpu.VMEM((ts, IN_WIDTH), F32),
            pltpu.VMEM((HEADS, DK, DK), F32),
            pltpu.VMEM((POOL_HIST + SUB_TILE, KW), F32),
            pltpu.VMEM((ts, d), F32),
            pltpu.VMEM((d, IN_WIDTH), BF16),
            pltpu.VMEM((len(POOL_WINDOWS), DK, DK), BF16),
            pltpu.VMEM((d, d), BF16),
        ],
        compiler_params=pltpu.CompilerParams(
            dimension_semantics=("arbitrary",), vmem_limit_bytes=VMEM_LIMIT),
        name="mix",
    )(x, x, g, w_in, lbl, hg, w_pool, ps, w_out, cm)


def _memkv_kernel(mem_ref, g_ref, wq_ref, wk_ref, wv_ref, kt_ref, v_ref, wq_out_ref, wk_bf, wv_bf):
    def dense(w_ref):
        return jnp.concatenate([w_ref[:, h, :] for h in range(XHEADS)], axis=1).astype(BF16)

    @pl.when(pl.program_id(0) == 0)
    def _():
        wq_out_ref[...] = dense(wq_ref)
        wk_bf[...] = dense(wk_ref)
        wv_bf[...] = dense(wv_ref)

    hm = _rms(mem_ref[0], g_ref[...]).astype(BF16)
    k = _dot(hm, wk_bf[...]) * (XDIM ** -0.5)
    kt_ref[0] = k.T.astype(BF16)
    v_ref[0] = _dot(hm, wv_bf[...]).astype(BF16)


def _memkv_call(mem, g, wq, wk, wv):
    bsz, m, d = mem.shape
    return pl.pallas_call(
        _memkv_kernel,
        grid=(bsz,),
        in_specs=[
            pl.BlockSpec((1, m, d), lambda b: (b, 0, 0)),
            _const_spec((1, d)),
            _const_spec((d, XHEADS, XDIM)),
            _const_spec((d, XHEADS, XDIM)),
            _const_spec((d, XHEADS, XDIM)),
        ],
        out_specs=[
            pl.BlockSpec((1, d, m), lambda b: (b, 0, 0)),
            pl.BlockSpec((1, m, d), lambda b: (b, 0, 0)),
            pl.BlockSpec((d, d), lambda b: (0, 0)),
        ],
        out_shape=[
            jax.ShapeDtypeStruct((bsz, d, m), BF16),
            jax.ShapeDtypeStruct((bsz, m, d), BF16),
            jax.ShapeDtypeStruct((d, d), BF16),
        ],
        scratch_shapes=[pltpu.VMEM((d, d), BF16), pltpu.VMEM((d, d), BF16)],
        compiler_params=pltpu.CompilerParams(
            dimension_semantics=("arbitrary",), vmem_limit_bytes=VMEM_LIMIT),
        name="memkv",
    )(mem, g, wq, wk, wv)


def _tail_kernel(x_ref, gx_ref, wq_ref, kt_ref, v_ref, wo_ref, gf_ref, w1_ref, w2_ref, gl_ref, out_ref):
    hm = x_ref.shape[1] // 2
    head_cols = [slice(h * XDIM, (h + 1) * XDIM) for h in range(XHEADS)]
    rows = [slice(i * hm, (i + 1) * hm) for i in range(2)]
    xs = [x_ref[0, r, :] for r in rows]

    def q_proj(x):
        return _dot(_rms(x, gx_ref[...]).astype(BF16), wq_ref[...]).astype(BF16)

    def scores(xq):
        return [_dot(xq[:, c], kt_ref[0, c, :]) for c in head_cols]

    def attend(ss):
        outs = []
        for s, c in zip(ss, head_cols):
            e = jnp.exp(s - jnp.max(s, axis=-1, keepdims=True))
            p = e * (1.0 / jnp.sum(e, axis=-1, keepdims=True))
            outs.append(_dot(p.astype(BF16), v_ref[0, :, c]).astype(BF16))
        return jnp.concatenate(outs, axis=1)

    def o_proj(x, att):
        return x + _dot(att, wo_ref[...].astype(BF16))

    def mlp(i, x):
        hf = _rms(x, gf_ref[...]).astype(BF16)
        acc = x
        for j in range(D_FF // FF_BLOCK):
            cols = slice(j * FF_BLOCK, (j + 1) * FF_BLOCK)
            u = jnp.maximum(_dot(hf, w1_ref[:, cols].astype(BF16)), 0.0)
            acc = acc + _dot((u * u).astype(BF16), w2_ref[cols, :].astype(BF16))
        out_ref[0, rows[i], :] = _rms(acc, gl_ref[...])

    s_a = scores(q_proj(xs[0]))
    xq_b = q_proj(xs[1])
    att_a = attend(s_a)
    s_b = scores(xq_b)
    x2_a = o_proj(xs[0], att_a)
    x2_b = o_proj(xs[1], attend(s_b))
    mlp(0, x2_a)
    mlp(1, x2_b)


def _tail_call(x, gx, wq, kt, v, wo, gf, w1, w2, gl):
    bsz, seq, d = x.shape
    tm = TM_TAIL
    return pl.pallas_call(
        _tail_kernel,
        grid=(bsz, seq // tm),
        in_specs=[
            pl.BlockSpec((1, tm, d), lambda b, s: (b, s, 0)),
            _const_spec((1, d)),
            _const_spec((d, d)),
            pl.BlockSpec((1, d, MEM_LEN), lambda b, s: (b, 0, 0)),
            pl.BlockSpec((1, MEM_LEN, d), lambda b, s: (b, 0, 0)),
            _const_spec((d, d)),
            _const_spec((1, d)),
            _const_spec((d, D_FF)),
            _const_spec((D_FF, d)),
            _const_spec((1, d)),
        ],
        out_specs=pl.BlockSpec((1, tm, d), lambda b, s: (b, s, 0)),
        out_shape=jax.ShapeDtypeStruct(x.shape, F32),
        compiler_params=pltpu.CompilerParams(
            dimension_semantics=("arbitrary", "arbitrary"), vmem_limit_bytes=TAIL_VMEM_LIMIT),
        name="tail",
    )(x, gx, wq, kt, v, wo, gf, w1, w2, gl)


def kernel(x, mem, norm_mix_g, w_in, lb_logits, hgrn_norm_g, w_pool, pool_scale, w_out,
           norm_x_g, norm_mem_g, w_xq, w_xk, w_xv, w_xo, norm_ffn_g, w_ff1, w_ff2, final_norm_g):
    bsz, seq, d = x.shape
    depth = norm_mix_g.shape[0]
    assert depth == 1 and lb_logits.shape[0] == 2
    assert seq % TS_MIX == 0 and TS_MIX % SUB_TILE == 0 and seq % TM_TAIL == 0
    assert d == 2 * KW and w_ff1.shape[-1] == D_FF and mem.shape[1] == MEM_LEN
    row = lambda a: a.reshape(1, -1).astype(F32)
    x = _mix_call(x, row(norm_mix_g), w_in.reshape(d, IN_WIDTH), lb_logits.astype(F32),
                  row(hgrn_norm_g), w_pool.reshape(len(POOL_WINDOWS), DK, DK), row(pool_scale),
                  w_out.reshape(d, d))
    split = (d, XHEADS, XDIM)
    kt, v, wq = _memkv_call(mem, row(norm_mem_g), w_xq.reshape(split), w_xk.reshape(split), w_xv.reshape(split))
    return _tail_call(x, row(norm_x_g), wq, kt, v, w_xo.reshape(d, d),
                      row(norm_ffn_g), w_ff1.reshape(d, D_FF), w_ff2.reshape(D_FF, d), row(final_norm_g))
```

```python
import functools

import jax
import jax.numpy as jnp
import numpy as np
from jax import lax
from jax.experimental import pallas as pl
from jax.experimental.pallas import tpu as pltpu

HEADS = 4
DK = 128
KW = HEADS * DK
PW = 2 * DK
CHUNK = 64
LEVELS = (32, 16, 8, 4, 2, 1)
POOL_WINDOWS = (2, 4, 8, 16)
POOL_HIST = 16
IN_WIDTH = 5 * KW
MEM_LEN = 256
XHEADS = 4
XDIM = 256
D_FF = 4096
EPS = 1e-6
LOG2E = 1.4426950408889634

TS_MIX = 512
SUB_TILE = 256
IN_BLOCK = 256
TM_TAIL = 512
FF_BLOCK = 1024
VMEM_LIMIT = 56 * 1024 * 1024
TAIL_VMEM_LIMIT = 60 * 1024 * 1024

F32 = jnp.float32
BF16 = jnp.bfloat16


def _rms(x, g):
    return x * lax.rsqrt(jnp.mean(x * x, axis=-1, keepdims=True) + EPS) * g


def _dot(a, b):
    return jnp.dot(a, b, preferred_element_type=F32)


def _dot_nt(a, b):
    return lax.dot_general(a, b, (((1,), (1,)), ((), ())), preferred_element_type=F32)


def _dot_tn(a, b):
    return lax.dot_general(a, b, (((0,), (0,)), ((), ())), preferred_element_type=F32)


def _sigmoid(x):
    return 1.0 / (1.0 + jnp.exp(-x))


def _block_diag(a, b):
    z = jnp.zeros_like(a)
    return jnp.concatenate([jnp.concatenate([a, z], axis=1), jnp.concatenate([z, b], axis=1)], axis=0)


def _cumsum_matrix():
    t = np.arange(CHUNK)[:, None]
    s = np.arange(CHUNK)[None, :]
    return (s <= t).astype(np.float32)


def _hgrn_tile(z_ref, ts, cm, lb_all, hg_all, st_ref, mixed_ref):
    n_chunks = ts // CHUNK
    qp = z_ref[:, 0:KW]
    fp = z_ref[:, KW:2 * KW]
    gp = z_ref[:, 3 * KW:4 * KW]

    f = lb_all + (1.0 - lb_all) * _sigmoid(fp)
    kk_all = 1.0 - f
    q_all = qp * _sigmoid(qp)
    gate_all = gp * _sigmoid(gp)

    lf2 = jnp.log(f) * LOG2E
    hi = lf2.astype(BF16)
    lo = (lf2 - hi.astype(F32)).astype(BF16)
    cm2 = jnp.concatenate([cm, cm], axis=1)
    b_chunks = []
    for ci in range(n_chunks):
        r = slice(ci * CHUNK, (ci + 1) * CHUNK)
        b_chunks.append(_dot(cm2, jnp.concatenate([hi[r], lo[r]], axis=0)))
    yield "prep"

    row = lax.broadcasted_iota(jnp.int32, (CHUNK, PW), 0)
    upper = {L: jnp.bitwise_and(row, 2 * L - 1) >= L for L in LEVELS}
    sign = {L: jnp.where(upper[L], 1.0, -1.0) for L in LEVELS if 1 < L < 8}
    up_f = {L: jnp.where(upper[L], 1.0, 0.0) for L in LEVELS if L < 8}
    lo_f = {L: jnp.where(upper[L], 0.0, 1.0) for L in LEVELS if L < 8}
    out_row = lax.broadcasted_iota(jnp.int32, (CHUNK, DK), 0)
    out_col = jnp.bitwise_and(lax.broadcasted_iota(jnp.int32, (CHUNK, DK), 1), CHUNK - 1)
    same_block = {L: jnp.where(jnp.bitwise_and(jnp.bitwise_xor(out_row, out_col), -2 * L) == 0, 1.0, 0.0)
                  for L in LEVELS[1:]}
    diagonal = jnp.where(out_row == out_col, 1.0, 0.0)
    first_head = lax.broadcasted_iota(jnp.int32, (CHUNK, DK), 1) < CHUNK

    def ref_rows(a, first, period, width):
        parts = [jnp.broadcast_to(a[r:r + 1, :], (width, a.shape[1])) for r in range(first, CHUNK, period)]
        return parts[0] if len(parts) == 1 else jnp.concatenate(parts, axis=0)

    def level_ref(b, L):
        if L >= 4:
            return ref_rows(b, L - 1, 2 * L, 2 * L)
        return jnp.where(jnp.bitwise_and(row, 7) < 4, ref_rows(b, 1, 8, 8), ref_rows(b, 5, 8, 8))

    def pair_dot(qe, ke):
        rhs = _block_diag(ke[:, :DK], ke[:, DK:]).T.astype(BF16)
        return _dot(qe.astype(BF16), rhs)

    def value_dot(entry):
        level_scores, v_bf = entry[5], entry[6]
        scores = level_scores[-1]
        for L, s_l in zip(LEVELS[1:], level_scores[1:-1]):
            scores = scores + s_l * same_block[L]
        scores = scores + level_scores[0]
        return _dot(scores.astype(BF16), _block_diag(v_bf[:, :DK], v_bf[:, DK:]))

    work = []
    o_intra = []
    for ci in range(n_chunks):
        r = slice(ci * CHUNK, (ci + 1) * CHUNK)
        for hp in range(HEADS // 2):
            pc = slice(hp * PW, (hp + 1) * PW)
            q, kk, b = q_all[r, pc], kk_all[r, pc], b_chunks[ci][:, pc]
            b_last = b[CHUNK - 1:CHUNK, :]
            v_bf = z_ref[r, 2 * KW + hp * PW:2 * KW + (hp + 1) * PW].astype(BF16)

            qe_bf = (q * jnp.exp2(b)).astype(BF16)
            k_dec = (kk * jnp.exp2(b_last - b)).astype(BF16)
            upd = [_dot_tn(v_bf[:, i * DK:(i + 1) * DK], k_dec[:, i * DK:(i + 1) * DK]) for i in range(2)]

            level_scores = []
            for L in LEVELS:
                if L >= 8:
                    ref = level_ref(b, L)
                    zeros = jnp.zeros((L, PW), F32)
                    q_runs, k_runs = [], []
                    for r0 in range(0, CHUNK, L):
                        run = slice(r0, r0 + L)
                        if (r0 // L) % 2:
                            q_runs.append(q[run] * jnp.exp2(b[run] - ref[run]))
                            k_runs.append(zeros)
                        else:
                            q_runs.append(zeros)
                            k_runs.append(kk[run] * jnp.exp2(ref[run] - b[run]))
                    level_scores.append(pair_dot(jnp.concatenate(q_runs, axis=0),
                                                 jnp.concatenate(k_runs, axis=0)))
                    continue
                if L == 1:
                    qe, ke = q * f[r, pc] * up_f[L], kk * lo_f[L]
                else:
                    e = jnp.exp2((b - level_ref(b, L)) * sign[L])
                    qe, ke = q * up_f[L] * e, kk * lo_f[L] * e
                level_scores.append(pair_dot(qe, ke))
            qk = q * kk
            per_row = jnp.where(first_head, jnp.sum(qk[:, :DK], axis=-1, keepdims=True),
                                jnp.sum(qk[:, DK:], axis=-1, keepdims=True))
            level_scores.append(per_row * diagonal)
            work.append((ci, hp, qe_bf, jnp.exp2(b_last), upd, level_scores, v_bf))
            if len(work) > 1:
                o_intra.append(value_dot(work[-2]))
            yield "levels"
    o_intra.append(value_dot(work[-1]))
    yield "levels"

    state = [st_ref[h] for h in range(HEADS)]
    o_inter = []
    for ci, hp, qe_bf, decay, upd, level_scores, v_bf in work:
        h0, h1 = 2 * hp, 2 * hp + 1
        o_inter.append(_dot_nt(qe_bf, _block_diag(state[h0].astype(BF16), state[h1].astype(BF16))))
        state[h0] = state[h0] * decay[:, :DK] + upd[0]
        state[h1] = state[h1] * decay[:, DK:] + upd[1]
    for h in range(HEADS):
        st_ref[h] = state[h]
    yield "state"

    for (ci, hp, *_), oa, ob in zip(work, o_inter, o_intra):
        r = slice(ci * CHUNK, (ci + 1) * CHUNK)
        o_pair = ob + oa
        for i in range(2):
            cols = slice((2 * hp + i) * DK, (2 * hp + i + 1) * DK)
            o = o_pair[:, i * DK:(i + 1) * DK]
            o = o * lax.rsqrt(jnp.mean(o * o, axis=-1, keepdims=True) + EPS) * hg_all[:, cols]
            mixed_ref[r, cols] = o * gate_all[r, cols]
        yield "norm"


def _mix_kernel(xn_ref, xp_ref, g_ref, w_in_f32, lbl_ref, hg_ref, w_pool_f32, ps_ref, w_out_f32, cm_ref,
                out_ref, za_ref, zb_ref, st_ref, pext_ref, mixed_ref, w_in_ref, w_pool_ref, w_out_ref,
                *, tiles_per_seq):
    g = pl.program_id(0)
    si = lax.rem(g + tiles_per_seq - 1, tiles_per_seq)

    @pl.when(g == 0)
    def _():
        w_in_ref[...] = w_in_f32[...].astype(BF16)
        w_pool_ref[...] = w_pool_f32[...].astype(BF16)
        w_out_ref[...] = w_out_f32[...].astype(BF16)
        za_ref[...] = _dot(_rms(xn_ref[0], g_ref[...]).astype(BF16), w_in_ref[...])

    @pl.when(si == 0)
    def _():
        st_ref[...] = jnp.zeros_like(st_ref)
        pext_ref[0:POOL_HIST, :] = jnp.zeros((POOL_HIST, KW), F32)

    args = (xn_ref, xp_ref, g_ref, w_in_ref, lbl_ref, hg_ref, w_pool_ref, ps_ref, w_out_ref, cm_ref,
            out_ref, st_ref, pext_ref, mixed_ref, si)

    @pl.when(jnp.logical_and(lax.rem(g, 2) == 0, g > 0))
    def _():
        _mix_step(*args, zn_ref=za_ref, zc_ref=zb_ref)

    @pl.when(lax.rem(g, 2) == 1)
    def _():
        _mix_step(*args, zn_ref=zb_ref, zc_ref=za_ref)


def _mix_step(xn_ref, xp_ref, g_ref, w_in_ref, lbl_ref, hg_ref, w_pool_ref, ps_ref, w_out_ref, cm_ref,
              out_ref, st_ref, pext_ref, mixed_ref, si, *, zn_ref, zc_ref):
    ts = xn_ref.shape[1]
    d = xn_ref.shape[2]
    n_sub = ts // SUB_TILE
    hn_bf = _rms(xn_ref[0], g_ref[...]).astype(BF16)

    def in_proj_unit(blk):
        def run():
            cols = slice(blk * IN_BLOCK, (blk + 1) * IN_BLOCK)
            zn_ref[:, cols] = _dot(hn_bf, w_in_ref[:, cols])
        return run

    def out_proj_unit(sub, blk):
        def run():
            r = slice(sub * SUB_TILE, (sub + 1) * SUB_TILE)
            cols = slice(blk * IN_BLOCK, (blk + 1) * IN_BLOCK)
            out_ref[0, r, cols] = xp_ref[0, r, cols] + _dot(mixed_ref[r, :].astype(BF16), w_out_ref[:, cols])
        return run

    in_units = [in_proj_unit(blk) for blk in range(IN_WIDTH // IN_BLOCK)]

    l0 = lbl_ref[0:1, :]
    l1 = lbl_ref[1:2, :]
    lmax = jnp.maximum(l0, l1)
    e0 = jnp.exp(l0 - lmax)
    lb_all = e0 / (e0 + jnp.exp(l1 - lmax))

    def pool(sub):
        r = slice(sub * SUB_TILE, (sub + 1) * SUB_TILE)
        p = zc_ref[r, 4 * KW:5 * KW]
        pext_ref[POOL_HIST:, :] = p
        pos = (si * ts + sub * SUB_TILE + 1 + lax.broadcasted_iota(jnp.int32, (SUB_TILE, 1), 0)).astype(F32)
        for gi, w in enumerate(POOL_WINDOWS):
            cols = slice(gi * DK, (gi + 1) * DK)
            acc = pext_ref[:, cols]
            sh = 1
            while sh < w:
                acc = acc + pltpu.roll(acc, sh, 0)
                sh *= 2
            win = acc[POOL_HIST:, :]
            pooled = win / jnp.minimum(pos, float(w)) - p[:, cols]
            y = _dot(pooled.astype(BF16), w_pool_ref[gi]) * ps_ref[:, cols]
            mixed_ref[r, KW + gi * DK:KW + (gi + 1) * DK] = y
        pext_ref[0:POOL_HIST, :] = pext_ref[SUB_TILE:SUB_TILE + POOL_HIST, :]

    def hgrn(sub):
        r = pl.ds(sub * SUB_TILE, SUB_TILE)
        return _hgrn_tile(zc_ref.at[r], SUB_TILE, cm_ref[...], lb_all, hg_ref[...], st_ref, mixed_ref.at[r])

    def step(gen, label):
        got = next(gen)
        assert got == label, (got, label)

    def fill(n=1):
        for _ in range(n):
            if in_units:
                in_units.pop(0)()

    n_items = (SUB_TILE // CHUNK) * (HEADS // 2)
    gens = [hgrn(sub) for sub in range(n_sub)]
    slot_every = max(1, n_items // 4)
    out_units = []
    step(gens[0], "prep")
    fill(3)
    for i in range(n_items + 1):
        step(gens[0], "levels")
        if i % slot_every == slot_every - 1:
            fill()
    for sub in range(n_sub):
        cur = gens[sub]
        nxt = gens[sub + 1] if sub + 1 < n_sub else None
        if nxt is not None:
            step(nxt, "prep")
        step(cur, "state")
        for i in range(n_items):
            if nxt is not None:
                step(nxt, "levels")
            step(cur, "norm")
            if i % slot_every == slot_every - 1:
                (out_units.pop(0) if out_units else fill)()
        if nxt is not None:
            step(nxt, "levels")
        pool(sub)
        out_units += [out_proj_unit(sub, blk) for blk in range(d // IN_BLOCK)]
    fill(len(in_units))
    for unit in out_units:
        unit()


def _const_spec(shape):
    nd = len(shape)
    return pl.BlockSpec(shape, lambda *_: (0,) * nd, pipeline_mode=pl.Buffered(1))


def _mix_call(x, g, w_in, lbl, hg, w_pool, ps, w_out):
    bsz, seq, d = x.shape
    ts = TS_MIX
    ns = seq // ts
    n_tiles = bsz * ns
    cm = jnp.asarray(_cumsum_matrix(), BF16)

    def tile_next(i):
        t = jnp.minimum(i, n_tiles - 1)
        return (t // ns, t % ns, 0)

    def tile_prev(i):
        t = jnp.maximum(i - 1, 0)
        return (t // ns, t % ns, 0)

    return pl.pallas_call(
        functools.partial(_mix_kernel, tiles_per_seq=ns),
        grid=(n_tiles + 1,),
        in_specs=[
            pl.BlockSpec((1, ts, d), tile_next),
            pl.BlockSpec((1, ts, d), tile_prev),
            _const_spec((1, d)),
            _const_spec((d, IN_WIDTH)),
            _const_spec((2, KW)),
            _const_spec((1, KW)),
            _const_spec((len(POOL_WINDOWS), DK, DK)),
            _const_spec((1, KW)),
            _const_spec((d, d)),
            _const_spec((CHUNK, CHUNK)),
        ],
        out_specs=pl.BlockSpec((1, ts, d), tile_prev),
        out_shape=jax.ShapeDtypeStruct(x.shape, F32),
        scratch_shapes=[
            pltpu.VMEM((ts, IN_WIDTH), F32),
            pltpu.VMEM((ts, IN_WIDTH), F32),
            pltpu.VMEM((HEADS, DK, DK), F32),
            pltpu.VMEM((POOL_HIST + SUB_TILE, KW), F32),
            pltpu.VMEM((ts, d), F32),
            pltpu.VMEM((d, IN_WIDTH), BF16),
            pltpu.VMEM((len(POOL_WINDOWS), DK, DK), BF16),
            pltpu.VMEM((d, d), BF16),
        ],
        compiler_params=pltpu.CompilerParams(
            dimension_semantics=("arbitrary",), vmem_limit_bytes=VMEM_LIMIT),
        name="mix",
    )(x, x, g, w_in, lbl, hg, w_pool, ps, w_out, cm)


def _memkv_kernel(mem_ref, g_ref, wq_ref, wk_ref, wv_ref, kt_ref, v_ref, wq_out_ref, wk_bf, wv_bf):
    def dense(w_ref):
        return jnp.concatenate([w_ref[:, h, :] for h in range(XHEADS)], axis=1).astype(BF16)

    @pl.when(pl.program_id(0) == 0)
    def _():
        wq_out_ref[...] = dense(wq_ref)
        wk_bf[...] = dense(wk_ref)
        wv_bf[...] = dense(wv_ref)

    hm = _rms(mem_ref[0], g_ref[...]).astype(BF16)
    k = _dot(hm, wk_bf[...]) * (XDIM ** -0.5)
    kt_ref[0] = k.T.astype(BF16)
    v_ref[0] = _dot(hm, wv_bf[...]).astype(BF16)


def _memkv_call(mem, g, wq, wk, wv):
    bsz, m, d = mem.shape
    return pl.pallas_call(
        _memkv_kernel,
        grid=(bsz,),
        in_specs=[
            pl.BlockSpec((1, m, d), lambda b: (b, 0, 0)),
            _const_spec((1, d)),
            _const_spec((d, XHEADS, XDIM)),
            _const_spec((d, XHEADS, XDIM)),
            _const_spec((d, XHEADS, XDIM)),
        ],
        out_specs=[
            pl.BlockSpec((1, d, m), lambda b: (b, 0, 0)),
            pl.BlockSpec((1, m, d), lambda b: (b, 0, 0)),
            pl.BlockSpec((d, d), lambda b: (0, 0)),
        ],
        out_shape=[
            jax.ShapeDtypeStruct((bsz, d, m), BF16),
            jax.ShapeDtypeStruct((bsz, m, d), BF16),
            jax.ShapeDtypeStruct((d, d), BF16),
        ],
        scratch_shapes=[pltpu.VMEM((d, d), BF16), pltpu.VMEM((d, d), BF16)],
        compiler_params=pltpu.CompilerParams(
            dimension_semantics=("arbitrary",), vmem_limit_bytes=VMEM_LIMIT),
        name="memkv",
    )(mem, g, wq, wk, wv)


def _tail_kernel(x_ref, gx_ref, wq_ref, kt_ref, v_ref, wo_ref, gf_ref, w1_ref, w2_ref, gl_ref, out_ref):
    hm = x_ref.shape[1] // 2
    head_cols = [slice(h * XDIM, (h + 1) * XDIM) for h in range(XHEADS)]
    rows = [slice(i * hm, (i + 1) * hm) for i in range(2)]
    xs = [x_ref[0, r, :] for r in rows]

    def q_proj(x):
        return _dot(_rms(x, gx_ref[...]).astype(BF16), wq_ref[...]).astype(BF16)

    def scores(xq):
        return [_dot(xq[:, c], kt_ref[0, c, :]) for c in head_cols]

    def attend(ss):
        outs = []
        for s, c in zip(ss, head_cols):
            e = jnp.exp(s - jnp.max(s, axis=-1, keepdims=True))
            p = e * (1.0 / jnp.sum(e, axis=-1, keepdims=True))
            outs.append(_dot(p.astype(BF16), v_ref[0, :, c]).astype(BF16))
        return jnp.concatenate(outs, axis=1)

    def o_proj(x, att):
        return x + _dot(att, wo_ref[...].astype(BF16))

    def mlp(i, x):
        hf = _rms(x, gf_ref[...]).astype(BF16)
        acc = x
        for j in range(D_FF // FF_BLOCK):
            cols = slice(j * FF_BLOCK, (j + 1) * FF_BLOCK)
            u = jnp.maximum(_dot(hf, w1_ref[:, cols].astype(BF16)), 0.0)
            acc = acc + _dot((u * u).astype(BF16), w2_ref[cols, :].astype(BF16))
        out_ref[0, rows[i], :] = _rms(acc, gl_ref[...])

    s_a = scores(q_proj(xs[0]))
    xq_b = q_proj(xs[1])
    att_a = attend(s_a)
    s_b = scores(xq_b)
    x2_a = o_proj(xs[0], att_a)
    x2_b = o_proj(xs[1], attend(s_b))
    mlp(0, x2_a)
    mlp(1, x2_b)


def _tail_call(x, gx, wq, kt, v, wo, gf, w1, w2, gl):
    bsz, seq, d = x.shape
    tm = TM_TAIL
    return pl.pallas_call(
        _tail_kernel,
        grid=(bsz, seq // tm),
        in_specs=[
            pl.BlockSpec((1, tm, d), lambda b, s: (b, s, 0)),
            _const_spec((1, d)),
            _const_spec((d, d)),
            pl.BlockSpec((1, d, MEM_LEN), lambda b, s: (b, 0, 0)),
            pl.BlockSpec((1, MEM_LEN, d), lambda b, s: (b, 0, 0)),
            _const_spec((d, d)),
            _const_spec((1, d)),
            _const_spec((d, D_FF)),
            _const_spec((D_FF, d)),
            _const_spec((1, d)),
        ],
        out_specs=pl.BlockSpec((1, tm, d), lambda b, s: (b, s, 0)),
        out_shape=jax.ShapeDtypeStruct(x.shape, F32),
        compiler_params=pltpu.CompilerParams(
            dimension_semantics=("arbitrary", "arbitrary"), vmem_limit_bytes=TAIL_VMEM_LIMIT),
        name="tail",
    )(x, gx, wq, kt, v, wo, gf, w1, w2, gl)


def kernel(x, mem, norm_mix_g, w_in, lb_logits, hgrn_norm_g, w_pool, pool_scale, w_out,
           norm_x_g, norm_mem_g, w_xq, w_xk, w_xv, w_xo, norm_ffn_g, w_ff1, w_ff2, final_norm_g):
    bsz, seq, d = x.shape
    depth = norm_mix_g.shape[0]
    assert depth == 1 and lb_logits.shape[0] == 2
    assert seq % TS_MIX == 0 and TS_MIX % SUB_TILE == 0 and seq % TM_TAIL == 0
    assert d == 2 * KW and w_ff1.shape[-1] == D_FF and mem.shape[1] == MEM_LEN
    row = lambda a: a.reshape(1, -1).astype(F32)
    x = _mix_call(x, row(norm_mix_g), w_in.reshape(d, IN_WIDTH), lb_logits.astype(F32),
                  row(hgrn_norm_g), w_pool.reshape(len(POOL_WINDOWS), DK, DK), row(pool_scale),
                  w_out.reshape(d, d))
    split = (d, XHEADS, XDIM)
    kt, v, wq = _memkv_call(mem, row(norm_mem_g), w_xq.reshape(split), w_xk.reshape(split), w_xv.reshape(split))
    return _tail_call(x, row(norm_x_g), wq, kt, v, w_xo.reshape(d, d),
                      row(norm_ffn_g), w_ff1.reshape(d, D_FF), w_ff2.reshape(D_FF, d), row(final_norm_g))
```

```python
import functools

import jax
import jax.numpy as jnp
import numpy as np
from jax import lax
from jax.experimental import pallas as pl
from jax.experimental.pallas import tpu as pltpu

HEADS = 4
DK = 128
KW = HEADS * DK
PW = 2 * DK
CHUNK = 64
LEVELS = (32, 16, 8, 4, 2, 1)
POOL_WINDOWS = (2, 4, 8, 16)
POOL_HIST = 16
IN_WIDTH = 5 * KW
MEM_LEN = 256
XHEADS = 4
XDIM = 256
D_FF = 4096
EPS = 1e-6
LOG2E = 1.4426950408889634

TS_MIX = 512
SUB_TILE = 256
IN_BLOCK = 256
TM_TAIL = 512
FF_BLOCK = 1024
VMEM_LIMIT = 56 * 1024 * 1024
TAIL_VMEM_LIMIT = 60 * 1024 * 1024

F32 = jnp.float32
BF16 = jnp.bfloat16


def _rms(x, g):
    return x * lax.rsqrt(jnp.mean(x * x, axis=-1, keepdims=True) + EPS) * g


def _dot(a, b):
    return jnp.dot(a, b, preferred_element_type=F32)


def _dot_nt(a, b):
    return lax.dot_general(a, b, (((1,), (1,)), ((), ())), preferred_element_type=F32)


def _dot_tn(a, b):
    return lax.dot_general(a, b, (((0,), (0,)), ((), ())), preferred_element_type=F32)


def _sigmoid(x):
    return 1.0 / (1.0 + jnp.exp(-x))


def _block_diag(a, b):
    z = jnp.zeros_like(a)
    return jnp.concatenate([jnp.concatenate([a, z], axis=1), jnp.concatenate([z, b], axis=1)], axis=0)


def _cumsum_matrix():
    t = np.arange(CHUNK)[:, None]
    s = np.arange(CHUNK)[None, :]
    return (s <= t).astype(np.float32)


def _hgrn_tile(z_ref, ts, cm, lb_all, hg_all, st_ref, mixed_ref):
    n_chunks = ts // CHUNK
    qp = z_ref[:, 0:KW]
    fp = z_ref[:, KW:2 * KW]
    gp = z_ref[:, 3 * KW:4 * KW]

    f = lb_all + (1.0 - lb_all) * _sigmoid(fp)
    kk_all = 1.0 - f
    q_all = qp * _sigmoid(qp)
    gate_all = gp * _sigmoid(gp)

    lf2 = jnp.log(f) * LOG2E
    hi = lf2.astype(BF16)
    lo = (lf2 - hi.astype(F32)).astype(BF16)
    cm2 = jnp.concatenate([cm, cm], axis=1)
    b_chunks = []
    for ci in range(n_chunks):
        r = slice(ci * CHUNK, (ci + 1) * CHUNK)
        b_chunks.append(_dot(cm2, jnp.concatenate([hi[r], lo[r]], axis=0)))
    yield "prep"

    row = lax.broadcasted_iota(jnp.int32, (CHUNK, PW), 0)
    upper = {L: jnp.bitwise_and(row, 2 * L - 1) >= L for L in LEVELS}
    sign = {L: jnp.where(upper[L], 1.0, -1.0) for L in LEVELS if 1 < L < 8}
    up_f = {L: jnp.where(upper[L], 1.0, 0.0) for L in LEVELS if L < 8}
    lo_f = {L: jnp.where(upper[L], 0.0, 1.0) for L in LEVELS if L < 8}
    out_row = lax.broadcasted_iota(jnp.int32, (CHUNK, DK), 0)
    out_col = jnp.bitwise_and(lax.broadcasted_iota(jnp.int32, (CHUNK, DK), 1), CHUNK - 1)
    same_block = {L: jnp.where(jnp.bitwise_and(jnp.bitwise_xor(out_row, out_col), -2 * L) == 0, 1.0, 0.0)
                  for L in LEVELS[1:]}
    diagonal = jnp.where(out_row == out_col, 1.0, 0.0)
    first_head = lax.broadcasted_iota(jnp.int32, (CHUNK, DK), 1) < CHUNK
    odd_subdiagonal = jnp.where(jnp.logical_and(out_col == out_row - 1, jnp.bitwise_and(out_row, 1) == 1), 1.0, 0.0)

    def ref_rows(a, first, period, width):
        parts = [jnp.broadcast_to(a[r:r + 1, :], (width, a.shape[1])) for r in range(first, CHUNK, period)]
        return parts[0] if len(parts) == 1 else jnp.concatenate(parts, axis=0)

    def level_ref(b, L):
        if L >= 4:
            return ref_rows(b, L - 1, 2 * L, 2 * L)
        return jnp.where(jnp.bitwise_and(row, 7) < 4, ref_rows(b, 1, 8, 8), ref_rows(b, 5, 8, 8))

    def pair_dot(qe, ke):
        rhs = _block_diag(ke[:, :DK], ke[:, DK:]).T.astype(BF16)
        return _dot(qe.astype(BF16), rhs)

    def value_dot(entry):
        level_scores, v_bf = entry[5], entry[6]
        scores = level_scores[-1]
        for L, s_l in zip(LEVELS[1:-1], level_scores[1:-1]):
            scores = scores + s_l * same_block[L]
        scores = scores + level_scores[0]
        return _dot(scores.astype(BF16), _block_diag(v_bf[:, :DK], v_bf[:, DK:]))

    work = []
    o_intra = []
    for ci in range(n_chunks):
        r = slice(ci * CHUNK, (ci + 1) * CHUNK)
        for hp in range(HEADS // 2):
            pc = slice(hp * PW, (hp + 1) * PW)
            q, kk, b = q_all[r, pc], kk_all[r, pc], b_chunks[ci][:, pc]
            b_last = b[CHUNK - 1:CHUNK, :]
            v_bf = z_ref[r, 2 * KW + hp * PW:2 * KW + (hp + 1) * PW].astype(BF16)

            qe_bf = (q * jnp.exp2(b)).astype(BF16)
            k_dec = (kk * jnp.exp2(b_last - b)).astype(BF16)
            upd = [_dot_tn(v_bf[:, i * DK:(i + 1) * DK], k_dec[:, i * DK:(i + 1) * DK]) for i in range(2)]

            level_scores = []
            for L in LEVELS:
                if L >= 8:
                    ref = level_ref(b, L)
                    zeros = jnp.zeros((L, PW), F32)
                    q_runs, k_runs = [], []
                    for r0 in range(0, CHUNK, L):
                        run = slice(r0, r0 + L)
                        if (r0 // L) % 2:
                            q_runs.append(q[run] * jnp.exp2(b[run] - ref[run]))
                            k_runs.append(zeros)
                        else:
                            q_runs.append(zeros)
                            k_runs.append(kk[run] * jnp.exp2(ref[run] - b[run]))
                    level_scores.append(pair_dot(jnp.concatenate(q_runs, axis=0),
                                                 jnp.concatenate(k_runs, axis=0)))
                    continue
                if L == 1:
                    continue
                e = jnp.exp2((b - level_ref(b, L)) * sign[L])
                level_scores.append(pair_dot(q * up_f[L] * e, kk * lo_f[L] * e))

            def row_sums(a):
                return jnp.where(first_head, jnp.sum(a[:, :DK], axis=-1, keepdims=True),
                                 jnp.sum(a[:, DK:], axis=-1, keepdims=True))

            level_scores.append(row_sums(q * kk) * diagonal
                                + row_sums(q * f[r, pc] * pltpu.roll(kk, 1, 0)) * odd_subdiagonal)
            work.append((ci, hp, qe_bf, jnp.exp2(b_last), upd, level_scores, v_bf))
            if len(work) > 1:
                o_intra.append(value_dot(work[-2]))
            yield "levels"
    o_intra.append(value_dot(work[-1]))
    yield "levels"

    state = [st_ref[h] for h in range(HEADS)]
    o_inter = []
    for ci, hp, qe_bf, decay, upd, level_scores, v_bf in work:
        h0, h1 = 2 * hp, 2 * hp + 1
        o_inter.append(_dot_nt(qe_bf, _block_diag(state[h0].astype(BF16), state[h1].astype(BF16))))
        state[h0] = state[h0] * decay[:, :DK] + upd[0]
        state[h1] = state[h1] * decay[:, DK:] + upd[1]
    for h in range(HEADS):
        st_ref[h] = state[h]
    yield "state"

    for (ci, hp, *_), oa, ob in zip(work, o_inter, o_intra):
        r = slice(ci * CHUNK, (ci + 1) * CHUNK)
        o_pair = ob + oa
        for i in range(2):
            cols = slice((2 * hp + i) * DK, (2 * hp + i + 1) * DK)
            o = o_pair[:, i * DK:(i + 1) * DK]
            o = o * lax.rsqrt(jnp.mean(o * o, axis=-1, keepdims=True) + EPS) * hg_all[:, cols]
            mixed_ref[r, cols] = o * gate_all[r, cols]
        yield "norm"


def _mix_kernel(xn_ref, xp_ref, g_ref, w_in_f32, lbl_ref, hg_ref, w_pool_f32, ps_ref, w_out_f32, cm_ref,
                out_ref, za_ref, zb_ref, st_ref, pext_ref, mixed_ref, w_in_ref, w_pool_ref, w_out_ref,
                *, tiles_per_seq):
    g = pl.program_id(0)
    si = lax.rem(g + tiles_per_seq - 1, tiles_per_seq)

    @pl.when(g == 0)
    def _():
        w_in_ref[...] = w_in_f32[...].astype(BF16)
        w_pool_ref[...] = w_pool_f32[...].astype(BF16)
        w_out_ref[...] = w_out_f32[...].astype(BF16)
        za_ref[...] = _dot(_rms(xn_ref[0], g_ref[...]).astype(BF16), w_in_ref[...])

    @pl.when(si == 0)
    def _():
        st_ref[...] = jnp.zeros_like(st_ref)
        pext_ref[0:POOL_HIST, :] = jnp.zeros((POOL_HIST, KW), F32)

    args = (xn_ref, xp_ref, g_ref, w_in_ref, lbl_ref, hg_ref, w_pool_ref, ps_ref, w_out_ref, cm_ref,
            out_ref, st_ref, pext_ref, mixed_ref, si)

    @pl.when(jnp.logical_and(lax.rem(g, 2) == 0, g > 0))
    def _():
        _mix_step(*args, zn_ref=za_ref, zc_ref=zb_ref)

    @pl.when(lax.rem(g, 2) == 1)
    def _():
        _mix_step(*args, zn_ref=zb_ref, zc_ref=za_ref)


def _mix_step(xn_ref, xp_ref, g_ref, w_in_ref, lbl_ref, hg_ref, w_pool_ref, ps_ref, w_out_ref, cm_ref,
              out_ref, st_ref, pext_ref, mixed_ref, si, *, zn_ref, zc_ref):
    ts = xn_ref.shape[1]
    d = xn_ref.shape[2]
    n_sub = ts // SUB_TILE
    hn_bf = _rms(xn_ref[0], g_ref[...]).astype(BF16)

    def in_proj_unit(blk):
        def run():
            cols = slice(blk * IN_BLOCK, (blk + 1) * IN_BLOCK)
            zn_ref[:, cols] = _dot(hn_bf, w_in_ref[:, cols])
        return run

    def out_proj_unit(sub, blk):
        def run():
            r = slice(sub * SUB_TILE, (sub + 1) * SUB_TILE)
            cols = slice(blk * IN_BLOCK, (blk + 1) * IN_BLOCK)
            out_ref[0, r, cols] = xp_ref[0, r, cols] + _dot(mixed_ref[r, :].astype(BF16), w_out_ref[:, cols])
        return run

    in_units = [in_proj_unit(blk) for blk in range(IN_WIDTH // IN_BLOCK)]

    l0 = lbl_ref[0:1, :]
    l1 = lbl_ref[1:2, :]
    lmax = jnp.maximum(l0, l1)
    e0 = jnp.exp(l0 - lmax)
    lb_all = e0 / (e0 + jnp.exp(l1 - lmax))

    def pool(sub):
        r = slice(sub * SUB_TILE, (sub + 1) * SUB_TILE)
        p = zc_ref[r, 4 * KW:5 * KW]
        pext_ref[POOL_HIST:, :] = p
        pos = (si * ts + sub * SUB_TILE + 1 + lax.broadcasted_iota(jnp.int32, (SUB_TILE, 1), 0)).astype(F32)
        for gi, w in enumerate(POOL_WINDOWS):
            cols = slice(gi * DK, (gi + 1) * DK)
            acc = pext_ref[:, cols]
            sh = 1
            while sh < w:
                acc = acc + pltpu.roll(acc, sh, 0)
                sh *= 2
            win = acc[POOL_HIST:, :]
            pooled = win / jnp.minimum(pos, float(w)) - p[:, cols]
            y = _dot(pooled.astype(BF16), w_pool_ref[gi]) * ps_ref[:, cols]
            mixed_ref[r, KW + gi * DK:KW + (gi + 1) * DK] = y
        pext_ref[0:POOL_HIST, :] = pext_ref[SUB_TILE:SUB_TILE + POOL_HIST, :]

    def hgrn(sub):
        r = pl.ds(sub * SUB_TILE, SUB_TILE)
        return _hgrn_tile(zc_ref.at[r], SUB_TILE, cm_ref[...], lb_all, hg_ref[...], st_ref, mixed_ref.at[r])

    def step(gen, label):
        got = next(gen)
        assert got == label, (got, label)

    def fill(n=1):
        for _ in range(n):
            if in_units:
                in_units.pop(0)()

    n_items = (SUB_TILE // CHUNK) * (HEADS // 2)
    gens = [hgrn(sub) for sub in range(n_sub)]
    slot_every = max(1, n_items // 4)
    out_units = []
    step(gens[0], "prep")
    fill(3)
    for i in range(n_items + 1):
        step(gens[0], "levels")
        if i % slot_every == slot_every - 1:
            fill()
    for sub in range(n_sub):
        cur = gens[sub]
        nxt = gens[sub + 1] if sub + 1 < n_sub else None
        if nxt is not None:
            step(nxt, "prep")
        step(cur, "state")
        for i in range(n_items):
            if nxt is not None:
                step(nxt, "levels")
            step(cur, "norm")
            if i % slot_every == slot_every - 1:
                (out_units.pop(0) if out_units else fill)()
        if nxt is not None:
            step(nxt, "levels")
        pool(sub)
        out_units += [out_proj_unit(sub, blk) for blk in range(d // IN_BLOCK)]
    fill(len(in_units))
    for unit in out_units:
        unit()


def _const_spec(shape):
    nd = len(shape)
    return pl.BlockSpec(shape, lambda *_: (0,) * nd, pipeline_mode=pl.Buffered(1))


def _mix_call(x, g, w_in, lbl, hg, w_pool, ps, w_out):
    bsz, seq, d = x.shape
    ts = TS_MIX
    ns = seq // ts
    n_tiles = bsz * ns
    cm = jnp.asarray(_cumsum_matrix(), BF16)

    def tile_next(i):
        t = jnp.minimum(i, n_tiles - 1)
        return (t // ns, t % ns, 0)

    def tile_prev(i):
        t = jnp.maximum(i - 1, 0)
        return (t // ns, t % ns, 0)

    return pl.pallas_call(
        functools.partial(_mix_kernel, tiles_per_seq=ns),
        grid=(n_tiles + 1,),
        in_specs=[
            pl.BlockSpec((1, ts, d), tile_next),
            pl.BlockSpec((1, ts, d), tile_prev),
            _const_spec((1, d)),
            _const_spec((d, IN_WIDTH)),
            _const_spec((2, KW)),
            _const_spec((1, KW)),
            _const_spec((len(POOL_WINDOWS), DK, DK)),
            _const_spec((1, KW)),
            _const_spec((d, d)),
            _const_spec((CHUNK, CHUNK)),
        ],
        out_specs=pl.BlockSpec((1, ts, d), tile_prev),
        out_shape=jax.ShapeDtypeStruct(x.shape, F32),
        scratch_shapes=[
            pltpu.VMEM((ts, IN_WIDTH), F32),
            pltpu.VMEM((ts, IN_WIDTH), F32),
            pltpu.VMEM((HEADS, DK, DK), F32),
            pltpu.VMEM((POOL_HIST + SUB_TILE, KW), F32),
            pltpu.VMEM((ts, d), F32),
            pltpu.VMEM((d, IN_WIDTH), BF16),
            pltpu.VMEM((len(POOL_WINDOWS), DK, DK), BF16),
            pltpu.VMEM((d, d), BF16),
        ],
        compiler_params=pltpu.CompilerParams(
            dimension_semantics=("arbitrary",), vmem_limit_bytes=VMEM_LIMIT),
        name="mix",
    )(x, x, g, w_in, lbl, hg, w_pool, ps, w_out, cm)


def _memkv_kernel(mem_ref, g_ref, wq_ref, wk_ref, wv_ref, kt_ref, v_ref, wq_out_ref, wk_bf, wv_bf):
    def dense(w_ref):
        return jnp.concatenate([w_ref[:, h, :] for h in range(XHEADS)], axis=1).astype(BF16)

    @pl.when(pl.program_id(0) == 0)
    def _():
        wq_out_ref[...] = dense(wq_ref)
        wk_bf[...] = dense(wk_ref)
        wv_bf[...] = dense(wv_ref)

    hm = _rms(mem_ref[0], g_ref[...]).astype(BF16)
    k = _dot(hm, wk_bf[...]) * (XDIM ** -0.5)
    kt_ref[0] = k.T.astype(BF16)
    v_ref[0] = _dot(hm, wv_bf[...]).astype(BF16)


def _memkv_call(mem, g, wq, wk, wv):
    bsz, m, d = mem.shape
    return pl.pallas_call(
        _memkv_kernel,
        grid=(bsz,),
        in_specs=[
            pl.BlockSpec((1, m, d), lambda b: (b, 0, 0)),
            _const_spec((1, d)),
            _const_spec((d, XHEADS, XDIM)),
            _const_spec((d, XHEADS, XDIM)),
            _const_spec((d, XHEADS, XDIM)),
        ],
        out_specs=[
            pl.BlockSpec((1, d, m), lambda b: (b, 0, 0)),
            pl.BlockSpec((1, m, d), lambda b: (b, 0, 0)),
            pl.BlockSpec((d, d), lambda b: (0, 0)),
        ],
        out_shape=[
            jax.ShapeDtypeStruct((bsz, d, m), BF16),
            jax.ShapeDtypeStruct((bsz, m, d), BF16),
            jax.ShapeDtypeStruct((d, d), BF16),
        ],
        scratch_shapes=[pltpu.VMEM((d, d), BF16), pltpu.VMEM((d, d), BF16)],
        compiler_params=pltpu.CompilerParams(
            dimension_semantics=("arbitrary",), vmem_limit_bytes=VMEM_LIMIT),
        name="memkv",
    )(mem, g, wq, wk, wv)


def _tail_kernel(x_ref, gx_ref, wq_ref, kt_ref, v_ref, wo_ref, gf_ref, w1_ref, w2_ref, gl_ref, out_ref):
    hm = x_ref.shape[1] // 2
    head_cols = [slice(h * XDIM, (h + 1) * XDIM) for h in range(XHEADS)]
    rows = [slice(i * hm, (i + 1) * hm) for i in range(2)]
    xs = [x_ref[0, r, :] for r in rows]

    def q_proj(x):
        return _dot(_rms(x, gx_ref[...]).astype(BF16), wq_ref[...]).astype(BF16)

    def scores(xq):
        return [_dot(xq[:, c], kt_ref[0, c, :]) for c in head_cols]

    def attend(ss):
        outs = []
        for s, c in zip(ss, head_cols):
            e = jnp.exp(s - jnp.max(s, axis=-1, keepdims=True))
            p = e * (1.0 / jnp.sum(e, axis=-1, keepdims=True))
            outs.append(_dot(p.astype(BF16), v_ref[0, :, c]).astype(BF16))
        return jnp.concatenate(outs, axis=1)

    def o_proj(x, att):
        return x + _dot(att, wo_ref[...].astype(BF16))

    def mlp(i, x):
        hf = _rms(x, gf_ref[...]).astype(BF16)
        acc = x
        for j in range(D_FF // FF_BLOCK):
            cols = slice(j * FF_BLOCK, (j + 1) * FF_BLOCK)
            u = jnp.maximum(_dot(hf, w1_ref[:, cols].astype(BF16)), 0.0)
            acc = acc + _dot((u * u).astype(BF16), w2_ref[cols, :].astype(BF16))
        out_ref[0, rows[i], :] = _rms(acc, gl_ref[...])

    s_a = scores(q_proj(xs[0]))
    xq_b = q_proj(xs[1])
    att_a = attend(s_a)
    s_b = scores(xq_b)
    x2_a = o_proj(xs[0], att_a)
    x2_b = o_proj(xs[1], attend(s_b))
    mlp(0, x2_a)
    mlp(1, x2_b)


def _tail_call(x, gx, wq, kt, v, wo, gf, w1, w2, gl):
    bsz, seq, d = x.shape
    tm = TM_TAIL
    return pl.pallas_call(
        _tail_kernel,
        grid=(bsz, seq // tm),
        in_specs=[
            pl.BlockSpec((1, tm, d), lambda b, s: (b, s, 0)),
            _const_spec((1, d)),
            _const_spec((d, d)),
            pl.BlockSpec((1, d, MEM_LEN), lambda b, s: (b, 0, 0)),
            pl.BlockSpec((1, MEM_LEN, d), lambda b, s: (b, 0, 0)),
            _const_spec((d, d)),
            _const_spec((1, d)),
            _const_spec((d, D_FF)),
            _const_spec((D_FF, d)),
            _const_spec((1, d)),
        ],
        out_specs=pl.BlockSpec((1, tm, d), lambda b, s: (b, s, 0)),
        out_shape=jax.ShapeDtypeStruct(x.shape, F32),
        compiler_params=pltpu.CompilerParams(
            dimension_semantics=("arbitrary", "arbitrary"), vmem_limit_bytes=TAIL_VMEM_LIMIT),
        name="tail",
    )(x, gx, wq, kt, v, wo, gf, w1, w2, gl)


def kernel(x, mem, norm_mix_g, w_in, lb_logits, hgrn_norm_g, w_pool, pool_scale, w_out,
           norm_x_g, norm_mem_g, w_xq, w_xk, w_xv, w_xo, norm_ffn_g, w_ff1, w_ff2, final_norm_g):
    bsz, seq, d = x.shape
    depth = norm_mix_g.shape[0]
    assert depth == 1 and lb_logits.shape[0] == 2
    assert seq % TS_MIX == 0 and TS_MIX % SUB_TILE == 0 and seq % TM_TAIL == 0
    assert d == 2 * KW and w_ff1.shape[-1] == D_FF and mem.shape[1] == MEM_LEN
    row = lambda a: a.reshape(1, -1).astype(F32)
    x = _mix_call(x, row(norm_mix_g), w_in.reshape(d, IN_WIDTH), lb_logits.astype(F32),
                  row(hgrn_norm_g), w_pool.reshape(len(POOL_WINDOWS), DK, DK), row(pool_scale),
                  w_out.reshape(d, d))
    split = (d, XHEADS, XDIM)
    kt, v, wq = _memkv_call(mem, row(norm_mem_g), w_xq.reshape(split), w_xk.reshape(split), w_xv.reshape(split))
    return _tail_call(x, row(norm_x_g), wq, kt, v, w_xo.reshape(d, d),
                      row(norm_ffn_g), w_ff1.reshape(d, D_FF), w_ff2.reshape(D_FF, d), row(final_norm_g))
```

```python
import functools

import jax
import jax.numpy as jnp
import numpy as np
from jax import lax
from jax.experimental import pallas as pl
from jax.experimental.pallas import tpu as pltpu

HEADS = 4
DK = 128
KW = HEADS * DK
PW = 2 * DK
CHUNK = 64
LEVELS = (32, 16, 8, 4, 2, 1)
POOL_WINDOWS = (2, 4, 8, 16)
POOL_HIST = 16
IN_WIDTH = 5 * KW
MEM_LEN = 256
XHEADS = 4
XDIM = 256
D_FF = 4096
EPS = 1e-6
LOG2E = 1.4426950408889634

TS_MIX = 512
SUB_TILE = 256
IN_BLOCK = 256
TM_TAIL = 512
FF_BLOCK = 1024
VMEM_LIMIT = 56 * 1024 * 1024
TAIL_VMEM_LIMIT = 60 * 1024 * 1024

F32 = jnp.float32
BF16 = jnp.bfloat16


def _rms(x, g):
    return x * lax.rsqrt(jnp.mean(x * x, axis=-1, keepdims=True) + EPS) * g


def _dot(a, b):
    return jnp.dot(a, b, preferred_element_type=F32)


def _dot_nt(a, b):
    return lax.dot_general(a, b, (((1,), (1,)), ((), ())), preferred_element_type=F32)


def _dot_tn(a, b):
    return lax.dot_general(a, b, (((0,), (0,)), ((), ())), preferred_element_type=F32)


def _sigmoid(x):
    return 1.0 / (1.0 + jnp.exp(-x))


def _block_diag(a, b):
    z = jnp.zeros_like(a)
    return jnp.concatenate([jnp.concatenate([a, z], axis=1), jnp.concatenate([z, b], axis=1)], axis=0)


def _cumsum_matrix():
    t = np.arange(CHUNK)[:, None]
    s = np.arange(CHUNK)[None, :]
    return (s <= t).astype(np.float32)


def _hgrn_tile(z_ref, ts, cm, lb_all, hg_all, st_ref, mixed_ref):
    n_chunks = ts // CHUNK
    qp = z_ref[:, 0:KW]
    fp = z_ref[:, KW:2 * KW]
    gp = z_ref[:, 3 * KW:4 * KW]

    f = lb_all + (1.0 - lb_all) * _sigmoid(fp)
    kk_all = 1.0 - f
    q_all = qp * _sigmoid(qp)
    gate_all = gp * _sigmoid(gp)

    lf2 = jnp.log(f) * LOG2E
    hi = lf2.astype(BF16)
    lo = (lf2 - hi.astype(F32)).astype(BF16)
    cm2 = jnp.concatenate([cm, cm], axis=1)
    b_chunks = []
    for ci in range(n_chunks):
        r = slice(ci * CHUNK, (ci + 1) * CHUNK)
        b_chunks.append(_dot(cm2, jnp.concatenate([hi[r], lo[r]], axis=0)))
    yield "prep"

    row = lax.broadcasted_iota(jnp.int32, (CHUNK, PW), 0)
    upper = {L: jnp.bitwise_and(row, 2 * L - 1) >= L for L in LEVELS}
    sign = {L: jnp.where(upper[L], 1.0, -1.0) for L in LEVELS if 1 < L < 8}
    up_f = {L: jnp.where(upper[L], 1.0, 0.0) for L in LEVELS if L < 8}
    lo_f = {L: jnp.where(upper[L], 0.0, 1.0) for L in LEVELS if L < 8}
    out_row = lax.broadcasted_iota(jnp.int32, (CHUNK, DK), 0)
    out_col = jnp.bitwise_and(lax.broadcasted_iota(jnp.int32, (CHUNK, DK), 1), CHUNK - 1)
    same_block = {L: jnp.where(jnp.bitwise_and(jnp.bitwise_xor(out_row, out_col), -2 * L) == 0, 1.0, 0.0)
                  for L in LEVELS[1:]}
    diagonal = jnp.where(out_row == out_col, 1.0, 0.0)
    first_head = lax.broadcasted_iota(jnp.int32, (CHUNK, DK), 1) < CHUNK
    odd_subdiagonal = jnp.where(jnp.logical_and(out_col == out_row - 1, jnp.bitwise_and(out_row, 1) == 1), 1.0, 0.0)

    def ref_rows(a, first, period, width):
        parts = [jnp.broadcast_to(a[r:r + 1, :], (width, a.shape[1])) for r in range(first, CHUNK, period)]
        return parts[0] if len(parts) == 1 else jnp.concatenate(parts, axis=0)

    def level_ref(b, L):
        if L >= 4:
            return ref_rows(b, L - 1, 2 * L, 2 * L)
        return jnp.where(jnp.bitwise_and(row, 7) < 4, ref_rows(b, 1, 8, 8), ref_rows(b, 5, 8, 8))

    def pair_dot(qe, ke):
        rhs = _block_diag(ke[:, :DK], ke[:, DK:]).T.astype(BF16)
        return _dot(qe.astype(BF16), rhs)

    def value_dot(entry):
        level_scores, v_bf = entry[5], entry[6]
        scores = level_scores[-1]
        for L, s_l in zip(LEVELS[1:-1], level_scores[1:-1]):
            scores = scores + s_l * same_block[L]
        scores = scores + level_scores[0]
        return _dot(scores.astype(BF16), _block_diag(v_bf[:, :DK], v_bf[:, DK:]))

    work = []
    o_intra = []
    for ci in range(n_chunks):
        r = slice(ci * CHUNK, (ci + 1) * CHUNK)
        for hp in range(HEADS // 2):
            pc = slice(hp * PW, (hp + 1) * PW)
            q, kk, b = q_all[r, pc], kk_all[r, pc], b_chunks[ci][:, pc]
            b_last = b[CHUNK - 1:CHUNK, :]
            v_bf = z_ref[r, 2 * KW + hp * PW:2 * KW + (hp + 1) * PW].astype(BF16)

            qe_bf = (q * jnp.exp2(b)).astype(BF16)
            k_dec = (kk * jnp.exp2(b_last - b)).astype(BF16)
            upd = [_dot_tn(v_bf[:, i * DK:(i + 1) * DK], k_dec[:, i * DK:(i + 1) * DK]) for i in range(2)]

            level_scores = []
            for L in LEVELS:
                if L >= 8:
                    ref = level_ref(b, L)
                    zeros = jnp.zeros((L, PW), F32)
                    q_runs, k_runs = [], []
                    for r0 in range(0, CHUNK, L):
                        run = slice(r0, r0 + L)
                        if (r0 // L) % 2:
                            q_runs.append(q[run] * jnp.exp2(b[run] - ref[run]))
                            k_runs.append(zeros)
                        else:
                            q_runs.append(zeros)
                            k_runs.append(kk[run] * jnp.exp2(ref[run] - b[run]))
                    level_scores.append(pair_dot(jnp.concatenate(q_runs, axis=0),
                                                 jnp.concatenate(k_runs, axis=0)))
                    continue
                if L == 1:
                    continue
                e = jnp.exp2((b - level_ref(b, L)) * sign[L])
                level_scores.append(pair_dot(q * up_f[L] * e, kk * lo_f[L] * e))

            def row_sums(a):
                return jnp.where(first_head, jnp.sum(a[:, :DK], axis=-1, keepdims=True),
                                 jnp.sum(a[:, DK:], axis=-1, keepdims=True))

            level_scores.append(row_sums(q * kk) * diagonal
                                + row_sums(q * f[r, pc] * pltpu.roll(kk, 1, 0)) * odd_subdiagonal)
            work.append((ci, hp, qe_bf, jnp.exp2(b_last), upd, level_scores, v_bf))
            if len(work) > 1:
                o_intra.append(value_dot(work[-2]))
            yield "levels"
    o_intra.append(value_dot(work[-1]))
    yield "levels"

    state = [st_ref[h] for h in range(HEADS)]
    o_inter = []
    for ci, hp, qe_bf, decay, upd, level_scores, v_bf in work:
        h0, h1 = 2 * hp, 2 * hp + 1
        o_inter.append(_dot_nt(qe_bf, _block_diag(state[h0].astype(BF16), state[h1].astype(BF16))))
        state[h0] = state[h0] * decay[:, :DK] + upd[0]
        state[h1] = state[h1] * decay[:, DK:] + upd[1]
    for h in range(HEADS):
        st_ref[h] = state[h]
    yield "state"

    for (ci, hp, *_), oa, ob in zip(work, o_inter, o_intra):
        r = slice(ci * CHUNK, (ci + 1) * CHUNK)
        o_pair = ob + oa
        for i in range(2):
            cols = slice((2 * hp + i) * DK, (2 * hp + i + 1) * DK)
            o = o_pair[:, i * DK:(i + 1) * DK]
            o = o * lax.rsqrt(jnp.mean(o * o, axis=-1, keepdims=True) + EPS) * hg_all[:, cols]
            mixed_ref[r, cols] = o * gate_all[r, cols]
        yield "norm"


def _mix_kernel(xn_ref, xp_ref, g_ref, w_in_f32, lbl_ref, hg_ref, w_pool_f32, ps_ref, w_out_f32, cm_ref,
                out_ref, za_ref, zb_ref, st_ref, pext_ref, mixed_ref, w_in_ref, w_pool_ref, w_out_ref,
                *, tiles_per_seq):
    g = pl.program_id(0)
    si = lax.rem(g + tiles_per_seq - 1, tiles_per_seq)

    @pl.when(g == 0)
    def _():
        w_in_ref[...] = w_in_f32[...].astype(BF16)
        w_pool_ref[...] = w_pool_f32[...].astype(BF16)
        w_out_ref[...] = w_out_f32[...].astype(BF16)
        za_ref[...] = _dot(_rms(xn_ref[0], g_ref[...]).astype(BF16), w_in_ref[...])

    @pl.when(si == 0)
    def _():
        st_ref[...] = jnp.zeros_like(st_ref)
        pext_ref[0:POOL_HIST, :] = jnp.zeros((POOL_HIST, KW), F32)

    args = (xn_ref, xp_ref, g_ref, w_in_ref, lbl_ref, hg_ref, w_pool_ref, ps_ref, w_out_ref, cm_ref,
            out_ref, st_ref, pext_ref, mixed_ref, si)

    @pl.when(jnp.logical_and(lax.rem(g, 2) == 0, g > 0))
    def _():
        _mix_step(*args, zn_ref=za_ref, zc_ref=zb_ref)

    @pl.when(lax.rem(g, 2) == 1)
    def _():
        _mix_step(*args, zn_ref=zb_ref, zc_ref=za_ref)


def _mix_step(xn_ref, xp_ref, g_ref, w_in_ref, lbl_ref, hg_ref, w_pool_ref, ps_ref, w_out_ref, cm_ref,
              out_ref, st_ref, pext_ref, mixed_ref, si, *, zn_ref, zc_ref):
    ts = xn_ref.shape[1]
    d = xn_ref.shape[2]
    n_sub = ts // SUB_TILE
    hn_bf = _rms(xn_ref[0], g_ref[...]).astype(BF16)

    def in_proj_unit(blk):
        def run():
            cols = slice(blk * IN_BLOCK, (blk + 1) * IN_BLOCK)
            zn_ref[:, cols] = _dot(hn_bf, w_in_ref[:, cols])
        return run

    def out_proj_unit(sub, blk):
        def run():
            r = slice(sub * SUB_TILE, (sub + 1) * SUB_TILE)
            cols = slice(blk * IN_BLOCK, (blk + 1) * IN_BLOCK)
            out_ref[0, r, cols] = xp_ref[0, r, cols] + _dot(mixed_ref[r, :].astype(BF16), w_out_ref[:, cols])
        return run

    in_units = [in_proj_unit(blk) for blk in range(IN_WIDTH // IN_BLOCK)]

    l0 = lbl_ref[0:1, :]
    l1 = lbl_ref[1:2, :]
    lmax = jnp.maximum(l0, l1)
    e0 = jnp.exp(l0 - lmax)
    lb_all = e0 / (e0 + jnp.exp(l1 - lmax))

    def pool(sub):
        r = slice(sub * SUB_TILE, (sub + 1) * SUB_TILE)
        p = zc_ref[r, 4 * KW:5 * KW]
        pext_ref[POOL_HIST:, :] = p
        pos = (si * ts + sub * SUB_TILE + 1 + lax.broadcasted_iota(jnp.int32, (SUB_TILE, 1), 0)).astype(F32)
        pooled = []
        for gi, w in enumerate(POOL_WINDOWS):
            cols = slice(gi * DK, (gi + 1) * DK)
            acc = pext_ref[:, cols]
            sh = 1
            while sh < w:
                acc = acc + pltpu.roll(acc, sh, 0)
                sh *= 2
            win = acc[POOL_HIST:, :]
            pooled.append((win / jnp.minimum(pos, float(w)) - p[:, cols]).astype(BF16))
        for gp in range(len(POOL_WINDOWS) // 2):
            cols = slice(gp * PW, (gp + 1) * PW)
            w_pair = _block_diag(w_pool_ref[2 * gp], w_pool_ref[2 * gp + 1])
            y = _dot(jnp.concatenate(pooled[2 * gp:2 * gp + 2], axis=1), w_pair) * ps_ref[:, cols]
            mixed_ref[r, KW + gp * PW:KW + (gp + 1) * PW] = y
        pext_ref[0:POOL_HIST, :] = pext_ref[SUB_TILE:SUB_TILE + POOL_HIST, :]

    def hgrn(sub):
        r = pl.ds(sub * SUB_TILE, SUB_TILE)
        return _hgrn_tile(zc_ref.at[r], SUB_TILE, cm_ref[...], lb_all, hg_ref[...], st_ref, mixed_ref.at[r])

    def step(gen, label):
        got = next(gen)
        assert got == label, (got, label)

    def fill(n=1):
        for _ in range(n):
            if in_units:
                in_units.pop(0)()

    n_items = (SUB_TILE // CHUNK) * (HEADS // 2)
    gens = [hgrn(sub) for sub in range(n_sub)]
    slot_every = max(1, n_items // 4)
    out_units = []
    step(gens[0], "prep")
    fill(3)
    for i in range(n_items + 1):
        step(gens[0], "levels")
        if i % slot_every == slot_every - 1:
            fill()
    for sub in range(n_sub):
        cur = gens[sub]
        nxt = gens[sub + 1] if sub + 1 < n_sub else None
        if nxt is not None:
            step(nxt, "prep")
        step(cur, "state")
        for i in range(n_items):
            if nxt is not None:
                step(nxt, "levels")
            step(cur, "norm")
            if i % slot_every == slot_every - 1:
                (out_units.pop(0) if out_units else fill)()
        if nxt is not None:
            step(nxt, "levels")
        pool(sub)
        out_units += [out_proj_unit(sub, blk) for blk in range(d // IN_BLOCK)]
    fill(len(in_units))
    for unit in out_units:
        unit()


def _const_spec(shape):
    nd = len(shape)
    return pl.BlockSpec(shape, lambda *_: (0,) * nd, pipeline_mode=pl.Buffered(1))


def _mix_call(x, g, w_in, lbl, hg, w_pool, ps, w_out):
    bsz, seq, d = x.shape
    ts = TS_MIX
    ns = seq // ts
    n_tiles = bsz * ns
    cm = jnp.asarray(_cumsum_matrix(), BF16)

    def tile_next(i):
        t = jnp.minimum(i, n_tiles - 1)
        return (t // ns, t % ns, 0)

    def tile_prev(i):
        t = jnp.maximum(i - 1, 0)
        return (t // ns, t % ns, 0)

    return pl.pallas_call(
        functools.partial(_mix_kernel, tiles_per_seq=ns),
        grid=(n_tiles + 1,),
        in_specs=[
            pl.BlockSpec((1, ts, d), tile_next),
            pl.BlockSpec((1, ts, d), tile_prev),
            _const_spec((1, d)),
            _const_spec((d, IN_WIDTH)),
            _const_spec((2, KW)),
            _const_spec((1, KW)),
            _const_spec((len(POOL_WINDOWS), DK, DK)),
            _const_spec((1, KW)),
            _const_spec((d, d)),
            _const_spec((CHUNK, CHUNK)),
        ],
        out_specs=pl.BlockSpec((1, ts, d), tile_prev),
        out_shape=jax.ShapeDtypeStruct(x.shape, F32),
        scratch_shapes=[
            pltpu.VMEM((ts, IN_WIDTH), F32),
            pltpu.VMEM((ts, IN_WIDTH), F32),
            pltpu.VMEM((HEADS, DK, DK), F32),
            pltpu.VMEM((POOL_HIST + SUB_TILE, KW), F32),
            pltpu.VMEM((ts, d), F32),
            pltpu.VMEM((d, IN_WIDTH), BF16),
            pltpu.VMEM((len(POOL_WINDOWS), DK, DK), BF16),
            pltpu.VMEM((d, d), BF16),
        ],
        compiler_params=pltpu.CompilerParams(
            dimension_semantics=("arbitrary",), vmem_limit_bytes=VMEM_LIMIT),
        name="mix",
    )(x, x, g, w_in, lbl, hg, w_pool, ps, w_out, cm)


def _memkv_kernel(mem_ref, g_ref, wq_ref, wk_ref, wv_ref, kt_ref, v_ref, wq_out_ref, wk_bf, wv_bf):
    def dense(w_ref):
        return jnp.concatenate([w_ref[:, h, :] for h in range(XHEADS)], axis=1).astype(BF16)

    @pl.when(pl.program_id(0) == 0)
    def _():
        wq_out_ref[...] = dense(wq_ref)
        wk_bf[...] = dense(wk_ref)
        wv_bf[...] = dense(wv_ref)

    hm = _rms(mem_ref[0], g_ref[...]).astype(BF16)
    k = _dot(hm, wk_bf[...]) * (XDIM ** -0.5)
    kt_ref[0] = k.T.astype(BF16)
    v_ref[0] = _dot(hm, wv_bf[...]).astype(BF16)


def _memkv_call(mem, g, wq, wk, wv):
    bsz, m, d = mem.shape
    return pl.pallas_call(
        _memkv_kernel,
        grid=(bsz,),
        in_specs=[
            pl.BlockSpec((1, m, d), lambda b: (b, 0, 0)),
            _const_spec((1, d)),
            _const_spec((d, XHEADS, XDIM)),
            _const_spec((d, XHEADS, XDIM)),
            _const_spec((d, XHEADS, XDIM)),
        ],
        out_specs=[
            pl.BlockSpec((1, d, m), lambda b: (b, 0, 0)),
            pl.BlockSpec((1, m, d), lambda b: (b, 0, 0)),
            pl.BlockSpec((d, d), lambda b: (0, 0)),
        ],
        out_shape=[
            jax.ShapeDtypeStruct((bsz, d, m), BF16),
            jax.ShapeDtypeStruct((bsz, m, d), BF16),
            jax.ShapeDtypeStruct((d, d), BF16),
        ],
        scratch_shapes=[pltpu.VMEM((d, d), BF16), pltpu.VMEM((d, d), BF16)],
        compiler_params=pltpu.CompilerParams(
            dimension_semantics=("arbitrary",), vmem_limit_bytes=VMEM_LIMIT),
        name="memkv",
    )(mem, g, wq, wk, wv)


def _tail_kernel(x_ref, gx_ref, wq_ref, kt_ref, v_ref, wo_ref, gf_ref, w1_ref, w2_ref, gl_ref, out_ref):
    hm = x_ref.shape[1] // 2
    head_cols = [slice(h * XDIM, (h + 1) * XDIM) for h in range(XHEADS)]
    rows = [slice(i * hm, (i + 1) * hm) for i in range(2)]
    xs = [x_ref[0, r, :] for r in rows]

    def q_proj(x):
        return _dot(_rms(x, gx_ref[...]).astype(BF16), wq_ref[...]).astype(BF16)

    def scores(xq):
        return [_dot(xq[:, c], kt_ref[0, c, :]) for c in head_cols]

    def attend(ss):
        outs = []
        for s, c in zip(ss, head_cols):
            e = jnp.exp(s - jnp.max(s, axis=-1, keepdims=True))
            p = e * (1.0 / jnp.sum(e, axis=-1, keepdims=True))
            outs.append(_dot(p.astype(BF16), v_ref[0, :, c]).astype(BF16))
        return jnp.concatenate(outs, axis=1)

    def o_proj(x, att):
        return x + _dot(att, wo_ref[...].astype(BF16))

    def mlp(i, x):
        hf = _rms(x, gf_ref[...]).astype(BF16)
        acc = x
        for j in range(D_FF // FF_BLOCK):
            cols = slice(j * FF_BLOCK, (j + 1) * FF_BLOCK)
            u = jnp.maximum(_dot(hf, w1_ref[:, cols].astype(BF16)), 0.0)
            acc = acc + _dot((u * u).astype(BF16), w2_ref[cols, :].astype(BF16))
        out_ref[0, rows[i], :] = _rms(acc, gl_ref[...])

    s_a = scores(q_proj(xs[0]))
    xq_b = q_proj(xs[1])
    att_a = attend(s_a)
    s_b = scores(xq_b)
    x2_a = o_proj(xs[0], att_a)
    x2_b = o_proj(xs[1], attend(s_b))
    mlp(0, x2_a)
    mlp(1, x2_b)


def _tail_call(x, gx, wq, kt, v, wo, gf, w1, w2, gl):
    bsz, seq, d = x.shape
    tm = TM_TAIL
    return pl.pallas_call(
        _tail_kernel,
        grid=(bsz, seq // tm),
        in_specs=[
            pl.BlockSpec((1, tm, d), lambda b, s: (b, s, 0)),
            _const_spec((1, d)),
            _const_spec((d, d)),
            pl.BlockSpec((1, d, MEM_LEN), lambda b, s: (b, 0, 0)),
            pl.BlockSpec((1, MEM_LEN, d), lambda b, s: (b, 0, 0)),
            _const_spec((d, d)),
            _const_spec((1, d)),
            _const_spec((d, D_FF)),
            _const_spec((D_FF, d)),
            _const_spec((1, d)),
        ],
        out_specs=pl.BlockSpec((1, tm, d), lambda b, s: (b, s, 0)),
        out_shape=jax.ShapeDtypeStruct(x.shape, F32),
        compiler_params=pltpu.CompilerParams(
            dimension_semantics=("arbitrary", "arbitrary"), vmem_limit_bytes=TAIL_VMEM_LIMIT),
        name="tail",
    )(x, gx, wq, kt, v, wo, gf, w1, w2, gl)


def kernel(x, mem, norm_mix_g, w_in, lb_logits, hgrn_norm_g, w_pool, pool_scale, w_out,
           norm_x_g, norm_mem_g, w_xq, w_xk, w_xv, w_xo, norm_ffn_g, w_ff1, w_ff2, final_norm_g):
    bsz, seq, d = x.shape
    depth = norm_mix_g.shape[0]
    assert depth == 1 and lb_logits.shape[0] == 2
    assert seq % TS_MIX == 0 and TS_MIX % SUB_TILE == 0 and seq % TM_TAIL == 0
    assert d == 2 * KW and w_ff1.shape[-1] == D_FF and mem.shape[1] == MEM_LEN
    row = lambda a: a.reshape(1, -1).astype(F32)
    x = _mix_call(x, row(norm_mix_g), w_in.reshape(d, IN_WIDTH), lb_logits.astype(F32),
                  row(hgrn_norm_g), w_pool.reshape(len(POOL_WINDOWS), DK, DK), row(pool_scale),
                  w_out.reshape(d, d))
    split = (d, XHEADS, XDIM)
    kt, v, wq = _memkv_call(mem, row(norm_mem_g), w_xq.reshape(split), w_xk.reshape(split), w_xv.reshape(split))
    return _tail_call(x, row(norm_x_g), wq, kt, v, w_xo.reshape(d, d),
                      row(norm_ffn_g), w_ff1.reshape(d, D_FF), w_ff2.reshape(D_FF, d), row(final_norm_g))
```
